```python
import math
import jax
import jax.numpy as jnp
from jax import lax
import numpy as np

D_MODEL = 1024
BATCH = 4
SEQ = 8192
DEPTH = 2
DEC_BATCH = 2
DEC_SEQ = 8192
PAST_LEN = 128

GRID_W = 64
N_HEADS = 16
HEAD_DIM = 64
A_KV_HEADS = 4
A_GROUP = N_HEADS // A_KV_HEADS
ROPE_THETA = 10000.0
AXIS_DIM = HEAD_DIM // 2
NA_WIN_H_MAX = 8
NA_WIN_W = 16
NA_BIAS_H = 2 * NA_WIN_H_MAX - 1
NA_BIAS_W = 2 * NA_WIN_W - 1
Q_BLOCK = 128
NA_Q_BLOCK = GRID_W
D_FF = 4 * D_MODEL
NORM_EPS = 1e-6
N_A_LAYERS = (DEPTH + 1) // 2
N_B_LAYERS = DEPTH // 2

kernel_name = "hybrid_axial_gqa_natten_encoder"


def rms_norm(x, gain):
    xf = x.astype(jnp.float32)
    y = xf * lax.rsqrt(jnp.mean(xf * xf, axis=-1, keepdims=True) + NORM_EPS)
    return (y * gain.astype(jnp.float32)).astype(x.dtype)


def axial_rope_tables(seq):
    t = jnp.arange(seq)
    row = (t // GRID_W).astype(jnp.float32)
    col = (t % GRID_W).astype(jnp.float32)
    inv = ROPE_THETA ** (-jnp.arange(0, AXIS_DIM, 2, dtype=jnp.float32) / AXIS_DIM)
    ang = jnp.concatenate([row[:, None] * inv, col[:, None] * inv], axis=-1)
    return jnp.cos(ang), jnp.sin(ang)


def apply_axial_rope(x, cos, sin):
    b, s, h, _ = x.shape
    half = AXIS_DIM // 2
    xa = x.reshape(b, s, h, 2, AXIS_DIM)
    x1, x2 = xa[..., :half], xa[..., half:]
    c = cos.reshape(s, 1, 2, half)
    sn = sin.reshape(s, 1, 2, half)
    out = jnp.concatenate([x1 * c - x2 * sn, x2 * c + x1 * sn], axis=-1)
    return out.reshape(b, s, h, HEAD_DIM)


def global_axial_gqa(h, w_qkv, q_gain, k_gain, w_o):
    b, s, _ = h.shape
    qkv = h @ w_qkv
    q, k, v = jnp.split(qkv, [N_HEADS * HEAD_DIM, (N_HEADS + A_KV_HEADS) * HEAD_DIM], axis=-1)
    q = q.reshape(b, s, N_HEADS, HEAD_DIM)
    k = k.reshape(b, s, A_KV_HEADS, HEAD_DIM)
    v = v.reshape(b, s, A_KV_HEADS, HEAD_DIM)
    cos, sin = axial_rope_tables(s)
    q = (apply_axial_rope(rms_norm(q, q_gain).astype(jnp.float32), cos, sin) * HEAD_DIM ** -0.5).astype(h.dtype)
    k = apply_axial_rope(rms_norm(k, k_gain).astype(jnp.float32), cos, sin).astype(h.dtype)
    nblk = s // Q_BLOCK
    qb = q.reshape(b, nblk, Q_BLOCK, A_KV_HEADS, A_GROUP, HEAD_DIM).transpose(1, 0, 2, 3, 4, 5)

    def block(qi):
        sc = jnp.einsum('bqkgd,bskd->bkgqs', qi, k, preferred_element_type=jnp.float32)
        p = jax.nn.softmax(sc, axis=-1).astype(v.dtype)
        return jnp.einsum('bkgqs,bskd->bqkgd', p, v)

    o = lax.map(block, qb)
    o = o.transpose(1, 0, 2, 3, 4, 5).reshape(b, s, N_HEADS * HEAD_DIM)
    return o @ w_o


def neighbourhood_indices(rows):
    kh = min(NA_WIN_H_MAX, rows)
    s = rows * GRID_W
    t = jnp.arange(s)
    r = t // GRID_W
    c = t % GRID_W
    r0 = jnp.clip(r - kh // 2, 0, rows - kh)
    c0 = jnp.clip(c - NA_WIN_W // 2, 0, GRID_W - NA_WIN_W)
    kr = r0[:, None] + jnp.arange(kh)[None, :]
    kc = c0[:, None] + jnp.arange(NA_WIN_W)[None, :]
    key_idx = (kr[:, :, None] * GRID_W + kc[:, None, :]).reshape(s, kh * NA_WIN_W)
    dr = kr - r[:, None] + (NA_WIN_H_MAX - 1)
    dc = kc - c[:, None] + (NA_WIN_W - 1)
    bias_idx = (dr[:, :, None] * NA_BIAS_W + dc[:, None, :]).reshape(s, kh * NA_WIN_W)
    return key_idx, bias_idx


def neighbourhood_attention(h, w_qkv, rel_bias, w_o):
    b, s, _ = h.shape
    rows = s // GRID_W
    qkv = h @ w_qkv
    q, k, v = jnp.split(qkv, 3, axis=-1)
    q = q.reshape(b, s, N_HEADS, HEAD_DIM) * HEAD_DIM ** -0.5
    k = k.reshape(b, s, N_HEADS, HEAD_DIM)
    v = v.reshape(b, s, N_HEADS, HEAD_DIM)
    key_idx, bias_idx = neighbourhood_indices(rows)
    kn = key_idx.shape[-1]
    nblk = s // NA_Q_BLOCK
    qb = q.reshape(b, nblk, NA_Q_BLOCK, N_HEADS, HEAD_DIM).swapaxes(0, 1)
    kib = key_idx.reshape(nblk, NA_Q_BLOCK, kn)
    bib = bias_idx.reshape(nblk, NA_Q_BLOCK, kn)
    table = rel_bias.reshape(N_HEADS, NA_BIAS_H * NA_BIAS_W)

    def block(args):
        qi, ki, bi = args
        kg = jnp.take(k, ki, axis=1)
        vg = jnp.take(v, ki, axis=1)
        sc = jnp.einsum('bqhd,bqnhd->bhqn', qi, kg, preferred_element_type=jnp.float32)
        sc = sc + jnp.take(table, bi, axis=1).astype(jnp.float32)
        p = jax.nn.softmax(sc, axis=-1).astype(v.dtype)
        return jnp.einsum('bhqn,bqnhd->bqhd', p, vg)

    o = lax.map(block, (qb, kib, bib))
    o = o.swapaxes(0, 1).reshape(b, s, N_HEADS * HEAD_DIM)
    return o @ w_o


def sq_relu_mlp(h, w_in, w_out):
    u = jax.nn.relu(h @ w_in)
    return (u * u) @ w_out


def trunk(x, norm_mix, norm_mlp, norm_final, a_w_qkv, a_q_norm, a_k_norm, a_w_o,
          b_w_qkv, b_rel_bias, b_w_o, mlp_w_in, mlp_w_out):
    for i in range(DEPTH):
        j = i // 2
        h = rms_norm(x, norm_mix[i])
        if i % 2 == 0:
            x = x + global_axial_gqa(h, a_w_qkv[j], a_q_norm[j], a_k_norm[j], a_w_o[j])
        else:
            x = x + neighbourhood_attention(h, b_w_qkv[j], b_rel_bias[j], b_w_o[j])
        x = x + sq_relu_mlp(rms_norm(x, norm_mlp[i]), mlp_w_in[i], mlp_w_out[i])
    return rms_norm(x, norm_final)


def setup_inputs(seed: int = 0) -> dict:
    key = jax.random.key(seed)
    ks = jax.random.split(key, 16)
    f32 = jnp.float32
    hd = N_HEADS * HEAD_DIM
    a_qkv_w = (N_HEADS + 2 * A_KV_HEADS) * HEAD_DIM
    nrm = lambda k, shape, scale: jax.random.normal(k, shape, f32) * scale
    return {
        "x_prompt": jax.random.normal(ks[0], (BATCH, SEQ, D_MODEL), f32),
        "x_sample": jax.random.normal(ks[1], (DEC_BATCH, DEC_SEQ, D_MODEL), f32),
        "norm_mix": 1.0 + nrm(ks[2], (DEPTH, D_MODEL), 0.05),
        "norm_mlp": 1.0 + nrm(ks[3], (DEPTH, D_MODEL), 0.05),
        "norm_final": 1.0 + nrm(ks[4], (D_MODEL,), 0.05),
        "a_w_qkv": nrm(ks[5], (N_A_LAYERS, D_MODEL, a_qkv_w), D_MODEL ** -0.5),
        "a_q_norm": 1.0 + nrm(ks[6], (N_A_LAYERS, HEAD_DIM), 0.05),
        "a_k_norm": 1.0 + nrm(ks[7], (N_A_LAYERS, HEAD_DIM), 0.05),
        "a_w_o": nrm(ks[8], (N_A_LAYERS, hd, D_MODEL), hd ** -0.5),
        "b_w_qkv": nrm(ks[9], (N_B_LAYERS, D_MODEL, 3 * hd), D_MODEL ** -0.5),
        "b_rel_bias": nrm(ks[10], (N_B_LAYERS, N_HEADS, NA_BIAS_H, NA_BIAS_W), 0.1),
        "b_w_o": nrm(ks[11], (N_B_LAYERS, hd, D_MODEL), hd ** -0.5),
        "mlp_w_in": nrm(ks[12], (DEPTH, D_MODEL, D_FF), D_MODEL ** -0.5),
        "mlp_w_out": nrm(ks[13], (DEPTH, D_FF, D_MODEL), D_FF ** -0.5),
    }


def reference(x_prompt, x_sample, norm_mix, norm_mlp, norm_final, a_w_qkv, a_q_norm, a_k_norm,
              a_w_o, b_w_qkv, b_rel_bias, b_w_o, mlp_w_in, mlp_w_out):
    y_prompt = trunk(x_prompt, norm_mix, norm_mlp, norm_final, a_w_qkv, a_q_norm, a_k_norm, a_w_o,
                     b_w_qkv, b_rel_bias, b_w_o, mlp_w_in, mlp_w_out)
    y_sample = trunk(x_sample, norm_mix, norm_mlp, norm_final, a_w_qkv, a_q_norm, a_k_norm, a_w_o,
                     b_w_qkv, b_rel_bias, b_w_o, mlp_w_in, mlp_w_out)
    return (y_prompt, y_sample)
```

```python
import functools

import jax
import jax.numpy as jnp
from jax import lax
from jax.experimental import pallas as pl
from jax.experimental.pallas import tpu as pltpu

D_MODEL = 1024
GRID_W = 64
N_HEADS = 16
HEAD_DIM = 64
A_KV_HEADS = 4
A_GROUP = N_HEADS // A_KV_HEADS
ROPE_THETA = 10000.0
AXIS_DIM = HEAD_DIM // 2
ROPE_HALF = AXIS_DIM // 2
NA_WIN_H = 8
NA_WIN_W = 16
NA_BIAS_H = 2 * NA_WIN_H - 1
NA_BIAS_W = 2 * NA_WIN_W - 1
D_FF = 4 * D_MODEL
NORM_EPS = 1e-6
HD = N_HEADS * HEAD_DIM
A_QKV = (N_HEADS + 2 * A_KV_HEADS) * HEAD_DIM

ROW_TILE = 512
FLASH_TQ = 256
FLASH_TK = 256
FLASH_HEADS = 2
NA_Q_ROWS = 4
NA_K_ROWS = 12
NA_NQ = NA_Q_ROWS * GRID_W
NA_NK = NA_K_ROWS * GRID_W
FF_CHUNK = 1024
VMEM_LIMIT_BYTES = 56 * 1024 * 1024

BF16 = jnp.bfloat16
F32 = jnp.float32


def _params(n_axes):
    return pltpu.CompilerParams(
        dimension_semantics=("arbitrary",) * n_axes, vmem_limit_bytes=VMEM_LIMIT_BYTES)


def _resident(shape):
    zeros = (0,) * len(shape)
    return pl.BlockSpec(shape, lambda *_: zeros, pipeline_mode=pl.Buffered(1))


def _rms(x, gain):
    ms = jnp.mean(x * x, axis=-1, keepdims=True)
    return x * lax.rsqrt(ms + NORM_EPS) * gain


def _qkv_a_kernel(x_ref, g_ref, wt_ref, qg_ref, kg_ref, cos_ref, sin_ref,
                  q_ref, k_ref, v_ref, s_ref, *, tm):
    h = _rms(x_ref[0], g_ref[...]).astype(BF16)
    qkv_t = lax.dot_general(wt_ref[...], h, (((1,), (1,)), ((), ())),
                            preferred_element_type=F32)
    n_slots = N_HEADS + 2 * A_KV_HEADS
    s_ref[...] = qkv_t.reshape(n_slots, HEAD_DIM, tm)

    def norm_rope(lo, hi, gain_ref, store):
        t = s_ref[lo:hi]
        r = lax.rsqrt(jnp.sum(t * t, axis=1, keepdims=True) * (1.0 / HEAD_DIM) + NORM_EPS)
        for a in range(2):
            d1 = a * AXIS_DIM
            d2 = d1 + ROPE_HALF
            x1 = s_ref[lo:hi, d1:d1 + ROPE_HALF, :] * r * gain_ref[d1:d1 + ROPE_HALF, :]
            x2 = s_ref[lo:hi, d2:d2 + ROPE_HALF, :] * r * gain_ref[d2:d2 + ROPE_HALF, :]
            c = cos_ref[a * ROPE_HALF:(a + 1) * ROPE_HALF, :]
            s = sin_ref[a * ROPE_HALF:(a + 1) * ROPE_HALF, :]
            store(d1, x1 * c - x2 * s)
            store(d2, x2 * c + x1 * s)

    def store_q(d, val):
        q_ref[0, :, d:d + ROPE_HALF, :] = val.astype(BF16)

    def store_k(d, val):
        s_ref[N_HEADS:N_HEADS + A_KV_HEADS, d:d + ROPE_HALF, :] = val

    norm_rope(0, N_HEADS, qg_ref, store_q)
    norm_rope(N_HEADS, N_HEADS + A_KV_HEADS, kg_ref, store_k)
    k_nat = s_ref[N_HEADS:N_HEADS + A_KV_HEADS].reshape(A_KV_HEADS * HEAD_DIM, tm).T
    for g in range(A_KV_HEADS):
        k_ref[0, g] = k_nat[:, g * HEAD_DIM:(g + 1) * HEAD_DIM].astype(BF16)
        for c in range(tm // FLASH_TK):
            v_ref[0, g, c] = s_ref[N_HEADS + A_KV_HEADS + g, :,
                                   c * FLASH_TK:(c + 1) * FLASH_TK].astype(BF16)


def _qkv_a(x, gain, w_t, q_gain, k_gain, cos_t, sin_t):
    b, s, _ = x.shape
    tm = ROW_TILE
    n_slots = N_HEADS + 2 * A_KV_HEADS
    return pl.pallas_call(
        functools.partial(_qkv_a_kernel, tm=tm),
        grid=(b, s // tm),
        in_specs=[
            pl.BlockSpec((1, tm, D_MODEL), lambda bi, i: (bi, i, 0)),
            _resident((1, D_MODEL)),
            _resident((A_QKV, D_MODEL)),
            _resident((HEAD_DIM, tm)),
            _resident((HEAD_DIM, tm)),
            pl.BlockSpec((AXIS_DIM, tm), lambda bi, i: (0, i)),
            pl.BlockSpec((AXIS_DIM, tm), lambda bi, i: (0, i)),
        ],
        out_specs=[
            pl.BlockSpec((1, N_HEADS, HEAD_DIM, tm), lambda bi, i: (bi, 0, 0, i)),
            pl.BlockSpec((1, A_KV_HEADS, tm, HEAD_DIM), lambda bi, i: (bi, 0, i, 0)),
            pl.BlockSpec((1, A_KV_HEADS, tm // FLASH_TK, HEAD_DIM, FLASH_TK),
                         lambda bi, i: (bi, 0, i, 0, 0)),
        ],
        out_shape=[
            jax.ShapeDtypeStruct((b, N_HEADS, HEAD_DIM, s), BF16),
            jax.ShapeDtypeStruct((b, A_KV_HEADS, s, HEAD_DIM), BF16),
            jax.ShapeDtypeStruct((b, A_KV_HEADS, s // FLASH_TK, HEAD_DIM, FLASH_TK), BF16),
        ],
        scratch_shapes=[pltpu.VMEM((n_slots, HEAD_DIM, tm), F32)],
        compiler_params=_params(2),
        name="qkv_a",
    )(x, gain, w_t, q_gain, k_gain, cos_t, sin_t)


def _flash_kernel(q_ref, k_ref, v_ref, o_ref, *, n_chunks, tq):
    q_t = [q_ref[0, e] for e in range(FLASH_HEADS)]

    def body(c, carry):
        start = pl.multiple_of(c * FLASH_TK, FLASH_TK)
        k_c = k_ref[0, 0, pl.ds(start, FLASH_TK), :]
        v_c = v_ref[0, 0, c]
        new = []
        for e in range(FLASH_HEADS):
            m, l, acc = carry[e]
            s_t = jnp.dot(k_c, q_t[e], preferred_element_type=F32)
            m_new = jnp.maximum(m, jnp.max(s_t, axis=0, keepdims=True))
            alpha = jnp.exp(m - m_new)
            p = jnp.exp(s_t - m_new)
            l = alpha * l + jnp.sum(p, axis=0, keepdims=True)
            acc = alpha * acc + jnp.dot(v_c, p.astype(BF16), preferred_element_type=F32)
            new.append((m_new, l, acc))
        return tuple(new)

    init = tuple((jnp.full((1, tq), -jnp.inf, F32), jnp.zeros((1, tq), F32),
                  jnp.zeros((HEAD_DIM, tq), F32)) for _ in range(FLASH_HEADS))
    final = lax.fori_loop(0, n_chunks, body, init)
    o_t = jnp.concatenate([acc / l for (_, l, acc) in final], axis=0)
    o_ref[0] = o_t.T.astype(BF16)


def _flash(q_t, k, v_t):
    b, _, _, s = q_t.shape
    tq = FLASH_TQ
    n_chunks = s // FLASH_TK
    pairs = N_HEADS // FLASH_HEADS
    per_kv = A_GROUP // FLASH_HEADS
    return pl.pallas_call(
        functools.partial(_flash_kernel, n_chunks=n_chunks, tq=tq),
        grid=(b, pairs, s // tq),
        in_specs=[
            pl.BlockSpec((1, FLASH_HEADS, HEAD_DIM, tq), lambda bi, hp, i: (bi, hp, 0, i)),
            pl.BlockSpec((1, 1, s, HEAD_DIM), lambda bi, hp, i: (bi, hp // per_kv, 0, 0)),
            pl.BlockSpec((1, 1, n_chunks, HEAD_DIM, FLASH_TK),
                         lambda bi, hp, i: (bi, hp // per_kv, 0, 0, 0)),
        ],
        out_specs=pl.BlockSpec((1, tq, FLASH_HEADS * HEAD_DIM), lambda bi, hp, i: (bi, i, hp)),
        out_shape=jax.ShapeDtypeStruct((b, s, HD), BF16),
        compiler_params=_params(3),
        name="flash_a",
    )(q_t, k, v_t)


def _proj_res_kernel(a_ref, w_ref, r_ref, o_ref):
    o_ref[0] = r_ref[0] + jnp.dot(a_ref[0], w_ref[...], preferred_element_type=F32)


def _proj_res(a, w, res):
    b, s, _ = res.shape
    tm = ROW_TILE
    return pl.pallas_call(
        _proj_res_kernel,
        grid=(b, s // tm),
        in_specs=[
            pl.BlockSpec((1, tm, HD), lambda bi, i: (bi, i, 0)),
            _resident((HD, D_MODEL)),
            pl.BlockSpec((1, tm, D_MODEL), lambda bi, i: (bi, i, 0)),
        ],
        out_specs=pl.BlockSpec((1, tm, D_MODEL), lambda bi, i: (bi, i, 0)),
        out_shape=jax.ShapeDtypeStruct((b, s, D_MODEL), F32),
        compiler_params=_params(2),
        name="proj_res",
    )(a, w, res)


def _mlp_kernel(x_ref, g_ref, win_ref, wout_ref, gf_ref, o_ref, *, final):
    x = x_ref[0]
    h = _rms(x, g_ref[...]).astype(BF16)
    acc = x
    for f in range(D_FF // FF_CHUNK):
        u = jnp.dot(h, win_ref[:, f * FF_CHUNK:(f + 1) * FF_CHUNK], preferred_element_type=F32)
        u = jnp.maximum(u, 0.0)
        acc = acc + jnp.dot((u * u).astype(BF16), wout_ref[f * FF_CHUNK:(f + 1) * FF_CHUNK, :],
                            preferred_element_type=F32)
    if final:
        acc = _rms(acc, gf_ref[...])
    o_ref[0] = acc


def _mlp(x, gain, w_in, w_out, gain_final, final):
    b, s, _ = x.shape
    tm = ROW_TILE
    return pl.pallas_call(
        functools.partial(_mlp_kernel, final=final),
        grid=(b, s // tm),
        in_specs=[
            pl.BlockSpec((1, tm, D_MODEL), lambda bi, i: (bi, i, 0)),
            _resident((1, D_MODEL)),
            _resident((D_MODEL, D_FF)),
            _resident((D_FF, D_MODEL)),
            _resident((1, D_MODEL)),
        ],
        out_specs=pl.BlockSpec((1, tm, D_MODEL), lambda bi, i: (bi, i, 0)),
        out_shape=jax.ShapeDtypeStruct((b, s, D_MODEL), F32),
        compiler_params=_params(2),
        name="mlp_final" if final else "mlp",
    )(x, gain, w_in, w_out, gain_final)


def _qkv_b_kernel(x_ref, g_ref, w_ref, o_ref):
    h = _rms(x_ref[0], g_ref[...]).astype(BF16)
    for j in range(3):
        y = jnp.dot(h, w_ref[:, j * HD:(j + 1) * HD], preferred_element_type=F32)
        if j == 0:
            y = y * (HEAD_DIM ** -0.5)
        o_ref[0, :, j * HD:(j + 1) * HD] = y.astype(BF16)


def _qkv_b(x, gain, w):
    b, s, _ = x.shape
    tm = ROW_TILE
    return pl.pallas_call(
        _qkv_b_kernel,
        grid=(b, s // tm),
        in_specs=[
            pl.BlockSpec((1, tm, D_MODEL), lambda bi, i: (bi, i, 0)),
            _resident((1, D_MODEL)),
            _resident((D_MODEL, 3 * HD)),
        ],
        out_specs=pl.BlockSpec((1, tm, 3 * HD), lambda bi, i: (bi, i, 0)),
        out_shape=jax.ShapeDtypeStruct((b, s, 3 * HD), BF16),
        compiler_params=_params(2),
        name="qkv_b",
    )(x, gain, w)


def _natten_kernel(q_ref, k_ref, v_ref, b_ref, o_ref, *, rows):
    t = pl.program_id(2)
    n_blocks = rows // NA_Q_ROWS
    k_base = jnp.clip(NA_Q_ROWS * t - NA_WIN_H // 2, 0, rows - NA_K_ROWS)
    variant = jnp.where(t == 0, 0, jnp.where(t == n_blocks - 1, 2, 1))
    start = pl.multiple_of(k_base * GRID_W, GRID_W)
    k_w = k_ref[0, pl.ds(start, NA_NK), :]
    v_w = v_ref[0, pl.ds(start, NA_NK), :]
    q = q_ref[0]
    lane = lax.broadcasted_iota(jnp.int32, (NA_NQ, 2 * HEAD_DIM), 1)
    out = None
    for e in range(2):
        mine = (lane >= e * HEAD_DIM) & (lane < (e + 1) * HEAD_DIM)
        q_e = jnp.where(mine, q, jnp.zeros_like(q))
        s = lax.dot_general(q_e, k_w, (((1,), (1,)), ((), ())), preferred_element_type=F32)
        s = s + b_ref[variant, e]
        m = jnp.max(s, axis=-1, keepdims=True)
        p = jnp.exp(s - m)
        l = jnp.sum(p, axis=-1, keepdims=True)
        o = jnp.dot(p.astype(BF16), v_w, preferred_element_type=F32) / l
        out = o if e == 0 else jnp.where(mine, o, out)
    o_ref[0] = out.astype(BF16)


def _natten(qkv, bias):
    b, s, _ = qkv.shape
    rows = s // GRID_W
    pairs = N_HEADS // 2
    lanes = 2 * HEAD_DIM
    return pl.pallas_call(
        functools.partial(_natten_kernel, rows=rows),
        grid=(b, pairs, rows // NA_Q_ROWS),
        in_specs=[
            pl.BlockSpec((1, NA_NQ, lanes), lambda bi, hp, t: (bi, t, hp)),
            pl.BlockSpec((1, s, lanes), lambda bi, hp, t: (bi, 0, pairs + hp)),
            pl.BlockSpec((1, s, lanes), lambda bi, hp, t: (bi, 0, 2 * pairs + hp)),
            pl.BlockSpec((3, 2, NA_NQ, NA_NK), lambda bi, hp, t: (0, hp, 0, 0)),
        ],
        out_specs=pl.BlockSpec((1, NA_NQ, lanes), lambda bi, hp, t: (bi, t, hp)),
        out_shape=jax.ShapeDtypeStruct((b, s, HD), BF16),
        compiler_params=_params(3),
        name="natten_b",
    )(qkv, qkv, qkv, bias)


def _rope_tables_t(seq):
    t = jnp.arange(seq)
    row = (t // GRID_W).astype(F32)
    col = (t % GRID_W).astype(F32)
    inv = ROPE_THETA ** (-jnp.arange(0, AXIS_DIM, 2, dtype=F32) / AXIS_DIM)
    ang = jnp.concatenate([inv[:, None] * row[None, :], inv[:, None] * col[None, :]], axis=0)
    return jnp.cos(ang), jnp.sin(ang)


def _natten_bias(rel_bias, rows):
    n_blocks = rows // NA_Q_ROWS
    table = rel_bias.reshape(N_HEADS, NA_BIAS_H * NA_BIAS_W)
    t = jnp.array([0, 1, n_blocks - 1])[:, None, None, None, None]
    a = jnp.arange(NA_Q_ROWS)[None, :, None, None, None]
    c = jnp.arange(GRID_W)[None, None, :, None, None]
    i = jnp.arange(NA_K_ROWS)[None, None, None, :, None]
    kc = jnp.arange(GRID_W)[None, None, None, None, :]
    r = NA_Q_ROWS * t + a
    r0 = jnp.clip(r - NA_WIN_H // 2, 0, rows - NA_WIN_H)
    kr = jnp.clip(NA_Q_ROWS * t - NA_WIN_H // 2, 0, rows - NA_K_ROWS) + i
    c0 = jnp.clip(c - NA_WIN_W // 2, 0, GRID_W - NA_WIN_W)
    valid = (kr >= r0) & (kr < r0 + NA_WIN_H) & (kc >= c0) & (kc < c0 + NA_WIN_W)
    idx = (kr - r + NA_WIN_H - 1) * NA_BIAS_W + (kc - c + NA_WIN_W - 1)
    idx = jnp.where(valid, idx, 0).reshape(3, NA_NQ, NA_NK)
    valid = valid.reshape(3, NA_NQ, NA_NK)
    vals = jnp.take(table, idx, axis=1)
    vals = jnp.where(valid[None], vals, -jnp.inf)
    return vals.transpose(1, 0, 2, 3).astype(F32)


def _trunk(x, p):
    _, s, _ = x.shape
    depth = p["norm_mix"].shape[0]
    for i in range(depth):
        j = i // 2
        g_mix = p["norm_mix"][i][None, :]
        if i % 2 == 0:
            q_t, k, v_t = _qkv_a(x, g_mix, p["a_w_qkv_t"][j], p["a_q_gain"][j], p["a_k_gain"][j],
                                 p["cos_t"], p["sin_t"])
            o = _flash(q_t, k, v_t)
            x = _proj_res(o, p["a_w_o"][j], x)
        else:
            qkv = _qkv_b(x, g_mix, p["b_w_qkv"][j])
            o = _natten(qkv, p["b_bias"][j])
            x = _proj_res(o, p["b_w_o"][j], x)
        x = _mlp(x, p["norm_mlp"][i][None, :], p["mlp_w_in"][i], p["mlp_w_out"][i],
                 p["norm_final"][None, :], final=(i == depth - 1))
    return x


def kernel(x_prompt, x_sample, norm_mix, norm_mlp, norm_final, a_w_qkv, a_q_norm, a_k_norm, a_w_o,
           b_w_qkv, b_rel_bias, b_w_o, mlp_w_in, mlp_w_out):
    scale = HEAD_DIM ** -0.5
    shared = {
        "norm_mix": norm_mix, "norm_mlp": norm_mlp, "norm_final": norm_final,
        "a_w_qkv_t": jnp.swapaxes(a_w_qkv, 1, 2).astype(BF16),
        "a_q_gain": jnp.broadcast_to((a_q_norm * scale)[:, :, None],
                                     a_q_norm.shape + (ROW_TILE,)),
        "a_k_gain": jnp.broadcast_to(a_k_norm[:, :, None], a_k_norm.shape + (ROW_TILE,)),
        "a_w_o": a_w_o.astype(BF16),
        "b_w_qkv": b_w_qkv.astype(BF16),
        "b_w_o": b_w_o.astype(BF16),
        "mlp_w_in": mlp_w_in.astype(BF16),
        "mlp_w_out": mlp_w_out.astype(BF16),
    }
    outs = []
    for x in (x_prompt, x_sample):
        s = x.shape[1]
        cos_t, sin_t = _rope_tables_t(s)
        p = dict(shared, cos_t=cos_t, sin_t=sin_t,
                 b_bias=jnp.stack([_natten_bias(rb, s // GRID_W) for rb in b_rel_bias]))
        outs.append(_trunk(x, p))
    return tuple(outs)
```

```python
import functools

import jax
import jax.numpy as jnp
from jax import lax
from jax.experimental import pallas as pl
from jax.experimental.pallas import tpu as pltpu

D_MODEL = 1024
GRID_W = 64
N_HEADS = 16
HEAD_DIM = 64
A_KV_HEADS = 4
A_GROUP = N_HEADS // A_KV_HEADS
ROPE_THETA = 10000.0
AXIS_DIM = HEAD_DIM // 2
ROPE_HALF = AXIS_DIM // 2
NA_WIN_H = 8
NA_WIN_W = 16
NA_BIAS_H = 2 * NA_WIN_H - 1
NA_BIAS_W = 2 * NA_WIN_W - 1
D_FF = 4 * D_MODEL
NORM_EPS = 1e-6
LOG2_E = 1.4426950408889634
HD =N_HEADS * HEAD_DIM
A_QKV = (N_HEADS + 2 * A_KV_HEADS) * HEAD_DIM

ROW_TILE = 512
FLASH_TQ = 256
FLASH_TK = 512
FLASH_HEADS = 4
FLASH_L_ROWS = 16
NA_Q_ROWS = 4
NA_K_ROWS = 12
NA_NQ = NA_Q_ROWS * GRID_W
NA_NK = NA_K_ROWS * GRID_W
FF_CHUNK = 1024
VMEM_LIMIT_BYTES = 56 * 1024 * 1024

BF16 = jnp.bfloat16
F32 = jnp.float32


def _params(n_axes):
    return pltpu.CompilerParams(
        dimension_semantics=("arbitrary",) * n_axes, vmem_limit_bytes=VMEM_LIMIT_BYTES)


def _resident(shape):
    zeros = (0,) * len(shape)
    return pl.BlockSpec(shape, lambda *_: zeros, pipeline_mode=pl.Buffered(1))


def _rms(x, gain):
    ms = jnp.mean(x * x, axis=-1, keepdims=True)
    return x * lax.rsqrt(ms + NORM_EPS) * gain


def _qkv_a_kernel(x_ref, g_ref, wt_ref, qg_ref, kg_ref, cos_ref, sin_ref,
                  q_ref, k_ref, v_ref, s_ref, *, tm):
    h = _rms(x_ref[0], g_ref[...]).astype(BF16)
    qkv_t = lax.dot_general(wt_ref[...], h, (((1,), (1,)), ((), ())),
                            preferred_element_type=F32)
    n_slots = N_HEADS + 2 * A_KV_HEADS
    s_ref[...] = qkv_t.reshape(n_slots, HEAD_DIM, tm)

    def norm_rope(lo, hi, gain_ref, store):
        t = s_ref[lo:hi]
        r = lax.rsqrt(jnp.sum(t * t, axis=1, keepdims=True) * (1.0 / HEAD_DIM) + NORM_EPS)
        for a in range(2):
            d1 = a * AXIS_DIM
            d2 = d1 + ROPE_HALF
            x1 = s_ref[lo:hi, d1:d1 + ROPE_HALF, :] * r * gain_ref[d1:d1 + ROPE_HALF, :]
            x2 = s_ref[lo:hi, d2:d2 + ROPE_HALF, :] * r * gain_ref[d2:d2 + ROPE_HALF, :]
            c = cos_ref[a * ROPE_HALF:(a + 1) * ROPE_HALF, :]
            s = sin_ref[a * ROPE_HALF:(a + 1) * ROPE_HALF, :]
            store(d1, x1 * c - x2 * s)
            store(d2, x2 * c + x1 * s)

    def store_q(d, val):
        q_ref[0, :, d:d + ROPE_HALF, :] = val.astype(BF16)

    def store_k(d, val):
        s_ref[N_HEADS:N_HEADS + A_KV_HEADS, d:d + ROPE_HALF, :] = val

    norm_rope(0, N_HEADS, qg_ref, store_q)
    norm_rope(N_HEADS, N_HEADS + A_KV_HEADS, kg_ref, store_k)
    k_nat = s_ref[N_HEADS:N_HEADS + A_KV_HEADS].reshape(A_KV_HEADS * HEAD_DIM, tm).T
    for g in range(A_KV_HEADS):
        k_ref[0, g] = k_nat[:, g * HEAD_DIM:(g + 1) * HEAD_DIM].astype(BF16)
        for c in range(tm // FLASH_TK):
            v_ref[0, g, c] = s_ref[N_HEADS + A_KV_HEADS + g, :,
                                   c * FLASH_TK:(c + 1) * FLASH_TK].astype(BF16)


def _qkv_a(x, gain, w_t, q_gain, k_gain, cos_t, sin_t):
    b, s, _ = x.shape
    tm = ROW_TILE
    n_slots = N_HEADS + 2 * A_KV_HEADS
    return pl.pallas_call(
        functools.partial(_qkv_a_kernel, tm=tm),
        grid=(b, s // tm),
        in_specs=[
            pl.BlockSpec((1, tm, D_MODEL), lambda bi, i: (bi, i, 0)),
            _resident((1, D_MODEL)),
            _resident((A_QKV, D_MODEL)),
            _resident((HEAD_DIM, tm)),
            _resident((HEAD_DIM, tm)),
            pl.BlockSpec((AXIS_DIM, tm), lambda bi, i: (0, i)),
            pl.BlockSpec((AXIS_DIM, tm), lambda bi, i: (0, i)),
        ],
        out_specs=[
            pl.BlockSpec((1, N_HEADS, HEAD_DIM, tm), lambda bi, i: (bi, 0, 0, i)),
            pl.BlockSpec((1, A_KV_HEADS, tm, HEAD_DIM), lambda bi, i: (bi, 0, i, 0)),
            pl.BlockSpec((1, A_KV_HEADS, tm // FLASH_TK, HEAD_DIM, FLASH_TK),
                         lambda bi, i: (bi, 0, i, 0, 0)),
        ],
        out_shape=[
            jax.ShapeDtypeStruct((b, N_HEADS, HEAD_DIM, s), BF16),
            jax.ShapeDtypeStruct((b, A_KV_HEADS, s, HEAD_DIM), BF16),
            jax.ShapeDtypeStruct((b, A_KV_HEADS, s // FLASH_TK, HEAD_DIM, FLASH_TK), BF16),
        ],
        scratch_shapes=[pltpu.VMEM((n_slots, HEAD_DIM, tm), F32)],
        compiler_params=_params(2),
        name="qkv_a",
    )(x, gain, w_t, q_gain, k_gain, cos_t, sin_t)


def _flash_kernel(q_ref, k_ref, v_ref, o_ref, s_ref, *, n_chunks, tq):
    def scores(c, slot, e):
        start = pl.multiple_of(c * FLASH_TK, FLASH_TK)
        k_c = k_ref[0, 0, pl.ds(start, FLASH_TK), :]
        s_ref[slot, e] = jnp.dot(k_c, q_ref[0, e], preferred_element_type=F32)

    ones = jnp.ones((FLASH_L_ROWS, FLASH_TK), BF16)

    def update(c, slot, e, stat):
        m, acc = stat
        s_t = s_ref[slot, e]
        m_new = jnp.maximum(m, jnp.max(s_t, axis=0, keepdims=True))
        alpha = jnp.exp2(m - m_new)
        p = jnp.exp2(s_t - m_new).astype(BF16)
        v_ext = jnp.concatenate([v_ref[0, 0, c], ones], axis=0)
        acc = alpha * acc + jnp.dot(v_ext, p, preferred_element_type=F32)
        return m_new, acc

    def step(c, slot, stats, issue_next):
        new = []
        for e in range(FLASH_HEADS):
            if issue_next:
                scores(c + 1, 1 - slot, e)
            new.append(update(c, slot, e, stats[e]))
        return tuple(new)

    def body(j, stats):
        stats = step(2 * j, 0, stats, True)
        return step(2 * j + 1, 1, stats, True)

    for e in range(FLASH_HEADS):
        scores(0, 0, e)
    stats = tuple((jnp.full((1, tq), -jnp.inf, F32),
                   jnp.zeros((HEAD_DIM + FLASH_L_ROWS, tq), F32)) for _ in range(FLASH_HEADS))
    stats = lax.fori_loop(0, n_chunks // 2 - 1, body, stats)
    stats = step(n_chunks - 2, 0, stats, True)
    stats = step(n_chunks - 1, 1, stats, False)
    o_t = jnp.concatenate([acc[:HEAD_DIM] / acc[HEAD_DIM:HEAD_DIM + 1] for (_, acc) in stats],
                          axis=0)
    o_ref[0] = o_t.T.astype(BF16)


def _flash(q_t, k, v_t):
    b, _, _, s = q_t.shape
    tq = FLASH_TQ
    n_chunks = s // FLASH_TK
    pairs = N_HEADS // FLASH_HEADS
    per_kv = A_GROUP // FLASH_HEADS
    return pl.pallas_call(
        functools.partial(_flash_kernel, n_chunks=n_chunks, tq=tq),
        grid=(b, pairs, s // tq),
        in_specs=[
            pl.BlockSpec((1, FLASH_HEADS, HEAD_DIM, tq), lambda bi, hp, i: (bi, hp, 0, i)),
            pl.BlockSpec((1, 1, s, HEAD_DIM), lambda bi, hp, i: (bi, hp // per_kv, 0, 0)),
            pl.BlockSpec((1, 1, n_chunks, HEAD_DIM, FLASH_TK),
                         lambda bi, hp, i: (bi, hp // per_kv, 0, 0, 0)),
        ],
        out_specs=pl.BlockSpec((1, tq, FLASH_HEADS * HEAD_DIM), lambda bi, hp, i: (bi, i, hp)),
        out_shape=jax.ShapeDtypeStruct((b, s, HD), BF16),
        scratch_shapes=[pltpu.VMEM((2, FLASH_HEADS, FLASH_TK, tq), F32)],
        compiler_params=_params(3),
        name="flash_a",
    )(q_t, k, v_t)


def _proj_res_kernel(a_ref, w_ref, r_ref, o_ref):
    o_ref[0] = r_ref[0] + jnp.dot(a_ref[0], w_ref[...], preferred_element_type=F32)


def _proj_res(a, w, res):
    b, s, _ = res.shape
    tm = ROW_TILE
    return pl.pallas_call(
        _proj_res_kernel,
        grid=(b, s // tm),
        in_specs=[
            pl.BlockSpec((1, tm, HD), lambda bi, i: (bi, i, 0)),
            _resident((HD, D_MODEL)),
            pl.BlockSpec((1, tm, D_MODEL), lambda bi, i: (bi, i, 0)),
        ],
        out_specs=pl.BlockSpec((1, tm, D_MODEL), lambda bi, i: (bi, i, 0)),
        out_shape=jax.ShapeDtypeStruct((b, s, D_MODEL), F32),
        compiler_params=_params(2),
        name="proj_res",
    )(a, w, res)


def _mlp_kernel(x_ref, g_ref, win_ref, wout_ref, gf_ref, o_ref, *, final):
    x = x_ref[0]
    h = _rms(x, g_ref[...]).astype(BF16)
    acc = x
    for f in range(D_FF // FF_CHUNK):
        u = jnp.dot(h, win_ref[:, f * FF_CHUNK:(f + 1) * FF_CHUNK], preferred_element_type=F32)
        u = jnp.maximum(u, 0.0)
        acc = acc + jnp.dot((u * u).astype(BF16), wout_ref[f * FF_CHUNK:(f + 1) * FF_CHUNK, :],
                            preferred_element_type=F32)
    if final:
        acc = _rms(acc, gf_ref[...])
    o_ref[0] = acc


def _mlp(x, gain, w_in, w_out, gain_final, final):
    b, s, _ = x.shape
    tm = ROW_TILE
    return pl.pallas_call(
        functools.partial(_mlp_kernel, final=final),
        grid=(b, s // tm),
        in_specs=[
            pl.BlockSpec((1, tm, D_MODEL), lambda bi, i: (bi, i, 0)),
            _resident((1, D_MODEL)),
            _resident((D_MODEL, D_FF)),
            _resident((D_FF, D_MODEL)),
            _resident((1, D_MODEL)),
        ],
        out_specs=pl.BlockSpec((1, tm, D_MODEL), lambda bi, i: (bi, i, 0)),
        out_shape=jax.ShapeDtypeStruct((b, s, D_MODEL), F32),
        compiler_params=_params(2),
        name="mlp_final" if final else "mlp",
    )(x, gain, w_in, w_out, gain_final)


def _qkv_b_kernel(x_ref, g_ref, w_ref, o_ref):
    h = _rms(x_ref[0], g_ref[...]).astype(BF16)
    for j in range(3):
        y = jnp.dot(h, w_ref[:, j * HD:(j + 1) * HD], preferred_element_type=F32)
        if j == 0:
            y = y * (HEAD_DIM ** -0.5)
        o_ref[0, :, j * HD:(j + 1) * HD] = y.astype(BF16)


def _qkv_b(x, gain, w):
    b, s, _ = x.shape
    tm = ROW_TILE
    return pl.pallas_call(
        _qkv_b_kernel,
        grid=(b, s // tm),
        in_specs=[
            pl.BlockSpec((1, tm, D_MODEL), lambda bi, i: (bi, i, 0)),
            _resident((1, D_MODEL)),
            _resident((D_MODEL, 3 * HD)),
        ],
        out_specs=pl.BlockSpec((1, tm, 3 * HD), lambda bi, i: (bi, i, 0)),
        out_shape=jax.ShapeDtypeStruct((b, s, 3 * HD), BF16),
        compiler_params=_params(2),
        name="qkv_b",
    )(x, gain, w)


def _natten_kernel(q_ref, k_ref, v_ref, b_ref, o_ref, *, rows):
    t = pl.program_id(2)
    n_blocks = rows // NA_Q_ROWS
    k_base = jnp.clip(NA_Q_ROWS * t - NA_WIN_H // 2, 0, rows - NA_K_ROWS)
    variant = jnp.where(t == 0, 0, jnp.where(t == n_blocks - 1, 2, 1))
    start = pl.multiple_of(k_base * GRID_W, GRID_W)
    k_w = k_ref[0, pl.ds(start, NA_NK), :]
    v_w = v_ref[0, pl.ds(start, NA_NK), :]
    q = q_ref[0]
    lane = lax.broadcasted_iota(jnp.int32, (NA_NQ, 2 * HEAD_DIM), 1)
    out = None
    for e in range(2):
        mine = (lane >= e * HEAD_DIM) & (lane < (e + 1) * HEAD_DIM)
        q_e = jnp.where(mine, q, jnp.zeros_like(q))
        s = lax.dot_general(q_e, k_w, (((1,), (1,)), ((), ())), preferred_element_type=F32)
        s = s + b_ref[variant, e]
        m = jnp.max(s, axis=-1, keepdims=True)
        p = jnp.exp(s - m)
        l = jnp.sum(p, axis=-1, keepdims=True)
        o = jnp.dot(p.astype(BF16), v_w, preferred_element_type=F32) / l
        out = o if e == 0 else jnp.where(mine, o, out)
    o_ref[0] = out.astype(BF16)


def _natten(qkv, bias):
    b, s, _ = qkv.shape
    rows = s // GRID_W
    pairs = N_HEADS // 2
    lanes = 2 * HEAD_DIM
    return pl.pallas_call(
        functools.partial(_natten_kernel, rows=rows),
        grid=(b, pairs, rows // NA_Q_ROWS),
        in_specs=[
            pl.BlockSpec((1, NA_NQ, lanes), lambda bi, hp, t: (bi, t, hp)),
            pl.BlockSpec((1, s, lanes), lambda bi, hp, t: (bi, 0, pairs + hp)),
            pl.BlockSpec((1, s, lanes), lambda bi, hp, t: (bi, 0, 2 * pairs + hp)),
            pl.BlockSpec((3, 2, NA_NQ, NA_NK), lambda bi, hp, t: (0, hp, 0, 0)),
        ],
        out_specs=pl.BlockSpec((1, NA_NQ, lanes), lambda bi, hp, t: (bi, t, hp)),
        out_shape=jax.ShapeDtypeStruct((b, s, HD), BF16),
        compiler_params=_params(3),
        name="natten_b",
    )(qkv, qkv, qkv, bias)


def _rope_tables_t(seq):
    t = jnp.arange(seq)
    row = (t // GRID_W).astype(F32)
    col = (t % GRID_W).astype(F32)
    inv = ROPE_THETA ** (-jnp.arange(0, AXIS_DIM, 2, dtype=F32) / AXIS_DIM)
    ang = jnp.concatenate([inv[:, None] * row[None, :], inv[:, None] * col[None, :]], axis=0)
    return jnp.cos(ang), jnp.sin(ang)


def _natten_bias(rel_bias, rows):
    n_blocks = rows // NA_Q_ROWS
    t = jnp.array([0, 1, n_blocks - 1])[:, None, None]
    a = jnp.arange(NA_Q_ROWS)[None, :, None]
    i = jnp.arange(NA_K_ROWS)[None, None, :]
    r = NA_Q_ROWS * t + a
    r0 = jnp.clip(r - NA_WIN_H // 2, 0, rows - NA_WIN_H)
    kr = jnp.clip(NA_Q_ROWS * t - NA_WIN_H // 2, 0, rows - NA_K_ROWS) + i
    valid_r = (kr >= r0) & (kr < r0 + NA_WIN_H)
    c = jnp.arange(GRID_W)[:, None]
    kc = jnp.arange(GRID_W)[None, :]
    c0 = jnp.clip(c - NA_WIN_W // 2, 0, GRID_W - NA_WIN_W)
    valid_c = (kc >= c0) & (kc < c0 + NA_WIN_W)
    pick_r = ((kr - r + NA_WIN_H - 1)[..., None] == jnp.arange(NA_BIAS_H)).astype(F32)
    pick_c = ((kc - c + NA_WIN_W - 1)[..., None] == jnp.arange(NA_BIAS_W)).astype(F32)
    vals = jnp.einsum("vaid,hde,cke->vhacik", pick_r, rel_bias.astype(F32), pick_c,
                      precision=lax.Precision.HIGHEST)
    valid = valid_r[:, None, :, None, :, None] & valid_c[None, None, None, :, None, :]
    vals = jnp.where(valid, vals, -jnp.inf)
    return vals.reshape(3, N_HEADS, NA_NQ, NA_NK)


def _trunk(x, p):
    _, s, _ = x.shape
    depth = p["norm_mix"].shape[0]
    for i in range(depth):
        j = i // 2
        g_mix = p["norm_mix"][i][None, :]
        if i % 2 == 0:
            q_t, k, v_t = _qkv_a(x, g_mix, p["a_w_qkv_t"][j], p["a_q_gain"][j], p["a_k_gain"][j],
                                 p["cos_t"], p["sin_t"])
            o = _flash(q_t, k, v_t)
            x = _proj_res(o, p["a_w_o"][j], x)
        else:
            qkv = _qkv_b(x, g_mix, p["b_w_qkv"][j])
            o = _natten(qkv, p["b_bias"][j])
            x = _proj_res(o, p["b_w_o"][j], x)
        x = _mlp(x, p["norm_mlp"][i][None, :], p["mlp_w_in"][i], p["mlp_w_out"][i],
                 p["norm_final"][None, :], final=(i == depth - 1))
    return x


def kernel(x_prompt, x_sample, norm_mix, norm_mlp, norm_final, a_w_qkv, a_q_norm, a_k_norm, a_w_o,
           b_w_qkv, b_rel_bias, b_w_o, mlp_w_in, mlp_w_out):
    scale = HEAD_DIM ** -0.5 * LOG2_E
    shared = {
        "norm_mix": norm_mix, "norm_mlp": norm_mlp, "norm_final": norm_final,
        "a_w_qkv_t": jnp.swapaxes(a_w_qkv, 1, 2).astype(BF16),
        "a_q_gain": jnp.broadcast_to((a_q_norm * scale)[:, :, None],
                                     a_q_norm.shape + (ROW_TILE,)),
        "a_k_gain": jnp.broadcast_to(a_k_norm[:, :, None], a_k_norm.shape + (ROW_TILE,)),
        "a_w_o": a_w_o.astype(BF16),
        "b_w_qkv": b_w_qkv.astype(BF16),
        "b_w_o": b_w_o.astype(BF16),
        "mlp_w_in": mlp_w_in.astype(BF16),
        "mlp_w_out": mlp_w_out.astype(BF16),
    }
    outs = []
    for x in (x_prompt, x_sample):
        s = x.shape[1]
        cos_t, sin_t = _rope_tables_t(s)
        p = dict(shared, cos_t=cos_t, sin_t=sin_t,
                 b_bias=jnp.stack([_natten_bias(rb, s // GRID_W) for rb in b_rel_bias]))
        outs.append(_trunk(x, p))
    return tuple(outs)
```

```python
import functools

import jax
import jax.numpy as jnp
from jax import lax
from jax.experimental import pallas as pl
from jax.experimental.pallas import tpu as pltpu

D_MODEL = 1024
GRID_W = 64
N_HEADS = 16
HEAD_DIM = 64
A_KV_HEADS = 4
A_GROUP = N_HEADS // A_KV_HEADS
ROPE_THETA = 10000.0
AXIS_DIM = HEAD_DIM // 2
ROPE_HALF = AXIS_DIM // 2
NA_WIN_H = 8
NA_WIN_W = 16
NA_BIAS_H = 2 * NA_WIN_H - 1
NA_BIAS_W = 2 * NA_WIN_W - 1
D_FF = 4 * D_MODEL
NORM_EPS = 1e-6
LOG2_E = 1.4426950408889634
NA_SCORE_SCALE = HEAD_DIM ** -0.5 * LOG2_E
HD = N_HEADS * HEAD_DIM
A_QKV = (N_HEADS + 2 * A_KV_HEADS) * HEAD_DIM

ROW_TILE = 512
FLASH_TQ = 256
FLASH_TK = 512
FLASH_HEADS = 4
FLASH_L_ROWS = 16
NA_Q_ROWS = 4
NA_K_ROWS = 12
NA_HEADS = 4
NA_NQ = NA_Q_ROWS * GRID_W
NA_NK = NA_K_ROWS * GRID_W
FF_CHUNK = 1024
VMEM_LIMIT_BYTES = 56 * 1024 * 1024

BF16 = jnp.bfloat16
F32 = jnp.float32


def _params(n_axes):
    return pltpu.CompilerParams(
        dimension_semantics=("arbitrary",) * n_axes, vmem_limit_bytes=VMEM_LIMIT_BYTES)


def _resident(shape):
    zeros = (0,) * len(shape)
    return pl.BlockSpec(shape, lambda *_: zeros, pipeline_mode=pl.Buffered(1))


def _rms(x, gain):
    ms = jnp.mean(x * x, axis=-1, keepdims=True)
    return x * lax.rsqrt(ms + NORM_EPS) * gain


def _qkv_a_kernel(x_ref, g_ref, wt_ref, qg_ref, kg_ref, cos_ref, sin_ref,
                  q_ref, k_ref, v_ref, s_ref, *, tm):
    h = _rms(x_ref[0], g_ref[...]).astype(BF16)
    qkv_t = lax.dot_general(wt_ref[...], h, (((1,), (1,)), ((), ())),
                            preferred_element_type=F32)
    n_slots = N_HEADS + 2 * A_KV_HEADS
    s_ref[...] = qkv_t.reshape(n_slots, HEAD_DIM, tm)

    def norm_rope(lo, hi, gain_ref, store):
        t = s_ref[lo:hi]
        r = lax.rsqrt(jnp.sum(t * t, axis=1, keepdims=True) * (1.0 / HEAD_DIM) + NORM_EPS)
        for a in range(2):
            d1 = a * AXIS_DIM
            d2 = d1 + ROPE_HALF
            x1 = s_ref[lo:hi, d1:d1 + ROPE_HALF, :] * r * gain_ref[d1:d1 + ROPE_HALF, :]
            x2 = s_ref[lo:hi, d2:d2 + ROPE_HALF, :] * r * gain_ref[d2:d2 + ROPE_HALF, :]
            c = cos_ref[a * ROPE_HALF:(a + 1) * ROPE_HALF, :]
            s = sin_ref[a * ROPE_HALF:(a + 1) * ROPE_HALF, :]
            store(d1, x1 * c - x2 * s)
            store(d2, x2 * c + x1 * s)

    def store_q(d, val):
        q_ref[0, :, d:d + ROPE_HALF, :] = val.astype(BF16)

    def store_k(d, val):
        s_ref[N_HEADS:N_HEADS + A_KV_HEADS, d:d + ROPE_HALF, :] = val

    norm_rope(0, N_HEADS, qg_ref, store_q)
    norm_rope(N_HEADS, N_HEADS + A_KV_HEADS, kg_ref, store_k)
    k_nat = s_ref[N_HEADS:N_HEADS + A_KV_HEADS].reshape(A_KV_HEADS * HEAD_DIM, tm).T
    for g in range(A_KV_HEADS):
        k_ref[0, g] = k_nat[:, g * HEAD_DIM:(g + 1) * HEAD_DIM].astype(BF16)
        for c in range(tm // FLASH_TK):
            v_ref[0, g, c] = s_ref[N_HEADS + A_KV_HEADS + g, :,
                                   c * FLASH_TK:(c + 1) * FLASH_TK].astype(BF16)


def _qkv_a(x, gain, w_t, q_gain, k_gain, cos_t, sin_t):
    b, s, _ = x.shape
    tm = ROW_TILE
    n_slots = N_HEADS + 2 * A_KV_HEADS
    return pl.pallas_call(
        functools.partial(_qkv_a_kernel, tm=tm),
        grid=(b, s // tm),
        in_specs=[
            pl.BlockSpec((1, tm, D_MODEL), lambda bi, i: (bi, i, 0)),
            _resident((1, D_MODEL)),
            _resident((A_QKV, D_MODEL)),
            _resident((HEAD_DIM, tm)),
            _resident((HEAD_DIM, tm)),
            pl.BlockSpec((AXIS_DIM, tm), lambda bi, i: (0, i)),
            pl.BlockSpec((AXIS_DIM, tm), lambda bi, i: (0, i)),
        ],
        out_specs=[
            pl.BlockSpec((1, N_HEADS, HEAD_DIM, tm), lambda bi, i: (bi, 0, 0, i)),
            pl.BlockSpec((1, A_KV_HEADS, tm, HEAD_DIM), lambda bi, i: (bi, 0, i, 0)),
            pl.BlockSpec((1, A_KV_HEADS, tm // FLASH_TK, HEAD_DIM, FLASH_TK),
                         lambda bi, i: (bi, 0, i, 0, 0)),
        ],
        out_shape=[
            jax.ShapeDtypeStruct((b, N_HEADS, HEAD_DIM, s), BF16),
            jax.ShapeDtypeStruct((b, A_KV_HEADS, s, HEAD_DIM), BF16),
            jax.ShapeDtypeStruct((b, A_KV_HEADS, s // FLASH_TK, HEAD_DIM, FLASH_TK), BF16),
        ],
        scratch_shapes=[pltpu.VMEM((n_slots, HEAD_DIM, tm), F32)],
        compiler_params=_params(2),
        name="qkv_a",
    )(x, gain, w_t, q_gain, k_gain, cos_t, sin_t)


def _flash_kernel(q_ref, k_ref, v_ref, o_ref, s_ref, *, n_chunks, tq):
    def scores(c, slot, e):
        start = pl.multiple_of(c * FLASH_TK, FLASH_TK)
        k_c = k_ref[0, 0, pl.ds(start, FLASH_TK), :]
        s_ref[slot, e] = jnp.dot(k_c, q_ref[0, e], preferred_element_type=F32)

    ones = jnp.ones((FLASH_L_ROWS, FLASH_TK), BF16)

    def update(c, slot, e, stat):
        m, acc = stat
        s_t = s_ref[slot, e]
        m_new = jnp.maximum(m, jnp.max(s_t, axis=0, keepdims=True))
        alpha = jnp.exp2(m - m_new)
        p = jnp.exp2(s_t - m_new).astype(BF16)
        v_ext = jnp.concatenate([v_ref[0, 0, c], ones], axis=0)
        acc = alpha * acc + jnp.dot(v_ext, p, preferred_element_type=F32)
        return m_new, acc

    def step(c, slot, stats, issue_next):
        new = []
        for e in range(FLASH_HEADS):
            if issue_next:
                scores(c + 1, 1 - slot, e)
            new.append(update(c, slot, e, stats[e]))
        return tuple(new)

    def body(j, stats):
        stats = step(2 * j, 0, stats, True)
        return step(2 * j + 1, 1, stats, True)

    for e in range(FLASH_HEADS):
        scores(0, 0, e)
    stats = tuple((jnp.full((1, tq), -jnp.inf, F32),
                   jnp.zeros((HEAD_DIM + FLASH_L_ROWS, tq), F32)) for _ in range(FLASH_HEADS))
    stats = lax.fori_loop(0, n_chunks // 2 - 1, body, stats)
    stats = step(n_chunks - 2, 0, stats, True)
    stats = step(n_chunks - 1, 1, stats, False)
    o_t = jnp.concatenate([acc[:HEAD_DIM] / acc[HEAD_DIM:HEAD_DIM + 1] for (_, acc) in stats],
                          axis=0)
    o_ref[0] = o_t.T.astype(BF16)


def _flash(q_t, k, v_t):
    b, _, _, s = q_t.shape
    tq = FLASH_TQ
    n_chunks = s // FLASH_TK
    pairs = N_HEADS // FLASH_HEADS
    per_kv = A_GROUP // FLASH_HEADS
    return pl.pallas_call(
        functools.partial(_flash_kernel, n_chunks=n_chunks, tq=tq),
        grid=(b, pairs, s // tq),
        in_specs=[
            pl.BlockSpec((1, FLASH_HEADS, HEAD_DIM, tq), lambda bi, hp, i: (bi, hp, 0, i)),
            pl.BlockSpec((1, 1, s, HEAD_DIM), lambda bi, hp, i: (bi, hp // per_kv, 0, 0)),
            pl.BlockSpec((1, 1, n_chunks, HEAD_DIM, FLASH_TK),
                         lambda bi, hp, i: (bi, hp // per_kv, 0, 0, 0)),
        ],
        out_specs=pl.BlockSpec((1, tq, FLASH_HEADS * HEAD_DIM), lambda bi, hp, i: (bi, i, hp)),
        out_shape=jax.ShapeDtypeStruct((b, s, HD), BF16),
        scratch_shapes=[pltpu.VMEM((2, FLASH_HEADS, FLASH_TK, tq), F32)],
        compiler_params=_params(3),
        name="flash_a",
    )(q_t, k, v_t)


def _proj_res_kernel(a_ref, w_ref, r_ref, o_ref):
    o_ref[0] = r_ref[0] + jnp.dot(a_ref[0], w_ref[...], preferred_element_type=F32)


def _proj_res(a, w, res):
    b, s, _ = res.shape
    tm = ROW_TILE
    return pl.pallas_call(
        _proj_res_kernel,
        grid=(b, s // tm),
        in_specs=[
            pl.BlockSpec((1, tm, HD), lambda bi, i: (bi, i, 0)),
            _resident((HD, D_MODEL)),
            pl.BlockSpec((1, tm, D_MODEL), lambda bi, i: (bi, i, 0)),
        ],
        out_specs=pl.BlockSpec((1, tm, D_MODEL), lambda bi, i: (bi, i, 0)),
        out_shape=jax.ShapeDtypeStruct((b, s, D_MODEL), F32),
        compiler_params=_params(2),
        name="proj_res",
    )(a, w, res)


def _mlp_kernel(x_ref, g_ref, win_ref, wout_ref, gf_ref, o_ref, *, final):
    x = x_ref[0]
    h = _rms(x, g_ref[...]).astype(BF16)
    acc = x
    for f in range(D_FF // FF_CHUNK):
        u = jnp.dot(h, win_ref[:, f * FF_CHUNK:(f + 1) * FF_CHUNK], preferred_element_type=F32)
        u = jnp.maximum(u, 0.0)
        acc = acc + jnp.dot((u * u).astype(BF16), wout_ref[f * FF_CHUNK:(f + 1) * FF_CHUNK, :],
                            preferred_element_type=F32)
    if final:
        acc = _rms(acc, gf_ref[...])
    o_ref[0] = acc


def _mlp(x, gain, w_in, w_out, gain_final, final):
    b, s, _ = x.shape
    tm = ROW_TILE
    return pl.pallas_call(
        functools.partial(_mlp_kernel, final=final),
        grid=(b, s // tm),
        in_specs=[
            pl.BlockSpec((1, tm, D_MODEL), lambda bi, i: (bi, i, 0)),
            _resident((1, D_MODEL)),
            _resident((D_MODEL, D_FF)),
            _resident((D_FF, D_MODEL)),
            _resident((1, D_MODEL)),
        ],
        out_specs=pl.BlockSpec((1, tm, D_MODEL), lambda bi, i: (bi, i, 0)),
        out_shape=jax.ShapeDtypeStruct((b, s, D_MODEL), F32),
        compiler_params=_params(2),
        name="mlp_final" if final else "mlp",
    )(x, gain, w_in, w_out, gain_final)


def _qkv_b_kernel(x_ref, g_ref, wqv_t_ref, wk_ref, q_ref, k_ref, v_ref, *, tm):
    h = _rms(x_ref[0], g_ref[...]).astype(BF16)
    k_ref[0] = jnp.dot(h, wk_ref[...], preferred_element_type=F32).astype(BF16)
    nt = (((1,), (1,)), ((), ()))
    q_t = lax.dot_general(wqv_t_ref[0:HD, :], h, nt, preferred_element_type=F32)
    q_ref[0] = (q_t * NA_SCORE_SCALE).astype(BF16)
    v_t = lax.dot_general(wqv_t_ref[HD:2 * HD, :], h, nt, preferred_element_type=F32)
    for hd in range(N_HEADS):
        for c in range(tm // NA_NQ):
            v_ref[0, hd, c] = v_t[hd * HEAD_DIM:(hd + 1) * HEAD_DIM,
                                  c * NA_NQ:(c + 1) * NA_NQ].astype(BF16)


def _qkv_b(x, gain, wqv_t, wk):
    b, s, _ = x.shape
    tm = ROW_TILE
    return pl.pallas_call(
        functools.partial(_qkv_b_kernel, tm=tm),
        grid=(b, s // tm),
        in_specs=[
            pl.BlockSpec((1, tm, D_MODEL), lambda bi, i: (bi, i, 0)),
            _resident((1, D_MODEL)),
            _resident((2 * HD, D_MODEL)),
            _resident((D_MODEL, HD)),
        ],
        out_specs=[
            pl.BlockSpec((1, HD, tm), lambda bi, i: (bi, 0, i)),
            pl.BlockSpec((1, tm, HD), lambda bi, i: (bi, i, 0)),
            pl.BlockSpec((1, N_HEADS, tm // NA_NQ, HEAD_DIM, NA_NQ), lambda bi, i: (bi, 0, i, 0, 0)),
        ],
        out_shape=[
            jax.ShapeDtypeStruct((b, HD, s), BF16),
            jax.ShapeDtypeStruct((b, s, HD), BF16),
            jax.ShapeDtypeStruct((b, N_HEADS, s // NA_NQ, HEAD_DIM, NA_NQ), BF16),
        ],
        compiler_params=_params(2),
        name="qkv_b",
    )(x, gain, wqv_t, wk)


def _natten_kernel(q_ref, qn_ref, k_ref, v_ref, b_ref, o_ref, s_ref, *, rows):
    u = pl.program_id(2)
    n_blocks = rows // NA_Q_ROWS
    width = NA_HEADS * HEAD_DIM
    row_id = lax.broadcasted_iota(jnp.int32, (width, NA_NQ), 0)
    ones = jnp.ones((FLASH_L_ROWS, NA_NK), BF16)

    def key_base(blk):
        return jnp.clip(NA_Q_ROWS * blk - NA_WIN_H // 2, 0, rows - NA_K_ROWS)

    def scores(blk, q_blk, slot, e):
        variant = jnp.where(blk == 0, 0, jnp.where(blk == n_blocks - 1, 2, 1))
        start = pl.multiple_of(key_base(blk) * GRID_W, NA_NQ)
        k_w = k_ref[0, pl.ds(start, NA_NK), :]
        mine = (row_id >= e * HEAD_DIM) & (row_id < (e + 1) * HEAD_DIM)
        q_e = jnp.where(mine, q_blk, jnp.zeros_like(q_blk))
        s_ref[slot, e] = jnp.dot(k_w, q_e, preferred_element_type=F32) + b_ref[variant, e]

    def attend(blk, slot, e):
        chunk0 = key_base(blk) // NA_Q_ROWS
        s_t = s_ref[slot, e]
        m = jnp.max(s_t, axis=0, keepdims=True)
        p = jnp.exp2(s_t - m).astype(BF16)
        v_t = [v_ref[0, e, chunk0 + j] for j in range(NA_K_ROWS // NA_Q_ROWS)]
        v_ext = jnp.concatenate([jnp.concatenate(v_t, axis=1), ones], axis=0)
        acc = jnp.dot(v_ext, p, preferred_element_type=F32)
        return acc[:HEAD_DIM] / acc[HEAD_DIM:HEAD_DIM + 1]

    @pl.when(u == 0)
    def _():
        for e in range(NA_HEADS):
            scores(0, q_ref[0, :, 0:NA_NQ], 0, e)

    def half(blk, slot, blk_next, q_next, row0):
        outs = []
        for e in range(NA_HEADS):
            scores(blk_next, q_next, 1 - slot, e)
            outs.append(attend(blk, slot, e))
        o_ref[0, row0:row0 + NA_NQ, :] = jnp.concatenate(outs, axis=0).T.astype(BF16)

    half(2 * u, 0, 2 * u + 1, q_ref[0, :, NA_NQ:2 * NA_NQ], 0)
    half(2 * u + 1, 1, jnp.minimum(2 * u + 2, n_blocks - 1), qn_ref[0], NA_NQ)


def _natten(q_t, k, v_t, bias):
    b, s, _ = k.shape
    rows = s // GRID_W
    n_blocks = rows // NA_Q_ROWS
    groups = N_HEADS // NA_HEADS
    width = NA_HEADS * HEAD_DIM
    return pl.pallas_call(
        functools.partial(_natten_kernel, rows=rows),
        grid=(b, groups, n_blocks // 2),
        in_specs=[
            pl.BlockSpec((1, width, 2 * NA_NQ), lambda bi, hg, u: (bi, hg, u)),
            pl.BlockSpec((1, width, NA_NQ),
                         lambda bi, hg, u: (bi, hg, jnp.minimum(2 * u + 2, n_blocks - 1))),
            pl.BlockSpec((1, s, width), lambda bi, hg, u: (bi, 0, hg)),
            pl.BlockSpec((1, NA_HEADS, s // NA_NQ, HEAD_DIM, NA_NQ),
                         lambda bi, hg, u: (bi, hg, 0, 0, 0)),
            pl.BlockSpec((3, NA_HEADS, NA_NK, NA_NQ), lambda bi, hg, u: (0, hg, 0, 0),
                         pipeline_mode=pl.Buffered(1)),
        ],
        out_specs=pl.BlockSpec((1, 2 * NA_NQ, width), lambda bi, hg, u: (bi, u, hg)),
        out_shape=jax.ShapeDtypeStruct((b, s, HD), BF16),
        scratch_shapes=[pltpu.VMEM((2, NA_HEADS, NA_NK, NA_NQ), F32)],
        compiler_params=_params(3),
        name="natten_b",
    )(q_t, q_t, k, v_t, bias)


def _rope_tables_t(seq):
    t = jnp.arange(seq)
    row = (t // GRID_W).astype(F32)
    col = (t % GRID_W).astype(F32)
    inv = ROPE_THETA ** (-jnp.arange(0, AXIS_DIM, 2, dtype=F32) / AXIS_DIM)
    ang = jnp.concatenate([inv[:, None] * row[None, :], inv[:, None] * col[None, :]], axis=0)
    return jnp.cos(ang), jnp.sin(ang)


def _natten_bias(rel_bias, rows):
    n_blocks = rows // NA_Q_ROWS
    t = jnp.array([0, 1, n_blocks - 1])[:, None, None]
    a = jnp.arange(NA_Q_ROWS)[None, :, None]
    i = jnp.arange(NA_K_ROWS)[None, None, :]
    r = NA_Q_ROWS * t + a
    r0 = jnp.clip(r - NA_WIN_H // 2, 0, rows - NA_WIN_H)
    kr = jnp.clip(NA_Q_ROWS * t - NA_WIN_H // 2, 0, rows - NA_K_ROWS) + i
    valid_r = (kr >= r0) & (kr < r0 + NA_WIN_H)
    c = jnp.arange(GRID_W)[:, None]
    kc = jnp.arange(GRID_W)[None, :]
    c0 = jnp.clip(c - NA_WIN_W // 2, 0, GRID_W - NA_WIN_W)
    valid_c = (kc >= c0) & (kc < c0 + NA_WIN_W)
    pick_r = ((kr - r + NA_WIN_H - 1)[..., None] == jnp.arange(NA_BIAS_H)).astype(F32)
    pick_c = ((kc - c + NA_WIN_W - 1)[..., None] == jnp.arange(NA_BIAS_W)).astype(F32)
    vals = jnp.einsum("vaid,hde,cke->vhikac", pick_r, rel_bias.astype(F32), pick_c,
                      precision=lax.Precision.HIGHEST)
    valid = (valid_r.transpose(0, 2, 1)[:, None, :, None, :, None]
             & valid_c.T[None, None, None, :, None, :])
    vals = jnp.where(valid, vals * LOG2_E, -jnp.inf)
    return vals.reshape(3, N_HEADS, NA_NK, NA_NQ)


def _trunk(x, p):
    _, s, _ = x.shape
    depth = p["norm_mix"].shape[0]
    for i in range(depth):
        j = i // 2
        g_mix = p["norm_mix"][i][None, :]
        if i % 2 == 0:
            q_t, k, v_t = _qkv_a(x, g_mix, p["a_w_qkv_t"][j], p["a_q_gain"][j], p["a_k_gain"][j],
                                 p["cos_t"], p["sin_t"])
            o = _flash(q_t, k, v_t)
            x = _proj_res(o, p["a_w_o"][j], x)
        else:
            q_t, k, v_t = _qkv_b(x, g_mix, p["b_w_qv_t"][j], p["b_w_k"][j])
            o = _natten(q_t, k, v_t, p["b_bias"][j])
            x = _proj_res(o, p["b_w_o"][j], x)
        x = _mlp(x, p["norm_mlp"][i][None, :], p["mlp_w_in"][i], p["mlp_w_out"][i],
                 p["norm_final"][None, :], final=(i == depth - 1))
    return x


def kernel(x_prompt, x_sample, norm_mix, norm_mlp, norm_final, a_w_qkv, a_q_norm, a_k_norm, a_w_o,
           b_w_qkv, b_rel_bias, b_w_o, mlp_w_in, mlp_w_out):
    scale = HEAD_DIM ** -0.5 * LOG2_E
    shared = {
        "norm_mix": norm_mix, "norm_mlp": norm_mlp, "norm_final": norm_final,
        "a_w_qkv_t": jnp.swapaxes(a_w_qkv, 1, 2).astype(BF16),
        "a_q_gain": jnp.broadcast_to((a_q_norm * scale)[:, :, None],
                                     a_q_norm.shape + (ROW_TILE,)),
        "a_k_gain": jnp.broadcast_to(a_k_norm[:, :, None], a_k_norm.shape + (ROW_TILE,)),
        "a_w_o": a_w_o.astype(BF16),
        "b_w_qv_t": jnp.swapaxes(jnp.concatenate([b_w_qkv[:, :, :HD], b_w_qkv[:, :, 2 * HD:]],
                                                 axis=2), 1, 2).astype(BF16),
        "b_w_k": b_w_qkv[:, :, HD:2 * HD].astype(BF16),
        "b_w_o": b_w_o.astype(BF16),
        "mlp_w_in": mlp_w_in.astype(BF16),
        "mlp_w_out": mlp_w_out.astype(BF16),
    }
    outs = []
    for x in (x_prompt, x_sample):
        s = x.shape[1]
        cos_t, sin_t = _rope_tables_t(s)
        p = dict(shared, cos_t=cos_t, sin_t=sin_t,
                 b_bias=jnp.stack([_natten_bias(rb, s // GRID_W) for rb in b_rel_bias]))
        outs.append(_trunk(x, p))
    return tuple(outs)
```

```python
import functools

import jax
import jax.numpy as jnp
from jax import lax
from jax.experimental import pallas as pl
from jax.experimental.pallas import tpu as pltpu

D_MODEL = 1024
GRID_W = 64
N_HEADS = 16
HEAD_DIM = 64
A_KV_HEADS = 4
A_GROUP = N_HEADS // A_KV_HEADS
ROPE_THETA = 10000.0
AXIS_DIM = HEAD_DIM // 2
ROPE_HALF = AXIS_DIM // 2
NA_WIN_H = 8
NA_WIN_W = 16
NA_BIAS_H = 2 * NA_WIN_H - 1
NA_BIAS_W = 2 * NA_WIN_W - 1
D_FF = 4 * D_MODEL
NORM_EPS = 1e-6
LOG2_E = 1.4426950408889634
NA_SCORE_SCALE = HEAD_DIM ** -0.5 * LOG2_E
HD = N_HEADS * HEAD_DIM
A_QKV = (N_HEADS + 2 * A_KV_HEADS) * HEAD_DIM

ROW_TILE = 512
FLASH_TQ = 256
FLASH_TK = 512
FLASH_HEADS = 4
FLASH_L_ROWS = 16
NA_Q_ROWS = 4
NA_K_ROWS = 12
NA_HEADS = 4
NA_NQ = NA_Q_ROWS * GRID_W
NA_NK = NA_K_ROWS * GRID_W
FF_CHUNK = 1024
VMEM_LIMIT_BYTES = 56 * 1024 * 1024

BF16 = jnp.bfloat16
F32 = jnp.float32


def _params(n_axes):
    return pltpu.CompilerParams(
        dimension_semantics=("arbitrary",) * n_axes, vmem_limit_bytes=VMEM_LIMIT_BYTES)


def _resident(shape):
    zeros = (0,) * len(shape)
    return pl.BlockSpec(shape, lambda *_: zeros, pipeline_mode=pl.Buffered(1))


def _rms(x, gain):
    ms = jnp.mean(x * x, axis=-1, keepdims=True)
    return x * lax.rsqrt(ms + NORM_EPS) * gain


def _qkv_a_kernel(x_ref, g_ref, wt_ref, qg_ref, kg_ref, cos_ref, sin_ref,
                  q_ref, k_ref, v_ref, s_ref, *, tm):
    h = _rms(x_ref[0], g_ref[...]).astype(BF16)
    qkv_t = lax.dot_general(wt_ref[...], h, (((1,), (1,)), ((), ())),
                            preferred_element_type=F32)
    n_slots = N_HEADS + 2 * A_KV_HEADS
    s_ref[...] = qkv_t.reshape(n_slots, HEAD_DIM, tm)

    def norm_rope(lo, hi, gain_ref, store):
        t = s_ref[lo:hi]
        r = lax.rsqrt(jnp.sum(t * t, axis=1, keepdims=True) * (1.0 / HEAD_DIM) + NORM_EPS)
        for a in range(2):
            d1 = a * AXIS_DIM
            d2 = d1 + ROPE_HALF
            x1 = s_ref[lo:hi, d1:d1 + ROPE_HALF, :] * r * gain_ref[d1:d1 + ROPE_HALF, :]
            x2 = s_ref[lo:hi, d2:d2 + ROPE_HALF, :] * r * gain_ref[d2:d2 + ROPE_HALF, :]
            c = cos_ref[a * ROPE_HALF:(a + 1) * ROPE_HALF, :]
            s = sin_ref[a * ROPE_HALF:(a + 1) * ROPE_HALF, :]
            store(d1, x1 * c - x2 * s)
            store(d2, x2 * c + x1 * s)

    def store_q(d, val):
        q_ref[0, :, d:d + ROPE_HALF, :] = val.astype(BF16)

    def store_k(d, val):
        s_ref[N_HEADS:N_HEADS + A_KV_HEADS, d:d + ROPE_HALF, :] = val

    norm_rope(0, N_HEADS, qg_ref, store_q)
    norm_rope(N_HEADS, N_HEADS + A_KV_HEADS, kg_ref, store_k)
    k_nat = s_ref[N_HEADS:N_HEADS + A_KV_HEADS].reshape(A_KV_HEADS * HEAD_DIM, tm).T
    for g in range(A_KV_HEADS):
        k_ref[0, g] = k_nat[:, g * HEAD_DIM:(g + 1) * HEAD_DIM].astype(BF16)
        for c in range(tm // FLASH_TK):
            v_ref[0, g, c] = s_ref[N_HEADS + A_KV_HEADS + g, :,
                                   c * FLASH_TK:(c + 1) * FLASH_TK].astype(BF16)


def _qkv_a(x, gain, w_t, q_gain, k_gain, cos_t, sin_t):
    b, s, _ = x.shape
    tm = ROW_TILE
    n_slots = N_HEADS + 2 * A_KV_HEADS
    return pl.pallas_call(
        functools.partial(_qkv_a_kernel, tm=tm),
        grid=(b, s // tm),
        in_specs=[
            pl.BlockSpec((1, tm, D_MODEL), lambda bi, i: (bi, i, 0)),
            _resident((1, D_MODEL)),
            _resident((A_QKV, D_MODEL)),
            _resident((HEAD_DIM, tm)),
            _resident((HEAD_DIM, tm)),
            pl.BlockSpec((AXIS_DIM, tm), lambda bi, i: (0, i)),
            pl.BlockSpec((AXIS_DIM, tm), lambda bi, i: (0, i)),
        ],
        out_specs=[
            pl.BlockSpec((1, N_HEADS, HEAD_DIM, tm), lambda bi, i: (bi, 0, 0, i)),
            pl.BlockSpec((1, A_KV_HEADS, tm, HEAD_DIM), lambda bi, i: (bi, 0, i, 0)),
            pl.BlockSpec((1, A_KV_HEADS, tm // FLASH_TK, HEAD_DIM, FLASH_TK),
                         lambda bi, i: (bi, 0, i, 0, 0)),
        ],
        out_shape=[
            jax.ShapeDtypeStruct((b, N_HEADS, HEAD_DIM, s), BF16),
            jax.ShapeDtypeStruct((b, A_KV_HEADS, s, HEAD_DIM), BF16),
            jax.ShapeDtypeStruct((b, A_KV_HEADS, s // FLASH_TK, HEAD_DIM, FLASH_TK), BF16),
        ],
        scratch_shapes=[pltpu.VMEM((n_slots, HEAD_DIM, tm), F32)],
        compiler_params=_params(2),
        name="qkv_a",
    )(x, gain, w_t, q_gain, k_gain, cos_t, sin_t)


def _flash_kernel(q_ref, k_ref, v_ref, o_ref, s_ref, *, n_chunks, tq):
    def scores(c, slot, e):
        start = pl.multiple_of(c * FLASH_TK, FLASH_TK)
        k_c = k_ref[0, 0, pl.ds(start, FLASH_TK), :]
        s_ref[slot, e] = jnp.dot(k_c, q_ref[0, e], preferred_element_type=F32)

    ones = jnp.ones((FLASH_L_ROWS, FLASH_TK), BF16)

    def update(c, slot, e, stat):
        m, acc = stat
        s_t = s_ref[slot, e]
        m_new = jnp.maximum(m, jnp.max(s_t, axis=0, keepdims=True))
        alpha = jnp.exp2(m - m_new)
        p = jnp.exp2(s_t - m_new).astype(BF16)
        v_ext = jnp.concatenate([v_ref[0, 0, c], ones], axis=0)
        acc = alpha * acc + jnp.dot(v_ext, p, preferred_element_type=F32)
        return m_new, acc

    def step(c, slot, stats, issue_next):
        new = []
        for e in range(FLASH_HEADS):
            if issue_next:
                scores(c + 1, 1 - slot, e)
            new.append(update(c, slot, e, stats[e]))
        return tuple(new)

    def body(j, stats):
        stats = step(2 * j, 0, stats, True)
        return step(2 * j + 1, 1, stats, True)

    for e in range(FLASH_HEADS):
        scores(0, 0, e)
    stats = tuple((jnp.full((1, tq), -jnp.inf, F32),
                   jnp.zeros((HEAD_DIM + FLASH_L_ROWS, tq), F32)) for _ in range(FLASH_HEADS))
    stats = lax.fori_loop(0, n_chunks // 2 - 1, body, stats, unroll=True)
    stats = step(n_chunks - 2, 0, stats, True)
    stats = step(n_chunks - 1, 1, stats, False)
    o_t = jnp.concatenate([acc[:HEAD_DIM] / acc[HEAD_DIM:HEAD_DIM + 1] for (_, acc) in stats],
                          axis=0)
    o_ref[0] = o_t.T.astype(BF16)


def _flash(q_t, k, v_t):
    b, _, _, s = q_t.shape
    tq = FLASH_TQ
    n_chunks = s // FLASH_TK
    pairs = N_HEADS // FLASH_HEADS
    per_kv = A_GROUP // FLASH_HEADS
    return pl.pallas_call(
        functools.partial(_flash_kernel, n_chunks=n_chunks, tq=tq),
        grid=(b, pairs, s // tq),
        in_specs=[
            pl.BlockSpec((1, FLASH_HEADS, HEAD_DIM, tq), lambda bi, hp, i: (bi, hp, 0, i)),
            pl.BlockSpec((1, 1, s, HEAD_DIM), lambda bi, hp, i: (bi, hp // per_kv, 0, 0)),
            pl.BlockSpec((1, 1, n_chunks, HEAD_DIM, FLASH_TK),
                         lambda bi, hp, i: (bi, hp // per_kv, 0, 0, 0)),
        ],
        out_specs=pl.BlockSpec((1, tq, FLASH_HEADS * HEAD_DIM), lambda bi, hp, i: (bi, i, hp)),
        out_shape=jax.ShapeDtypeStruct((b, s, HD), BF16),
        scratch_shapes=[pltpu.VMEM((2, FLASH_HEADS, FLASH_TK, tq), F32)],
        compiler_params=_params(3),
        name="flash_a",
    )(q_t, k, v_t)


def _proj_res_kernel(a_ref, w_ref, r_ref, o_ref):
    o_ref[0] = r_ref[0] + jnp.dot(a_ref[0], w_ref[...], preferred_element_type=F32)


def _proj_res(a, w, res):
    b, s, _ = res.shape
    tm = ROW_TILE
    return pl.pallas_call(
        _proj_res_kernel,
        grid=(b, s // tm),
        in_specs=[
            pl.BlockSpec((1, tm, HD), lambda bi, i: (bi, i, 0)),
            _resident((HD, D_MODEL)),
            pl.BlockSpec((1, tm, D_MODEL), lambda bi, i: (bi, i, 0)),
        ],
        out_specs=pl.BlockSpec((1, tm, D_MODEL), lambda bi, i: (bi, i, 0)),
        out_shape=jax.ShapeDtypeStruct((b, s, D_MODEL), F32),
        compiler_params=_params(2),
        name="proj_res",
    )(a, w, res)


def _mlp_kernel(x_ref, g_ref, win_ref, wout_ref, gf_ref, o_ref, *, final):
    x = x_ref[0]
    h = _rms(x, g_ref[...]).astype(BF16)
    acc = x
    for f in range(D_FF // FF_CHUNK):
        u = jnp.dot(h, win_ref[:, f * FF_CHUNK:(f + 1) * FF_CHUNK], preferred_element_type=F32)
        u = jnp.maximum(u, 0.0)
        acc = acc + jnp.dot((u * u).astype(BF16), wout_ref[f * FF_CHUNK:(f + 1) * FF_CHUNK, :],
                            preferred_element_type=F32)
    if final:
        acc = _rms(acc, gf_ref[...])
    o_ref[0] = acc


def _mlp(x, gain, w_in, w_out, gain_final, final):
    b, s, _ = x.shape
    tm = ROW_TILE
    return pl.pallas_call(
        functools.partial(_mlp_kernel, final=final),
        grid=(b, s // tm),
        in_specs=[
            pl.BlockSpec((1, tm, D_MODEL), lambda bi, i: (bi, i, 0)),
            _resident((1, D_MODEL)),
            _resident((D_MODEL, D_FF)),
            _resident((D_FF, D_MODEL)),
            _resident((1, D_MODEL)),
        ],
        out_specs=pl.BlockSpec((1, tm, D_MODEL), lambda bi, i: (bi, i, 0)),
        out_shape=jax.ShapeDtypeStruct((b, s, D_MODEL), F32),
        compiler_params=_params(2),
        name="mlp_final" if final else "mlp",
    )(x, gain, w_in, w_out, gain_final)


def _qkv_b_kernel(x_ref, g_ref, wqv_t_ref, wk_ref, q_ref, k_ref, v_ref, *, tm):
    h = _rms(x_ref[0], g_ref[...]).astype(BF16)
    k_ref[0] = jnp.dot(h, wk_ref[...], preferred_element_type=F32).astype(BF16)
    nt = (((1,), (1,)), ((), ()))
    q_t = lax.dot_general(wqv_t_ref[0:HD, :], h, nt, preferred_element_type=F32)
    q_ref[0] = (q_t * NA_SCORE_SCALE).astype(BF16)
    v_t = lax.dot_general(wqv_t_ref[HD:2 * HD, :], h, nt, preferred_element_type=F32)
    for hd in range(N_HEADS):
        for c in range(tm // NA_NQ):
            v_ref[0, hd, c] = v_t[hd * HEAD_DIM:(hd + 1) * HEAD_DIM,
                                  c * NA_NQ:(c + 1) * NA_NQ].astype(BF16)


def _qkv_b(x, gain, wqv_t, wk):
    b, s, _ = x.shape
    tm = ROW_TILE
    return pl.pallas_call(
        functools.partial(_qkv_b_kernel, tm=tm),
        grid=(b, s // tm),
        in_specs=[
            pl.BlockSpec((1, tm, D_MODEL), lambda bi, i: (bi, i, 0)),
            _resident((1, D_MODEL)),
            _resident((2 * HD, D_MODEL)),
            _resident((D_MODEL, HD)),
        ],
        out_specs=[
            pl.BlockSpec((1, HD, tm), lambda bi, i: (bi, 0, i)),
            pl.BlockSpec((1, tm, HD), lambda bi, i: (bi, i, 0)),
            pl.BlockSpec((1, N_HEADS, tm // NA_NQ, HEAD_DIM, NA_NQ), lambda bi, i: (bi, 0, i, 0, 0)),
        ],
        out_shape=[
            jax.ShapeDtypeStruct((b, HD, s), BF16),
            jax.ShapeDtypeStruct((b, s, HD), BF16),
            jax.ShapeDtypeStruct((b, N_HEADS, s // NA_NQ, HEAD_DIM, NA_NQ), BF16),
        ],
        compiler_params=_params(2),
        name="qkv_b",
    )(x, gain, wqv_t, wk)


def _natten_kernel(q_ref, qn_ref, k_ref, v_ref, b_ref, o_ref, s_ref, *, rows):
    u = pl.program_id(2)
    n_blocks = rows // NA_Q_ROWS
    width = NA_HEADS * HEAD_DIM
    row_id = lax.broadcasted_iota(jnp.int32, (width, NA_NQ), 0)
    ones = jnp.ones((FLASH_L_ROWS, NA_NK), BF16)

    def key_base(blk):
        return jnp.clip(NA_Q_ROWS * blk - NA_WIN_H // 2, 0, rows - NA_K_ROWS)

    def scores(blk, q_blk, slot, e):
        variant = jnp.where(blk == 0, 0, jnp.where(blk == n_blocks - 1, 2, 1))
        start = pl.multiple_of(key_base(blk) * GRID_W, NA_NQ)
        k_w = k_ref[0, pl.ds(start, NA_NK), :]
        mine = (row_id >= e * HEAD_DIM) & (row_id < (e + 1) * HEAD_DIM)
        q_e = jnp.where(mine, q_blk, jnp.zeros_like(q_blk))
        s_ref[slot, e] = jnp.dot(k_w, q_e, preferred_element_type=F32) + b_ref[variant, e]

    def attend(blk, slot, e):
        chunk0 = key_base(blk) // NA_Q_ROWS
        s_t = s_ref[slot, e]
        m = jnp.max(s_t, axis=0, keepdims=True)
        p = jnp.exp2(s_t - m).astype(BF16)
        v_t = [v_ref[0, e, chunk0 + j] for j in range(NA_K_ROWS // NA_Q_ROWS)]
        v_ext = jnp.concatenate([jnp.concatenate(v_t, axis=1), ones], axis=0)
        acc = jnp.dot(v_ext, p, preferred_element_type=F32)
        return acc[:HEAD_DIM] / acc[HEAD_DIM:HEAD_DIM + 1]

    @pl.when(u == 0)
    def _():
        for e in range(NA_HEADS):
            scores(0, q_ref[0, :, 0:NA_NQ], 0, e)

    def half(blk, slot, blk_next, q_next, row0):
        outs = []
        for e in range(NA_HEADS):
            scores(blk_next, q_next, 1 - slot, e)
            outs.append(attend(blk, slot, e))
        o_ref[0, row0:row0 + NA_NQ, :] = jnp.concatenate(outs, axis=0).T.astype(BF16)

    half(2 * u, 0, 2 * u + 1, q_ref[0, :, NA_NQ:2 * NA_NQ], 0)
    half(2 * u + 1, 1, jnp.minimum(2 * u + 2, n_blocks - 1), qn_ref[0], NA_NQ)


def _natten(q_t, k, v_t, bias):
    b, s, _ = k.shape
    rows = s // GRID_W
    n_blocks = rows // NA_Q_ROWS
    groups = N_HEADS // NA_HEADS
    width = NA_HEADS * HEAD_DIM
    return pl.pallas_call(
        functools.partial(_natten_kernel, rows=rows),
        grid=(groups, b, n_blocks // 2),
        in_specs=[
            pl.BlockSpec((1, width, 2 * NA_NQ), lambda hg, bi, u: (bi, hg, u)),
            pl.BlockSpec((1, width, NA_NQ),
                         lambda hg, bi, u: (bi, hg, jnp.minimum(2 * u + 2, n_blocks - 1))),
            pl.BlockSpec((1, s, width), lambda hg, bi, u: (bi, 0, hg)),
            pl.BlockSpec((1, NA_HEADS, s // NA_NQ, HEAD_DIM, NA_NQ),
                         lambda hg, bi, u: (bi, hg, 0, 0, 0)),
            pl.BlockSpec((3, NA_HEADS, NA_NK, NA_NQ), lambda hg, bi, u: (0, hg, 0, 0),
                         pipeline_mode=pl.Buffered(1)),
        ],
        out_specs=pl.BlockSpec((1, 2 * NA_NQ, width), lambda hg, bi, u: (bi, u, hg)),
        out_shape=jax.ShapeDtypeStruct((b, s, HD), BF16),
        scratch_shapes=[pltpu.VMEM((2, NA_HEADS, NA_NK, NA_NQ), F32)],
        compiler_params=_params(3),
        name="natten_b",
    )(q_t, q_t, k, v_t, bias)


def _rope_tables_t(seq):
    t = jnp.arange(seq)
    row = (t // GRID_W).astype(F32)
    col = (t % GRID_W).astype(F32)
    inv = ROPE_THETA ** (-jnp.arange(0, AXIS_DIM, 2, dtype=F32) / AXIS_DIM)
    ang = jnp.concatenate([inv[:, None] * row[None, :], inv[:, None] * col[None, :]], axis=0)
    return jnp.cos(ang), jnp.sin(ang)


def _natten_bias(rel_bias, rows):
    n_blocks = rows // NA_Q_ROWS
    t = jnp.array([0, 1, n_blocks - 1])[:, None, None]
    a = jnp.arange(NA_Q_ROWS)[None, :, None]
    i = jnp.arange(NA_K_ROWS)[None, None, :]
    r = NA_Q_ROWS * t + a
    r0 = jnp.clip(r - NA_WIN_H // 2, 0, rows - NA_WIN_H)
    kr = jnp.clip(NA_Q_ROWS * t - NA_WIN_H // 2, 0, rows - NA_K_ROWS) + i
    valid_r = (kr >= r0) & (kr < r0 + NA_WIN_H)
    c = jnp.arange(GRID_W)[:, None]
    kc = jnp.arange(GRID_W)[None, :]
    c0 = jnp.clip(c - NA_WIN_W // 2, 0, GRID_W - NA_WIN_W)
    valid_c = (kc >= c0) & (kc < c0 + NA_WIN_W)
    pick_r = ((kr - r + NA_WIN_H - 1)[..., None] == jnp.arange(NA_BIAS_H)).astype(F32)
    pick_c = ((kc - c + NA_WIN_W - 1)[..., None] == jnp.arange(NA_BIAS_W)).astype(F32)
    vals = jnp.einsum("vaid,hde,cke->vhikac", pick_r, rel_bias.astype(F32), pick_c,
                      precision=lax.Precision.HIGHEST)
    valid = (valid_r.transpose(0, 2, 1)[:, None, :, None, :, None]
             & valid_c.T[None, None, None, :, None, :])
    vals = jnp.where(valid, vals * LOG2_E, -jnp.inf)
    return vals.reshape(3, N_HEADS, NA_NK, NA_NQ)


def _trunk(x, p):
    _, s, _ = x.shape
    depth = p["norm_mix"].shape[0]
    for i in range(depth):
        j = i // 2
        g_mix = p["norm_mix"][i][None, :]
        if i % 2 == 0:
            q_t, k, v_t = _qkv_a(x, g_mix, p["a_w_qkv_t"][j], p["a_q_gain"][j], p["a_k_gain"][j],
                                 p["cos_t"], p["sin_t"])
            o = _flash(q_t, k, v_t)
            x = _proj_res(o, p["a_w_o"][j], x)
        else:
            q_t, k, v_t = _qkv_b(x, g_mix, p["b_w_qv_t"][j], p["b_w_k"][j])
            o = _natten(q_t, k, v_t, p["b_bias"][j])
            x = _proj_res(o, p["b_w_o"][j], x)
        x = _mlp(x, p["norm_mlp"][i][None, :], p["mlp_w_in"][i], p["mlp_w_out"][i],
                 p["norm_final"][None, :], final=(i == depth - 1))
    return x


def kernel(x_prompt, x_sample, norm_mix, norm_mlp, norm_final, a_w_qkv, a_q_norm, a_k_norm, a_w_o,
           b_w_qkv, b_rel_bias, b_w_o, mlp_w_in, mlp_w_out):
    scale = HEAD_DIM ** -0.5 * LOG2_E
    shared = {
        "norm_mix": norm_mix, "norm_mlp": norm_mlp, "norm_final": norm_final,
        "a_w_qkv_t": jnp.swapaxes(a_w_qkv, 1, 2).astype(BF16),
        "a_q_gain": jnp.broadcast_to((a_q_norm * scale)[:, :, None],
                                     a_q_norm.shape + (ROW_TILE,)),
        "a_k_gain": jnp.broadcast_to(a_k_norm[:, :, None], a_k_norm.shape + (ROW_TILE,)),
        "a_w_o": a_w_o.astype(BF16),
        "b_w_qv_t": jnp.swapaxes(jnp.concatenate([b_w_qkv[:, :, :HD], b_w_qkv[:, :, 2 * HD:]],
                                                 axis=2), 1, 2).astype(BF16),
        "b_w_k": b_w_qkv[:, :, HD:2 * HD].astype(BF16),
        "b_w_o": b_w_o.astype(BF16),
        "mlp_w_in": mlp_w_in.astype(BF16),
        "mlp_w_out": mlp_w_out.astype(BF16),
    }
    outs = []
    for x in (x_prompt, x_sample):
        s = x.shape[1]
        cos_t, sin_t = _rope_tables_t(s)
        p = dict(shared, cos_t=cos_t, sin_t=sin_t,
                 b_bias=jnp.stack([_natten_bias(rb, s // GRID_W) for rb in b_rel_bias]))
        outs.append(_trunk(x, p))
    return tuple(outs)
```

```python
import functools

import jax
import jax.numpy as jnp
from jax import lax
from jax.experimental import pallas as pl
from jax.experimental.pallas import tpu as pltpu

D_MODEL = 1024
GRID_W = 64
N_HEADS = 16
HEAD_DIM = 64
A_KV_HEADS = 4
A_GROUP = N_HEADS // A_KV_HEADS
ROPE_THETA = 10000.0
AXIS_DIM = HEAD_DIM // 2
ROPE_HALF = AXIS_DIM // 2
NA_WIN_H = 8
NA_WIN_W = 16
NA_BIAS_H = 2 * NA_WIN_H - 1
NA_BIAS_W = 2 * NA_WIN_W - 1
D_FF = 4 * D_MODEL
NORM_EPS = 1e-6
LOG2_E = 1.4426950408889634
NA_SCORE_SCALE = HEAD_DIM ** -0.5 * LOG2_E
HD = N_HEADS * HEAD_DIM
A_QKV = (N_HEADS + 2 * A_KV_HEADS) * HEAD_DIM

ROW_TILE = 512
FLASH_TQ = 256
FLASH_TK = 512
FLASH_HEADS = 4
FLASH_L_ROWS = 16
NA_Q_ROWS = 4
NA_K_ROWS = 12
NA_HEADS = 4
O_GROUPS = 4
O_GROUP_WIDTH = 256
NA_NQ = NA_Q_ROWS * GRID_W
NA_NK = NA_K_ROWS * GRID_W
FF_CHUNK = 1024
VMEM_LIMIT_BYTES = 56 * 1024 * 1024

BF16 = jnp.bfloat16
F32 = jnp.float32


def _params(n_axes):
    return pltpu.CompilerParams(
        dimension_semantics=("arbitrary",) * n_axes, vmem_limit_bytes=VMEM_LIMIT_BYTES)


def _resident(shape):
    zeros = (0,) * len(shape)
    return pl.BlockSpec(shape, lambda *_: zeros, pipeline_mode=pl.Buffered(1))


def _rms(x, gain):
    ms = jnp.mean(x * x, axis=-1, keepdims=True)
    return x * lax.rsqrt(ms + NORM_EPS) * gain


def _qkv_a_kernel(x_ref, g_ref, wt_ref, qg_ref, kg_ref, cos_ref, sin_ref,
                  q_ref, k_ref, v_ref, s_ref, *, tm):
    h = _rms(x_ref[0], g_ref[...]).astype(BF16)
    qkv_t = lax.dot_general(wt_ref[...], h, (((1,), (1,)), ((), ())),
                            preferred_element_type=F32)
    n_slots = N_HEADS + 2 * A_KV_HEADS
    s_ref[...] = qkv_t.reshape(n_slots, HEAD_DIM, tm)

    def norm_rope(lo, hi, gain_ref, store):
        t = s_ref[lo:hi]
        r = lax.rsqrt(jnp.sum(t * t, axis=1, keepdims=True) * (1.0 / HEAD_DIM) + NORM_EPS)
        for a in range(2):
            d1 = a * AXIS_DIM
            d2 = d1 + ROPE_HALF
            x1 = s_ref[lo:hi, d1:d1 + ROPE_HALF, :] * r * gain_ref[d1:d1 + ROPE_HALF, :]
            x2 = s_ref[lo:hi, d2:d2 + ROPE_HALF, :] * r * gain_ref[d2:d2 + ROPE_HALF, :]
            c = cos_ref[a * ROPE_HALF:(a + 1) * ROPE_HALF, :]
            s = sin_ref[a * ROPE_HALF:(a + 1) * ROPE_HALF, :]
            store(d1, x1 * c - x2 * s)
            store(d2, x2 * c + x1 * s)

    def store_q(d, val):
        q_ref[0, :, d:d + ROPE_HALF, :] = val.astype(BF16)

    def store_k(d, val):
        s_ref[N_HEADS:N_HEADS + A_KV_HEADS, d:d + ROPE_HALF, :] = val

    norm_rope(0, N_HEADS, qg_ref, store_q)
    norm_rope(N_HEADS, N_HEADS + A_KV_HEADS, kg_ref, store_k)
    k_nat = s_ref[N_HEADS:N_HEADS + A_KV_HEADS].reshape(A_KV_HEADS * HEAD_DIM, tm).T
    for g in range(A_KV_HEADS):
        k_ref[0, g] = k_nat[:, g * HEAD_DIM:(g + 1) * HEAD_DIM].astype(BF16)
        for c in range(tm // FLASH_TK):
            v_ref[0, g, c] = s_ref[N_HEADS + A_KV_HEADS + g, :,
                                   c * FLASH_TK:(c + 1) * FLASH_TK].astype(BF16)


def _qkv_a(x, gain, w_t, q_gain, k_gain, cos_t, sin_t):
    b, s, _ = x.shape
    tm = ROW_TILE
    n_slots = N_HEADS + 2 * A_KV_HEADS
    return pl.pallas_call(
        functools.partial(_qkv_a_kernel, tm=tm),
        grid=(b, s // tm),
        in_specs=[
            pl.BlockSpec((1, tm, D_MODEL), lambda bi, i: (bi, i, 0)),
            _resident((1, D_MODEL)),
            _resident((A_QKV, D_MODEL)),
            _resident((HEAD_DIM, tm)),
            _resident((HEAD_DIM, tm)),
            pl.BlockSpec((AXIS_DIM, tm), lambda bi, i: (0, i)),
            pl.BlockSpec((AXIS_DIM, tm), lambda bi, i: (0, i)),
        ],
        out_specs=[
            pl.BlockSpec((1, N_HEADS, HEAD_DIM, tm), lambda bi, i: (bi, 0, 0, i)),
            pl.BlockSpec((1, A_KV_HEADS, tm, HEAD_DIM), lambda bi, i: (bi, 0, i, 0)),
            pl.BlockSpec((1, A_KV_HEADS, tm // FLASH_TK, HEAD_DIM, FLASH_TK),
                         lambda bi, i: (bi, 0, i, 0, 0)),
        ],
        out_shape=[
            jax.ShapeDtypeStruct((b, N_HEADS, HEAD_DIM, s), BF16),
            jax.ShapeDtypeStruct((b, A_KV_HEADS, s, HEAD_DIM), BF16),
            jax.ShapeDtypeStruct((b, A_KV_HEADS, s // FLASH_TK, HEAD_DIM, FLASH_TK), BF16),
        ],
        scratch_shapes=[pltpu.VMEM((n_slots, HEAD_DIM, tm), F32)],
        compiler_params=_params(2),
        name="qkv_a",
    )(x, gain, w_t, q_gain, k_gain, cos_t, sin_t)


def _flash_kernel(q_ref, k_ref, v_ref, o_ref, s_ref, *, n_chunks, tq):
    def scores(c, slot, e):
        start = pl.multiple_of(c * FLASH_TK, FLASH_TK)
        k_c = k_ref[0, 0, pl.ds(start, FLASH_TK), :]
        s_ref[slot, e] = jnp.dot(k_c, q_ref[0, e], preferred_element_type=F32)

    ones = jnp.ones((FLASH_L_ROWS, FLASH_TK), BF16)

    def update(c, slot, e, stat):
        m, acc = stat
        s_t = s_ref[slot, e]
        m_new = jnp.maximum(m, jnp.max(s_t, axis=0, keepdims=True))
        alpha = jnp.exp2(m - m_new)
        p = jnp.exp2(s_t - m_new).astype(BF16)
        v_ext = jnp.concatenate([v_ref[0, 0, c], ones], axis=0)
        acc = alpha * acc + jnp.dot(v_ext, p, preferred_element_type=F32)
        return m_new, acc

    def step(c, slot, stats, issue_next):
        new = []
        for e in range(FLASH_HEADS):
            if issue_next:
                scores(c + 1, 1 - slot, e)
            new.append(update(c, slot, e, stats[e]))
        return tuple(new)

    def body(j, stats):
        stats = step(2 * j, 0, stats, True)
        return step(2 * j + 1, 1, stats, True)

    for e in range(FLASH_HEADS):
        scores(0, 0, e)
    stats = tuple((jnp.full((1, tq), -jnp.inf, F32),
                   jnp.zeros((HEAD_DIM + FLASH_L_ROWS, tq), F32)) for _ in range(FLASH_HEADS))
    stats = lax.fori_loop(0, n_chunks // 2 - 1, body, stats, unroll=True)
    stats = step(n_chunks - 2, 0, stats, True)
    stats = step(n_chunks - 1, 1, stats, False)
    o_t = jnp.concatenate([acc[:HEAD_DIM] / acc[HEAD_DIM:HEAD_DIM + 1] for (_, acc) in stats],
                          axis=0)
    o_ref[0, 0] = o_t.T.astype(BF16)


def _flash(q_t, k, v_t):
    b, _, _, s = q_t.shape
    tq = FLASH_TQ
    n_chunks = s // FLASH_TK
    pairs = N_HEADS // FLASH_HEADS
    per_kv = A_GROUP // FLASH_HEADS
    return pl.pallas_call(
        functools.partial(_flash_kernel, n_chunks=n_chunks, tq=tq),
        grid=(b, pairs, s // tq),
        in_specs=[
            pl.BlockSpec((1, FLASH_HEADS, HEAD_DIM, tq), lambda bi, hp, i: (bi, hp, 0, i)),
            pl.BlockSpec((1, 1, s, HEAD_DIM), lambda bi, hp, i: (bi, hp // per_kv, 0, 0)),
            pl.BlockSpec((1, 1, n_chunks, HEAD_DIM, FLASH_TK),
                         lambda bi, hp, i: (bi, hp // per_kv, 0, 0, 0)),
        ],
        out_specs=pl.BlockSpec((1, 1, tq, O_GROUP_WIDTH), lambda bi, hp, i: (bi, hp, i, 0)),
        out_shape=jax.ShapeDtypeStruct((b, O_GROUPS, s, O_GROUP_WIDTH), BF16),
        scratch_shapes=[pltpu.VMEM((2, FLASH_HEADS, FLASH_TK, tq), F32)],
        compiler_params=_params(3),
        name="flash_a",
    )(q_t, k, v_t)


def _proj_res_kernel(a_ref, w_ref, r_ref, o_ref):
    a = jnp.concatenate([a_ref[0, g] for g in range(O_GROUPS)], axis=1)
    o_ref[0] = r_ref[0] + jnp.dot(a, w_ref[...], preferred_element_type=F32)


def _proj_res(a, w, res):
    b, s, _ = res.shape
    tm = ROW_TILE
    return pl.pallas_call(
        _proj_res_kernel,
        grid=(b, s // tm),
        in_specs=[
            pl.BlockSpec((1, O_GROUPS, tm, O_GROUP_WIDTH), lambda bi, i: (bi, 0, i, 0)),
            _resident((HD, D_MODEL)),
            pl.BlockSpec((1, tm, D_MODEL), lambda bi, i: (bi, i, 0)),
        ],
        out_specs=pl.BlockSpec((1, tm, D_MODEL), lambda bi, i: (bi, i, 0)),
        out_shape=jax.ShapeDtypeStruct((b, s, D_MODEL), F32),
        compiler_params=_params(2),
        name="proj_res",
    )(a, w, res)


def _mlp_kernel(x_ref, g_ref, win_ref, wout_ref, gf_ref, o_ref, *, final):
    x = x_ref[0]
    h = _rms(x, g_ref[...]).astype(BF16)
    acc = x
    for f in range(D_FF // FF_CHUNK):
        u = jnp.dot(h, win_ref[:, f * FF_CHUNK:(f + 1) * FF_CHUNK], preferred_element_type=F32)
        u = jnp.maximum(u, 0.0)
        acc = acc + jnp.dot((u * u).astype(BF16), wout_ref[f * FF_CHUNK:(f + 1) * FF_CHUNK, :],
                            preferred_element_type=F32)
    if final:
        acc = _rms(acc, gf_ref[...])
    o_ref[0] = acc


def _mlp(x, gain, w_in, w_out, gain_final, final):
    b, s, _ = x.shape
    tm = ROW_TILE
    return pl.pallas_call(
        functools.partial(_mlp_kernel, final=final),
        grid=(b, s // tm),
        in_specs=[
            pl.BlockSpec((1, tm, D_MODEL), lambda bi, i: (bi, i, 0)),
            _resident((1, D_MODEL)),
            _resident((D_MODEL, D_FF)),
            _resident((D_FF, D_MODEL)),
            _resident((1, D_MODEL)),
        ],
        out_specs=pl.BlockSpec((1, tm, D_MODEL), lambda bi, i: (bi, i, 0)),
        out_shape=jax.ShapeDtypeStruct((b, s, D_MODEL), F32),
        compiler_params=_params(2),
        name="mlp_final" if final else "mlp",
    )(x, gain, w_in, w_out, gain_final)


def _qkv_b_kernel(x_ref, g_ref, wqv_t_ref, wk_ref, q_ref, k_ref, v_ref, *, tm):
    h = _rms(x_ref[0], g_ref[...]).astype(BF16)
    k = jnp.dot(h, wk_ref[...], preferred_element_type=F32).astype(BF16)
    width = NA_HEADS * HEAD_DIM
    for g in range(N_HEADS // NA_HEADS):
        k_ref[0, g] = k[:, g * width:(g + 1) * width]
    nt = (((1,), (1,)), ((), ()))
    q_t = lax.dot_general(wqv_t_ref[0:HD, :], h, nt, preferred_element_type=F32)
    q_ref[0, 0] = (q_t * NA_SCORE_SCALE).astype(BF16)
    v_t = lax.dot_general(wqv_t_ref[HD:2 * HD, :], h, nt, preferred_element_type=F32)
    for hd in range(N_HEADS):
        for c in range(tm // NA_NQ):
            v_ref[0, hd, c] = v_t[hd * HEAD_DIM:(hd + 1) * HEAD_DIM,
                                  c * NA_NQ:(c + 1) * NA_NQ].astype(BF16)


def _qkv_b(x, gain, wqv_t, wk):
    b, s, _ = x.shape
    tm = 2 * NA_NQ
    groups = N_HEADS // NA_HEADS
    width = NA_HEADS * HEAD_DIM
    return pl.pallas_call(
        functools.partial(_qkv_b_kernel, tm=tm),
        grid=(b, s // tm),
        in_specs=[
            pl.BlockSpec((1, tm, D_MODEL), lambda bi, i: (bi, i, 0)),
            _resident((1, D_MODEL)),
            _resident((2 * HD, D_MODEL)),
            _resident((D_MODEL, HD)),
        ],
        out_specs=[
            pl.BlockSpec((1, 1, HD, tm), lambda bi, i: (bi, i, 0, 0)),
            pl.BlockSpec((1, groups, tm, width), lambda bi, i: (bi, 0, i, 0)),
            pl.BlockSpec((1, N_HEADS, tm // NA_NQ, HEAD_DIM, NA_NQ), lambda bi, i: (bi, 0, i, 0, 0)),
        ],
        out_shape=[
            jax.ShapeDtypeStruct((b, s // tm, HD, tm), BF16),
            jax.ShapeDtypeStruct((b, groups, s, width), BF16),
            jax.ShapeDtypeStruct((b, N_HEADS, s // NA_NQ, HEAD_DIM, NA_NQ), BF16),
        ],
        compiler_params=_params(2),
        name="qkv_b",
    )(x, gain, wqv_t, wk)


def _natten_kernel(q_ref, qn_ref, k_ref, v_ref, b_ref, o_ref, s_ref, *, rows):
    u = pl.program_id(2)
    n_blocks = rows // NA_Q_ROWS
    width = NA_HEADS * HEAD_DIM
    row_id = lax.broadcasted_iota(jnp.int32, (width, NA_NQ), 0)
    ones = jnp.ones((FLASH_L_ROWS, NA_NK), BF16)

    def key_base(blk):
        return jnp.clip(NA_Q_ROWS * blk - NA_WIN_H // 2, 0, rows - NA_K_ROWS)

    def scores(blk, q_blk, slot, e):
        variant = jnp.where(blk == 0, 0, jnp.where(blk == n_blocks - 1, 2, 1))
        start = pl.multiple_of(key_base(blk) * GRID_W, NA_NQ)
        k_w = k_ref[0, 0, pl.ds(start, NA_NK), :]
        mine = (row_id >= e * HEAD_DIM) & (row_id < (e + 1) * HEAD_DIM)
        q_e = jnp.where(mine, q_blk, jnp.zeros_like(q_blk))
        s_ref[slot, e] = jnp.dot(k_w, q_e, preferred_element_type=F32) + b_ref[variant, e]

    def attend(blk, slot, e):
        chunk0 = key_base(blk) // NA_Q_ROWS
        s_t = s_ref[slot, e]
        m = jnp.max(s_t, axis=0, keepdims=True)
        p = jnp.exp2(s_t - m).astype(BF16)
        v_t = [v_ref[0, e, chunk0 + j] for j in range(NA_K_ROWS // NA_Q_ROWS)]
        v_ext = jnp.concatenate([jnp.concatenate(v_t, axis=1), ones], axis=0)
        acc = jnp.dot(v_ext, p, preferred_element_type=F32)
        return acc[:HEAD_DIM] / acc[HEAD_DIM:HEAD_DIM + 1]

    @pl.when(u == 0)
    def _():
        for e in range(NA_HEADS):
            scores(0, q_ref[0, 0, :, 0:NA_NQ], 0, e)

    def half(blk, slot, blk_next, q_next, row0):
        outs = []
        for e in range(NA_HEADS):
            scores(blk_next, q_next, 1 - slot, e)
            outs.append(attend(blk, slot, e))
        o_ref[0, 0, row0:row0 + NA_NQ, :] = jnp.concatenate(outs, axis=0).T.astype(BF16)

    half(2 * u, 0, 2 * u + 1, q_ref[0, 0, :, NA_NQ:2 * NA_NQ], 0)
    u_next = jnp.minimum(u + 1, n_blocks // 2 - 1)
    half(2 * u + 1, 1, 2 * u_next, qn_ref[0, 0, :, 0:NA_NQ], NA_NQ)


def _natten(q_t, k, v_t, bias):
    b, _, s, _ = k.shape
    rows = s // GRID_W
    n_blocks = rows // NA_Q_ROWS
    groups = N_HEADS // NA_HEADS
    width = NA_HEADS * HEAD_DIM
    return pl.pallas_call(
        functools.partial(_natten_kernel, rows=rows),
        grid=(groups, b, n_blocks // 2),
        in_specs=[
            pl.BlockSpec((1, 1, width, 2 * NA_NQ), lambda hg, bi, u: (bi, u, hg, 0)),
            pl.BlockSpec((1, 1, width, 2 * NA_NQ),
                         lambda hg, bi, u: (bi, jnp.minimum(u + 1, n_blocks // 2 - 1), hg, 0)),
            pl.BlockSpec((1, 1, s, width), lambda hg, bi, u: (bi, hg, 0, 0)),
            pl.BlockSpec((1, NA_HEADS, s // NA_NQ, HEAD_DIM, NA_NQ),
                         lambda hg, bi, u: (bi, hg, 0, 0, 0)),
            pl.BlockSpec((3, NA_HEADS, NA_NK, NA_NQ), lambda hg, bi, u: (0, hg, 0, 0),
                         pipeline_mode=pl.Buffered(1)),
        ],
        out_specs=pl.BlockSpec((1, 1, 2 * NA_NQ, width), lambda hg, bi, u: (bi, hg, u, 0)),
        out_shape=jax.ShapeDtypeStruct((b, groups, s, width), BF16),
        scratch_shapes=[pltpu.VMEM((2, NA_HEADS, NA_NK, NA_NQ), F32)],
        compiler_params=_params(3),
        name="natten_b",
    )(q_t, q_t, k, v_t, bias)


def _rope_tables_t(seq):
    t = jnp.arange(seq)
    row = (t // GRID_W).astype(F32)
    col = (t % GRID_W).astype(F32)
    inv = ROPE_THETA ** (-jnp.arange(0, AXIS_DIM, 2, dtype=F32) / AXIS_DIM)
    ang = jnp.concatenate([inv[:, None] * row[None, :], inv[:, None] * col[None, :]], axis=0)
    return jnp.cos(ang), jnp.sin(ang)


def _natten_bias(rel_bias, rows):
    n_blocks = rows // NA_Q_ROWS
    t = jnp.array([0, 1, n_blocks - 1])[:, None, None]
    a = jnp.arange(NA_Q_ROWS)[None, :, None]
    i = jnp.arange(NA_K_ROWS)[None, None, :]
    r = NA_Q_ROWS * t + a
    r0 = jnp.clip(r - NA_WIN_H // 2, 0, rows - NA_WIN_H)
    kr = jnp.clip(NA_Q_ROWS * t - NA_WIN_H // 2, 0, rows - NA_K_ROWS) + i
    valid_r = (kr >= r0) & (kr < r0 + NA_WIN_H)
    c = jnp.arange(GRID_W)[:, None]
    kc = jnp.arange(GRID_W)[None, :]
    c0 = jnp.clip(c - NA_WIN_W // 2, 0, GRID_W - NA_WIN_W)
    valid_c = (kc >= c0) & (kc < c0 + NA_WIN_W)
    pick_r = ((kr - r + NA_WIN_H - 1)[..., None] == jnp.arange(NA_BIAS_H)).astype(F32)
    pick_c = ((kc - c + NA_WIN_W - 1)[..., None] == jnp.arange(NA_BIAS_W)).astype(F32)
    vals = jnp.einsum("vaid,hde,cke->vhikac", pick_r, rel_bias.astype(F32), pick_c,
                      precision=lax.Precision.HIGHEST)
    valid = (valid_r.transpose(0, 2, 1)[:, None, :, None, :, None]
             & valid_c.T[None, None, None, :, None, :])
    vals = jnp.where(valid, vals * LOG2_E, -jnp.inf)
    return vals.reshape(3, N_HEADS, NA_NK, NA_NQ)


def _trunk(x, p):
    _, s, _ = x.shape
    depth = p["norm_mix"].shape[0]
    for i in range(depth):
        j = i // 2
        g_mix = p["norm_mix"][i][None, :]
        if i % 2 == 0:
            q_t, k, v_t = _qkv_a(x, g_mix, p["a_w_qkv_t"][j], p["a_q_gain"][j], p["a_k_gain"][j],
                                 p["cos_t"], p["sin_t"])
            o = _flash(q_t, k, v_t)
            x = _proj_res(o, p["a_w_o"][j], x)
        else:
            q_t, k, v_t = _qkv_b(x, g_mix, p["b_w_qv_t"][j], p["b_w_k"][j])
            o = _natten(q_t, k, v_t, p["b_bias"][j])
            x = _proj_res(o, p["b_w_o"][j], x)
        x = _mlp(x, p["norm_mlp"][i][None, :], p["mlp_w_in"][i], p["mlp_w_out"][i],
                 p["norm_final"][None, :], final=(i == depth - 1))
    return x


def kernel(x_prompt, x_sample, norm_mix, norm_mlp, norm_final, a_w_qkv, a_q_norm, a_k_norm, a_w_o,
           b_w_qkv, b_rel_bias, b_w_o, mlp_w_in, mlp_w_out):
    scale = HEAD_DIM ** -0.5 * LOG2_E
    shared = {
        "norm_mix": norm_mix, "norm_mlp": norm_mlp, "norm_final": norm_final,
        "a_w_qkv_t": jnp.swapaxes(a_w_qkv, 1, 2).astype(BF16),
        "a_q_gain": jnp.broadcast_to((a_q_norm * scale)[:, :, None],
                                     a_q_norm.shape + (ROW_TILE,)),
        "a_k_gain": jnp.broadcast_to(a_k_norm[:, :, None], a_k_norm.shape + (ROW_TILE,)),
        "a_w_o": a_w_o.astype(BF16),
        "b_w_qv_t": jnp.swapaxes(jnp.concatenate([b_w_qkv[:, :, :HD], b_w_qkv[:, :, 2 * HD:]],
                                                 axis=2), 1, 2).astype(BF16),
        "b_w_k": b_w_qkv[:, :, HD:2 * HD].astype(BF16),
        "b_w_o": b_w_o.astype(BF16),
        "mlp_w_in": mlp_w_in.astype(BF16),
        "mlp_w_out": mlp_w_out.astype(BF16),
    }
    outs = []
    for x in (x_prompt, x_sample):
        s = x.shape[1]
        cos_t, sin_t = _rope_tables_t(s)
        p = dict(shared, cos_t=cos_t, sin_t=sin_t,
                 b_bias=jnp.stack([_natten_bias(rb, s // GRID_W) for rb in b_rel_bias]))
        outs.append(_trunk(x, p))
    return tuple(outs)
```

```python
import functools

import jax
import jax.numpy as jnp
from jax import lax
from jax.experimental import pallas as pl
from jax.experimental.pallas import tpu as pltpu

D_MODEL = 1024
GRID_W = 64
N_HEADS = 16
HEAD_DIM = 64
A_KV_HEADS = 4
A_GROUP = N_HEADS // A_KV_HEADS
ROPE_THETA = 10000.0
AXIS_DIM = HEAD_DIM // 2
ROPE_HALF = AXIS_DIM // 2
NA_WIN_H = 8
NA_WIN_W = 16
NA_BIAS_H = 2 * NA_WIN_H - 1
NA_BIAS_W = 2 * NA_WIN_W - 1
D_FF = 4 * D_MODEL
NORM_EPS = 1e-6
LOG2_E = 1.4426950408889634
NA_SCORE_SCALE = HEAD_DIM ** -0.5 * LOG2_E
HD = N_HEADS * HEAD_DIM
A_QKV = (N_HEADS + 2 * A_KV_HEADS) * HEAD_DIM

ROW_TILE = 512
FLASH_TQ = 256
FLASH_TK = 512
FLASH_HEADS = 4
FLASH_L_ROWS = 16
NA_Q_ROWS = 4
NA_K_ROWS = 12
NA_HEADS = 4
O_GROUPS = 4
O_GROUP_WIDTH = 256
NA_NQ = NA_Q_ROWS * GRID_W
NA_NK = NA_K_ROWS * GRID_W
FF_CHUNK = 1024
VMEM_LIMIT_BYTES = 56 * 1024 * 1024

BF16 = jnp.bfloat16
F32 = jnp.float32


def _params(n_axes):
    return pltpu.CompilerParams(
        dimension_semantics=("arbitrary",) * n_axes, vmem_limit_bytes=VMEM_LIMIT_BYTES)


def _resident(shape):
    zeros = (0,) * len(shape)
    return pl.BlockSpec(shape, lambda *_: zeros, pipeline_mode=pl.Buffered(1))


def _rms(x, gain):
    ms = jnp.mean(x * x, axis=-1, keepdims=True)
    return x * lax.rsqrt(ms + NORM_EPS) * gain


def _qkv_a_kernel(x_ref, g_ref, wt_ref, qg_ref, kg_ref, cos_ref, sin_ref,
                  q_ref, k_ref, v_ref, s_ref, *, tm):
    h = _rms(x_ref[0], g_ref[...]).astype(BF16)
    qkv_t = lax.dot_general(wt_ref[...], h, (((1,), (1,)), ((), ())),
                            preferred_element_type=F32)
    n_slots = N_HEADS + 2 * A_KV_HEADS
    s_ref[...] = qkv_t.reshape(n_slots, HEAD_DIM, tm)

    def norm_rope(lo, hi, gain_ref, store):
        t = s_ref[lo:hi]
        r = lax.rsqrt(jnp.sum(t * t, axis=1, keepdims=True) * (1.0 / HEAD_DIM) + NORM_EPS)
        for a in range(2):
            d1 = a * AXIS_DIM
            d2 = d1 + ROPE_HALF
            x1 = s_ref[lo:hi, d1:d1 + ROPE_HALF, :] * r * gain_ref[d1:d1 + ROPE_HALF, :]
            x2 = s_ref[lo:hi, d2:d2 + ROPE_HALF, :] * r * gain_ref[d2:d2 + ROPE_HALF, :]
            c = cos_ref[a * ROPE_HALF:(a + 1) * ROPE_HALF, :]
            s = sin_ref[a * ROPE_HALF:(a + 1) * ROPE_HALF, :]
            store(d1, x1 * c - x2 * s)
            store(d2, x2 * c + x1 * s)

    def store_q(d, val):
        q_ref[0, :, d:d + ROPE_HALF, :] = val.astype(BF16)

    def store_k(d, val):
        s_ref[N_HEADS:N_HEADS + A_KV_HEADS, d:d + ROPE_HALF, :] = val

    norm_rope(0, N_HEADS, qg_ref, store_q)
    norm_rope(N_HEADS, N_HEADS + A_KV_HEADS, kg_ref, store_k)
    k_nat = s_ref[N_HEADS:N_HEADS + A_KV_HEADS].reshape(A_KV_HEADS * HEAD_DIM, tm).T
    for g in range(A_KV_HEADS):
        k_ref[0, g] = k_nat[:, g * HEAD_DIM:(g + 1) * HEAD_DIM].astype(BF16)
        for c in range(tm // FLASH_TK):
            v_ref[0, g, c] = s_ref[N_HEADS + A_KV_HEADS + g, :,
                                   c * FLASH_TK:(c + 1) * FLASH_TK].astype(BF16)


def _qkv_a(x, gain, w_t, q_gain, k_gain, cos_t, sin_t):
    b, s, _ = x.shape
    tm = ROW_TILE
    n_slots = N_HEADS + 2 * A_KV_HEADS
    return pl.pallas_call(
        functools.partial(_qkv_a_kernel, tm=tm),
        grid=(b, s // tm),
        in_specs=[
            pl.BlockSpec((1, tm, D_MODEL), lambda bi, i: (bi, i, 0)),
            _resident((1, D_MODEL)),
            _resident((A_QKV, D_MODEL)),
            _resident((HEAD_DIM, tm)),
            _resident((HEAD_DIM, tm)),
            pl.BlockSpec((AXIS_DIM, tm), lambda bi, i: (0, i)),
            pl.BlockSpec((AXIS_DIM, tm), lambda bi, i: (0, i)),
        ],
        out_specs=[
            pl.BlockSpec((1, N_HEADS, HEAD_DIM, tm), lambda bi, i: (bi, 0, 0, i)),
            pl.BlockSpec((1, A_KV_HEADS, tm, HEAD_DIM), lambda bi, i: (bi, 0, i, 0)),
            pl.BlockSpec((1, A_KV_HEADS, tm // FLASH_TK, HEAD_DIM, FLASH_TK),
                         lambda bi, i: (bi, 0, i, 0, 0)),
        ],
        out_shape=[
            jax.ShapeDtypeStruct((b, N_HEADS, HEAD_DIM, s), BF16),
            jax.ShapeDtypeStruct((b, A_KV_HEADS, s, HEAD_DIM), BF16),
            jax.ShapeDtypeStruct((b, A_KV_HEADS, s // FLASH_TK, HEAD_DIM, FLASH_TK), BF16),
        ],
        scratch_shapes=[pltpu.VMEM((n_slots, HEAD_DIM, tm), F32)],
        compiler_params=_params(2),
        name="qkv_a",
    )(x, gain, w_t, q_gain, k_gain, cos_t, sin_t)


def _flash_kernel(q_ref, k_ref, v_ref, qn_ref, kn_ref, o_ref, s_ref, *, n_chunks, tq):
    first = (pl.program_id(0) == 0) & (pl.program_id(1) == 0) & (pl.program_id(2) == 0)

    def scores(c, slot, e, q_src=q_ref, k_src=k_ref):
        k_c = k_src[0, 0, c * FLASH_TK:(c + 1) * FLASH_TK, :]
        s_ref[slot, e] = jnp.dot(k_c, q_src[0, e], preferred_element_type=F32)

    ones = jnp.ones((FLASH_L_ROWS, FLASH_TK), BF16)

    def update(c, slot, e, stat):
        m, acc = stat
        s_t = s_ref[slot, e]
        m_new = jnp.maximum(m, jnp.max(s_t, axis=0, keepdims=True))
        alpha = jnp.exp2(m - m_new)
        p = jnp.exp2(s_t - m_new).astype(BF16)
        v_ext = jnp.concatenate([v_ref[0, 0, c], ones], axis=0)
        acc = alpha * acc + jnp.dot(v_ext, p, preferred_element_type=F32)
        return m_new, acc

    @pl.when(first)
    def _():
        for e in range(FLASH_HEADS):
            scores(0, 0, e)

    stats = tuple((jnp.full((1, tq), -jnp.inf, F32),
                   jnp.zeros((HEAD_DIM + FLASH_L_ROWS, tq), F32)) for _ in range(FLASH_HEADS))
    for c in range(n_chunks):
        new = []
        for e in range(FLASH_HEADS):
            if c + 1 < n_chunks:
                scores(c + 1, (c + 1) % 2, e)
            else:
                scores(0, 0, e, qn_ref, kn_ref)
            new.append(update(c, c % 2, e, stats[e]))
        stats = tuple(new)
    o_t = jnp.concatenate([acc[:HEAD_DIM] / acc[HEAD_DIM:HEAD_DIM + 1] for (_, acc) in stats],
                          axis=0)
    o_ref[0, 0] = o_t.T.astype(BF16)


def _flash(q_t, k, v_t):
    b, _, _, s = q_t.shape
    tq = FLASH_TQ
    n_chunks = s // FLASH_TK
    pairs = N_HEADS // FLASH_HEADS
    per_kv = A_GROUP // FLASH_HEADS
    n_q = s // tq
    assert n_chunks % 2 == 0

    def following(bi, hp, i):
        flat = jnp.minimum((bi * pairs + hp) * n_q + i + 1, b * pairs * n_q - 1)
        return flat // (pairs * n_q), (flat // n_q) % pairs, flat % n_q

    def q_next(bi, hp, i):
        nb, nh, ni = following(bi, hp, i)
        return nb, nh, 0, ni

    def k_next(bi, hp, i):
        nb, nh, _ = following(bi, hp, i)
        return nb, nh // per_kv, 0, 0

    return pl.pallas_call(
        functools.partial(_flash_kernel, n_chunks=n_chunks, tq=tq),
        grid=(b, pairs, n_q),
        in_specs=[
            pl.BlockSpec((1, FLASH_HEADS, HEAD_DIM, tq), lambda bi, hp, i: (bi, hp, 0, i)),
            pl.BlockSpec((1, 1, s, HEAD_DIM), lambda bi, hp, i: (bi, hp // per_kv, 0, 0)),
            pl.BlockSpec((1, 1, n_chunks, HEAD_DIM, FLASH_TK),
                         lambda bi, hp, i: (bi, hp // per_kv, 0, 0, 0)),
            pl.BlockSpec((1, FLASH_HEADS, HEAD_DIM, tq), q_next),
            pl.BlockSpec((1, 1, s, HEAD_DIM), k_next),
        ],
        out_specs=pl.BlockSpec((1, 1, tq, O_GROUP_WIDTH), lambda bi, hp, i: (bi, hp, i, 0)),
        out_shape=jax.ShapeDtypeStruct((b, O_GROUPS, s, O_GROUP_WIDTH), BF16),
        scratch_shapes=[pltpu.VMEM((2, FLASH_HEADS, FLASH_TK, tq), F32)],
        compiler_params=_params(3),
        name="flash_a",
    )(q_t, k, v_t, q_t, k)


def _mlp_kernel(x_ref, a_ref, wo_ref, g_ref, win_ref, wout_ref, gf_ref, o_ref, *, final):
    a = jnp.concatenate([a_ref[0, g] for g in range(O_GROUPS)], axis=1)
    x = x_ref[0] + jnp.dot(a, wo_ref[...], preferred_element_type=F32)
    h = _rms(x, g_ref[...]).astype(BF16)
    acc = x
    for f in range(D_FF // FF_CHUNK):
        u = jnp.dot(h, win_ref[:, f * FF_CHUNK:(f + 1) * FF_CHUNK], preferred_element_type=F32)
        u = jnp.maximum(u, 0.0)
        acc = acc + jnp.dot((u * u).astype(BF16), wout_ref[f * FF_CHUNK:(f + 1) * FF_CHUNK, :],
                            preferred_element_type=F32)
    if final:
        acc = _rms(acc, gf_ref[...])
    o_ref[0] = acc


def _mlp(x, attn, w_o, gain, w_in, w_out, gain_final, final):
    b, s, _ = x.shape
    tm = ROW_TILE
    return pl.pallas_call(
        functools.partial(_mlp_kernel, final=final),
        grid=(b, s // tm),
        in_specs=[
            pl.BlockSpec((1, tm, D_MODEL), lambda bi, i: (bi, i, 0)),
            pl.BlockSpec((1, O_GROUPS, tm, O_GROUP_WIDTH), lambda bi, i: (bi, 0, i, 0)),
            _resident((HD, D_MODEL)),
            _resident((1, D_MODEL)),
            _resident((D_MODEL, D_FF)),
            _resident((D_FF, D_MODEL)),
            _resident((1, D_MODEL)),
        ],
        out_specs=pl.BlockSpec((1, tm, D_MODEL), lambda bi, i: (bi, i, 0)),
        out_shape=jax.ShapeDtypeStruct((b, s, D_MODEL), F32),
        compiler_params=_params(2),
        name="mlp_final" if final else "mlp",
    )(x, attn, w_o, gain, w_in, w_out, gain_final)


def _qkv_b_kernel(x_ref, g_ref, wqv_t_ref, wk_ref, q_ref, k_ref, v_ref, *, tm):
    h = _rms(x_ref[0], g_ref[...]).astype(BF16)
    k = jnp.dot(h, wk_ref[...], preferred_element_type=F32).astype(BF16)
    width = NA_HEADS * HEAD_DIM
    for g in range(N_HEADS // NA_HEADS):
        k_ref[0, g] = k[:, g * width:(g + 1) * width]
    nt = (((1,), (1,)), ((), ()))
    q_t = lax.dot_general(wqv_t_ref[0:HD, :], h, nt, preferred_element_type=F32)
    q_ref[0, 0] = (q_t * NA_SCORE_SCALE).astype(BF16)
    v_t = lax.dot_general(wqv_t_ref[HD:2 * HD, :], h, nt, preferred_element_type=F32)
    for hd in range(N_HEADS):
        for c in range(tm // NA_NQ):
            v_ref[0, hd, c] = v_t[hd * HEAD_DIM:(hd + 1) * HEAD_DIM,
                                  c * NA_NQ:(c + 1) * NA_NQ].astype(BF16)


def _qkv_b(x, gain, wqv_t, wk):
    b, s, _ = x.shape
    tm = 2 * NA_NQ
    groups = N_HEADS // NA_HEADS
    width = NA_HEADS * HEAD_DIM
    return pl.pallas_call(
        functools.partial(_qkv_b_kernel, tm=tm),
        grid=(b, s // tm),
        in_specs=[
            pl.BlockSpec((1, tm, D_MODEL), lambda bi, i: (bi, i, 0)),
            _resident((1, D_MODEL)),
            _resident((2 * HD, D_MODEL)),
            _resident((D_MODEL, HD)),
        ],
        out_specs=[
            pl.BlockSpec((1, 1, HD, tm), lambda bi, i: (bi, i, 0, 0)),
            pl.BlockSpec((1, groups, tm, width), lambda bi, i: (bi, 0, i, 0)),
            pl.BlockSpec((1, N_HEADS, tm // NA_NQ, HEAD_DIM, NA_NQ), lambda bi, i: (bi, 0, i, 0, 0)),
        ],
        out_shape=[
            jax.ShapeDtypeStruct((b, s // tm, HD, tm), BF16),
            jax.ShapeDtypeStruct((b, groups, s, width), BF16),
            jax.ShapeDtypeStruct((b, N_HEADS, s // NA_NQ, HEAD_DIM, NA_NQ), BF16),
        ],
        compiler_params=_params(2),
        name="qkv_b",
    )(x, gain, wqv_t, wk)


def _natten_kernel(q_ref, qn_ref, k_ref, v_ref, b_ref, o_ref, s_ref, *, rows):
    u = pl.program_id(2)
    n_blocks = rows // NA_Q_ROWS
    width = NA_HEADS * HEAD_DIM
    row_id = lax.broadcasted_iota(jnp.int32, (width, NA_NQ), 0)
    ones = jnp.ones((FLASH_L_ROWS, NA_NK), BF16)

    def key_base(blk):
        return jnp.clip(NA_Q_ROWS * blk - NA_WIN_H // 2, 0, rows - NA_K_ROWS)

    def scores(blk, q_blk, slot, e):
        variant = jnp.where(blk == 0, 0, jnp.where(blk == n_blocks - 1, 2, 1))
        start = pl.multiple_of(key_base(blk) * GRID_W, NA_NQ)
        k_w = k_ref[0, 0, pl.ds(start, NA_NK), :]
        mine = (row_id >= e * HEAD_DIM) & (row_id < (e + 1) * HEAD_DIM)
        q_e = jnp.where(mine, q_blk, jnp.zeros_like(q_blk))
        s_ref[slot, e] = jnp.dot(k_w, q_e, preferred_element_type=F32) + b_ref[variant, e]

    def attend(blk, slot, e):
        chunk0 = key_base(blk) // NA_Q_ROWS
        s_t = s_ref[slot, e]
        m = jnp.max(s_t, axis=0, keepdims=True)
        p = jnp.exp2(s_t - m).astype(BF16)
        v_t = [v_ref[0, e, chunk0 + j] for j in range(NA_K_ROWS // NA_Q_ROWS)]
        v_ext = jnp.concatenate([jnp.concatenate(v_t, axis=1), ones], axis=0)
        acc = jnp.dot(v_ext, p, preferred_element_type=F32)
        return acc[:HEAD_DIM] / acc[HEAD_DIM:HEAD_DIM + 1]

    @pl.when(u == 0)
    def _():
        for e in range(NA_HEADS):
            scores(0, q_ref[0, 0, :, 0:NA_NQ], 0, e)

    def half(blk, slot, blk_next, q_next, row0):
        outs = []
        for e in range(NA_HEADS):
            scores(blk_next, q_next, 1 - slot, e)
            outs.append(attend(blk, slot, e))
        o_ref[0, 0, row0:row0 + NA_NQ, :] = jnp.concatenate(outs, axis=0).T.astype(BF16)

    half(2 * u, 0, 2 * u + 1, q_ref[0, 0, :, NA_NQ:2 * NA_NQ], 0)
    u_next = jnp.minimum(u + 1, n_blocks // 2 - 1)
    half(2 * u + 1, 1, 2 * u_next, qn_ref[0, 0, :, 0:NA_NQ], NA_NQ)


def _natten(q_t, k, v_t, bias):
    b, _, s, _ = k.shape
    rows = s // GRID_W
    n_blocks = rows // NA_Q_ROWS
    groups = N_HEADS // NA_HEADS
    width = NA_HEADS * HEAD_DIM
    return pl.pallas_call(
        functools.partial(_natten_kernel, rows=rows),
        grid=(groups, b, n_blocks // 2),
        in_specs=[
            pl.BlockSpec((1, 1, width, 2 * NA_NQ), lambda hg, bi, u: (bi, u, hg, 0)),
            pl.BlockSpec((1, 1, width, 2 * NA_NQ),
                         lambda hg, bi, u: (bi, jnp.minimum(u + 1, n_blocks // 2 - 1), hg, 0)),
            pl.BlockSpec((1, 1, s, width), lambda hg, bi, u: (bi, hg, 0, 0)),
            pl.BlockSpec((1, NA_HEADS, s // NA_NQ, HEAD_DIM, NA_NQ),
                         lambda hg, bi, u: (bi, hg, 0, 0, 0)),
            pl.BlockSpec((3, NA_HEADS, NA_NK, NA_NQ), lambda hg, bi, u: (0, hg, 0, 0),
                         pipeline_mode=pl.Buffered(1)),
        ],
        out_specs=pl.BlockSpec((1, 1, 2 * NA_NQ, width), lambda hg, bi, u: (bi, hg, u, 0)),
        out_shape=jax.ShapeDtypeStruct((b, groups, s, width), BF16),
        scratch_shapes=[pltpu.VMEM((2, NA_HEADS, NA_NK, NA_NQ), F32)],
        compiler_params=_params(3),
        name="natten_b",
    )(q_t, q_t, k, v_t, bias)


def _rope_tables_t(seq):
    t = jnp.arange(seq)
    row = (t // GRID_W).astype(F32)
    col = (t % GRID_W).astype(F32)
    inv = ROPE_THETA ** (-jnp.arange(0, AXIS_DIM, 2, dtype=F32) / AXIS_DIM)
    ang = jnp.concatenate([inv[:, None] * row[None, :], inv[:, None] * col[None, :]], axis=0)
    return jnp.cos(ang), jnp.sin(ang)


def _natten_bias(rel_bias, rows):
    n_blocks = rows // NA_Q_ROWS
    t = jnp.array([0, 1, n_blocks - 1])[:, None, None]
    a = jnp.arange(NA_Q_ROWS)[None, :, None]
    i = jnp.arange(NA_K_ROWS)[None, None, :]
    r = NA_Q_ROWS * t + a
    r0 = jnp.clip(r - NA_WIN_H // 2, 0, rows - NA_WIN_H)
    kr = jnp.clip(NA_Q_ROWS * t - NA_WIN_H // 2, 0, rows - NA_K_ROWS) + i
    valid_r = (kr >= r0) & (kr < r0 + NA_WIN_H)
    c = jnp.arange(GRID_W)[:, None]
    kc = jnp.arange(GRID_W)[None, :]
    c0 = jnp.clip(c - NA_WIN_W // 2, 0, GRID_W - NA_WIN_W)
    valid_c = (kc >= c0) & (kc < c0 + NA_WIN_W)
    pick_r = ((kr - r + NA_WIN_H - 1)[..., None] == jnp.arange(NA_BIAS_H)).astype(F32)
    pick_c = ((kc - c + NA_WIN_W - 1)[..., None] == jnp.arange(NA_BIAS_W)).astype(F32)
    vals = jnp.einsum("vaid,hde,cke->vhikac", pick_r, rel_bias.astype(F32), pick_c,
                      precision=lax.Precision.HIGHEST)
    valid = (valid_r.transpose(0, 2, 1)[:, None, :, None, :, None]
             & valid_c.T[None, None, None, :, None, :])
    vals = jnp.where(valid, vals * LOG2_E, -jnp.inf)
    return vals.reshape(3, N_HEADS, NA_NK, NA_NQ)


def _trunk(x, p):
    _, s, _ = x.shape
    depth = p["norm_mix"].shape[0]
    for i in range(depth):
        j = i // 2
        g_mix = p["norm_mix"][i][None, :]
        if i % 2 == 0:
            q_t, k, v_t = _qkv_a(x, g_mix, p["a_w_qkv_t"][j], p["a_q_gain"][j], p["a_k_gain"][j],
                                 p["cos_t"], p["sin_t"])
            o = _flash(q_t, k, v_t)
            w_o = p["a_w_o"][j]
        else:
            q_t, k, v_t = _qkv_b(x, g_mix, p["b_w_qv_t"][j], p["b_w_k"][j])
            o = _natten(q_t, k, v_t, p["b_bias"][j])
            w_o = p["b_w_o"][j]
        x = _mlp(x, o, w_o, p["norm_mlp"][i][None, :], p["mlp_w_in"][i], p["mlp_w_out"][i],
                 p["norm_final"][None, :], final=(i == depth - 1))
    return x


def kernel(x_prompt, x_sample, norm_mix, norm_mlp, norm_final, a_w_qkv, a_q_norm, a_k_norm, a_w_o,
           b_w_qkv, b_rel_bias, b_w_o, mlp_w_in, mlp_w_out):
    scale = HEAD_DIM ** -0.5 * LOG2_E
    shared = {
        "norm_mix": norm_mix, "norm_mlp": norm_mlp, "norm_final": norm_final,
        "a_w_qkv_t": jnp.swapaxes(a_w_qkv, 1, 2).astype(BF16),
        "a_q_gain": jnp.broadcast_to((a_q_norm * scale)[:, :, None],
                                     a_q_norm.shape + (ROW_TILE,)),
        "a_k_gain": jnp.broadcast_to(a_k_norm[:, :, None], a_k_norm.shape + (ROW_TILE,)),
        "a_w_o": a_w_o.astype(BF16),
        "b_w_qv_t": jnp.swapaxes(jnp.concatenate([b_w_qkv[:, :, :HD], b_w_qkv[:, :, 2 * HD:]],
                                                 axis=2), 1, 2).astype(BF16),
        "b_w_k": b_w_qkv[:, :, HD:2 * HD].astype(BF16),
        "b_w_o": b_w_o.astype(BF16),
        "mlp_w_in": mlp_w_in.astype(BF16),
        "mlp_w_out": mlp_w_out.astype(BF16),
    }
    outs = []
    for x in (x_prompt, x_sample):
        s = x.shape[1]
        cos_t, sin_t = _rope_tables_t(s)
        p = dict(shared, cos_t=cos_t, sin_t=sin_t,
                 b_bias=jnp.stack([_natten_bias(rb, s // GRID_W) for rb in b_rel_bias]))
        outs.append(_trunk(x, p))
    return tuple(outs)
```

```python
import functools

import jax
import jax.numpy as jnp
from jax import lax
from jax.experimental import pallas as pl
from jax.experimental.pallas import tpu as pltpu

D_MODEL = 1024
GRID_W = 64
N_HEADS = 16
HEAD_DIM = 64
A_KV_HEADS = 4
A_GROUP = N_HEADS // A_KV_HEADS
ROPE_THETA = 10000.0
AXIS_DIM = HEAD_DIM // 2
ROPE_HALF = AXIS_DIM // 2
NA_WIN_H = 8
NA_WIN_W = 16
NA_BIAS_H = 2 * NA_WIN_H - 1
NA_BIAS_W = 2 * NA_WIN_W - 1
D_FF = 4 * D_MODEL
NORM_EPS = 1e-6
LOG2_E = 1.4426950408889634
NA_SCORE_SCALE = HEAD_DIM ** -0.5 * LOG2_E
HD = N_HEADS * HEAD_DIM
A_QKV = (N_HEADS + 2 * A_KV_HEADS) * HEAD_DIM

ROW_TILE = 512
FLASH_TQ = 256
FLASH_TK = 512
FLASH_HEADS = 4
FLASH_L_ROWS = 16
NA_Q_ROWS = 4
NA_K_ROWS = 12
NA_HEADS = 4
O_GROUPS = 4
O_GROUP_WIDTH = 256
NA_NQ = NA_Q_ROWS * GRID_W
NA_NK = NA_K_ROWS * GRID_W
FF_CHUNK = 1024
VMEM_LIMIT_BYTES = 56 * 1024 * 1024

BF16 = jnp.bfloat16
F32 = jnp.float32
F8 = jnp.float8_e4m3fn
F8_MAX = 448.0
QK_DEPTH = 4 * HEAD_DIM


def _params(n_axes):
    return pltpu.CompilerParams(
        dimension_semantics=("arbitrary",) * n_axes, vmem_limit_bytes=VMEM_LIMIT_BYTES)


def _resident(shape):
    zeros = (0,) * len(shape)
    return pl.BlockSpec(shape, lambda *_: zeros, pipeline_mode=pl.Buffered(1))


def _split_f8(x):
    x = jnp.clip(x, -F8_MAX, F8_MAX)
    hi = x.astype(F8)
    lo = (x - hi.astype(F32)).astype(F8)
    return hi, lo


def _rms(x, gain):
    ms = jnp.mean(x * x, axis=-1, keepdims=True)
    return x * lax.rsqrt(ms + NORM_EPS) * gain


def _qkv_a_kernel(x_ref, g_ref, wt_ref, qg_ref, kg_ref, cos_ref, sin_ref,
                  q_ref, k_ref, v_ref, s_ref, *, tm):
    h = _rms(x_ref[0], g_ref[...]).astype(BF16)
    qkv_t = lax.dot_general(wt_ref[...], h, (((1,), (1,)), ((), ())),
                            preferred_element_type=F32)
    n_slots = N_HEADS + 2 * A_KV_HEADS
    s_ref[...] = qkv_t.reshape(n_slots, HEAD_DIM, tm)

    def norm_rope(lo, hi, gain_ref, store):
        t = s_ref[lo:hi]
        r = lax.rsqrt(jnp.sum(t * t, axis=1, keepdims=True) * (1.0 / HEAD_DIM) + NORM_EPS)
        for a in range(2):
            d1 = a * AXIS_DIM
            d2 = d1 + ROPE_HALF
            x1 = s_ref[lo:hi, d1:d1 + ROPE_HALF, :] * r * gain_ref[d1:d1 + ROPE_HALF, :]
            x2 = s_ref[lo:hi, d2:d2 + ROPE_HALF, :] * r * gain_ref[d2:d2 + ROPE_HALF, :]
            c = cos_ref[a * ROPE_HALF:(a + 1) * ROPE_HALF, :]
            s = sin_ref[a * ROPE_HALF:(a + 1) * ROPE_HALF, :]
            store(d1, x1 * c - x2 * s)
            store(d2, x2 * c + x1 * s)

    def store_q(d, val):
        s_ref[0:N_HEADS, d:d + ROPE_HALF, :] = val

    def store_k(d, val):
        s_ref[N_HEADS:N_HEADS + A_KV_HEADS, d:d + ROPE_HALF, :] = val

    norm_rope(0, N_HEADS, qg_ref, store_q)
    norm_rope(N_HEADS, N_HEADS + A_KV_HEADS, kg_ref, store_k)
    q_hi, q_lo = _split_f8(s_ref[0:N_HEADS])
    for part, val in enumerate((q_hi, q_hi, q_lo, q_lo)):
        q_ref[0, :, part * HEAD_DIM:(part + 1) * HEAD_DIM, :] = val
    for g in range(A_KV_HEADS):
        k_hi, k_lo = _split_f8(s_ref[N_HEADS + g])
        k_hi, k_lo = k_hi.astype(F32), k_lo.astype(F32)
        k_ext_t = jnp.concatenate([k_hi, k_lo, k_hi, k_lo], axis=0)
        k_ref[0, g] = k_ext_t.T.astype(F8)
        for c in range(tm // FLASH_TK):
            v_ref[0, g, c] = s_ref[N_HEADS + A_KV_HEADS + g, :,
                                   c * FLASH_TK:(c + 1) * FLASH_TK].astype(BF16)


def _qkv_a(x, gain, w_t, q_gain, k_gain, cos_t, sin_t):
    b, s, _ = x.shape
    tm = ROW_TILE
    n_slots = N_HEADS + 2 * A_KV_HEADS
    return pl.pallas_call(
        functools.partial(_qkv_a_kernel, tm=tm),
        grid=(b, s // tm),
        in_specs=[
            pl.BlockSpec((1, tm, D_MODEL), lambda bi, i: (bi, i, 0)),
            _resident((1, D_MODEL)),
            _resident((A_QKV, D_MODEL)),
            _resident((HEAD_DIM, tm)),
            _resident((HEAD_DIM, tm)),
            pl.BlockSpec((AXIS_DIM, tm), lambda bi, i: (0, i)),
            pl.BlockSpec((AXIS_DIM, tm), lambda bi, i: (0, i)),
        ],
        out_specs=[
            pl.BlockSpec((1, N_HEADS, QK_DEPTH, tm), lambda bi, i: (bi, 0, 0, i)),
            pl.BlockSpec((1, A_KV_HEADS, tm, QK_DEPTH), lambda bi, i: (bi, 0, i, 0)),
            pl.BlockSpec((1, A_KV_HEADS, tm // FLASH_TK, HEAD_DIM, FLASH_TK),
                         lambda bi, i: (bi, 0, i, 0, 0)),
        ],
        out_shape=[
            jax.ShapeDtypeStruct((b, N_HEADS, QK_DEPTH, s), F8),
            jax.ShapeDtypeStruct((b, A_KV_HEADS, s, QK_DEPTH), F8),
            jax.ShapeDtypeStruct((b, A_KV_HEADS, s // FLASH_TK, HEAD_DIM, FLASH_TK), BF16),
        ],
        scratch_shapes=[pltpu.VMEM((n_slots, HEAD_DIM, tm), F32)],
        compiler_params=_params(2),
        name="qkv_a",
    )(x, gain, w_t, q_gain, k_gain, cos_t, sin_t)


def _flash_kernel(q_ref, k_ref, v_ref, qn_ref, kn_ref, o_ref, s_ref, cm_ref, *, n_chunks, tq):
    first = (pl.program_id(0) == 0) & (pl.program_id(1) == 0) & (pl.program_id(2) == 0)

    def scores(c, slot, e, q_src=q_ref, k_src=k_ref):
        k_c = k_src[0, 0, c * FLASH_TK:(c + 1) * FLASH_TK, :]
        s_t = jnp.dot(k_c, q_src[0, e], preferred_element_type=F32)
        s_ref[slot, e] = s_t
        cm_ref[slot, e] = jnp.max(s_t, axis=0, keepdims=True)

    ones = jnp.ones((FLASH_L_ROWS, FLASH_TK), BF16)

    def update(c, slot, e, stat):
        m, acc = stat
        s_t = s_ref[slot, e]
        m_new = jnp.maximum(m, cm_ref[slot, e])
        alpha = jnp.exp2(m - m_new)
        p = jnp.exp2(s_t - m_new).astype(BF16)
        v_ext = jnp.concatenate([v_ref[0, 0, c], ones], axis=0)
        acc = alpha * acc + jnp.dot(v_ext, p, preferred_element_type=F32)
        return m_new, acc

    @pl.when(first)
    def _():
        for e in range(FLASH_HEADS):
            scores(0, 0, e)

    stats = tuple((jnp.full((1, tq), -jnp.inf, F32),
                   jnp.zeros((HEAD_DIM + FLASH_L_ROWS, tq), F32)) for _ in range(FLASH_HEADS))
    for c in range(n_chunks):
        new = []
        for e in range(FLASH_HEADS):
            if c + 1 < n_chunks:
                scores(c + 1, (c + 1) % 2, e)
            else:
                scores(0, 0, e, qn_ref, kn_ref)
            new.append(update(c, c % 2, e, stats[e]))
        stats = tuple(new)
    o_t = jnp.concatenate([acc[:HEAD_DIM] / acc[HEAD_DIM:HEAD_DIM + 1] for (_, acc) in stats],
                          axis=0)
    o_ref[0, 0] = o_t.T.astype(BF16)


def _flash(q_t, k, v_t):
    b, _, _, s = q_t.shape
    tq = FLASH_TQ
    n_chunks = s // FLASH_TK
    pairs = N_HEADS // FLASH_HEADS
    per_kv = A_GROUP // FLASH_HEADS
    n_q = s // tq
    assert n_chunks % 2 == 0

    def following(bi, hp, i):
        flat = jnp.minimum((bi * pairs + hp) * n_q + i + 1, b * pairs * n_q - 1)
        return flat // (pairs * n_q), (flat // n_q) % pairs, flat % n_q

    def q_next(bi, hp, i):
        nb, nh, ni = following(bi, hp, i)
        return nb, nh, 0, ni

    def k_next(bi, hp, i):
        nb, nh, _ = following(bi, hp, i)
        return nb, nh // per_kv, 0, 0

    return pl.pallas_call(
        functools.partial(_flash_kernel, n_chunks=n_chunks, tq=tq),
        grid=(b, pairs, n_q),
        in_specs=[
            pl.BlockSpec((1, FLASH_HEADS, QK_DEPTH, tq), lambda bi, hp, i: (bi, hp, 0, i)),
            pl.BlockSpec((1, 1, s, QK_DEPTH), lambda bi, hp, i: (bi, hp // per_kv, 0, 0)),
            pl.BlockSpec((1, 1, n_chunks, HEAD_DIM, FLASH_TK),
                         lambda bi, hp, i: (bi, hp // per_kv, 0, 0, 0)),
            pl.BlockSpec((1, FLASH_HEADS, QK_DEPTH, tq), q_next),
            pl.BlockSpec((1, 1, s, QK_DEPTH), k_next),
        ],
        out_specs=pl.BlockSpec((1, 1, tq, O_GROUP_WIDTH), lambda bi, hp, i: (bi, hp, i, 0)),
        out_shape=jax.ShapeDtypeStruct((b, O_GROUPS, s, O_GROUP_WIDTH), BF16),
        scratch_shapes=[pltpu.VMEM((2, FLASH_HEADS, FLASH_TK, tq), F32),
                        pltpu.VMEM((2, FLASH_HEADS, 1, tq), F32)],
        compiler_params=_params(3),
        name="flash_a",
    )(q_t, k, v_t, q_t, k)


def _mlp_kernel(x_ref, a_ref, wo_ref, g_ref, win_ref, wout_ref, gf_ref, o_ref, *, final):
    a = jnp.concatenate([a_ref[0, g] for g in range(O_GROUPS)], axis=1)
    x = x_ref[0] + jnp.dot(a, wo_ref[...], preferred_element_type=F32)
    h = _rms(x, g_ref[...]).astype(BF16)
    acc = x
    for f in range(D_FF // FF_CHUNK):
        u = jnp.dot(h, win_ref[:, f * FF_CHUNK:(f + 1) * FF_CHUNK], preferred_element_type=F32)
        u = jnp.maximum(u, 0.0)
        acc = acc + jnp.dot((u * u).astype(BF16), wout_ref[f * FF_CHUNK:(f + 1) * FF_CHUNK, :],
                            preferred_element_type=F32)
    if final:
        acc = _rms(acc, gf_ref[...])
    o_ref[0] = acc


def _mlp(x, attn, w_o, gain, w_in, w_out, gain_final, final):
    b, s, _ = x.shape
    tm = ROW_TILE
    return pl.pallas_call(
        functools.partial(_mlp_kernel, final=final),
        grid=(b, s // tm),
        in_specs=[
            pl.BlockSpec((1, tm, D_MODEL), lambda bi, i: (bi, i, 0)),
            pl.BlockSpec((1, O_GROUPS, tm, O_GROUP_WIDTH), lambda bi, i: (bi, 0, i, 0)),
            _resident((HD, D_MODEL)),
            _resident((1, D_MODEL)),
            _resident((D_MODEL, D_FF)),
            _resident((D_FF, D_MODEL)),
            _resident((1, D_MODEL)),
        ],
        out_specs=pl.BlockSpec((1, tm, D_MODEL), lambda bi, i: (bi, i, 0)),
        out_shape=jax.ShapeDtypeStruct((b, s, D_MODEL), F32),
        compiler_params=_params(2),
        name="mlp_final" if final else "mlp",
    )(x, attn, w_o, gain, w_in, w_out, gain_final)


def _qkv_b_kernel(x_ref, g_ref, wqv_t_ref, wk_ref, q_ref, k_ref, v_ref, *, tm):
    h = _rms(x_ref[0], g_ref[...]).astype(BF16)
    k = jnp.dot(h, wk_ref[...], preferred_element_type=F32).astype(BF16)
    width = NA_HEADS * HEAD_DIM
    for g in range(N_HEADS // NA_HEADS):
        k_ref[0, g] = k[:, g * width:(g + 1) * width]
    nt = (((1,), (1,)), ((), ()))
    q_t = lax.dot_general(wqv_t_ref[0:HD, :], h, nt, preferred_element_type=F32)
    q_ref[0, 0] = (q_t * NA_SCORE_SCALE).astype(BF16)
    v_t = lax.dot_general(wqv_t_ref[HD:2 * HD, :], h, nt, preferred_element_type=F32)
    for hd in range(N_HEADS):
        for c in range(tm // NA_NQ):
            v_ref[0, hd, c] = v_t[hd * HEAD_DIM:(hd + 1) * HEAD_DIM,
                                  c * NA_NQ:(c + 1) * NA_NQ].astype(BF16)


def _qkv_b(x, gain, wqv_t, wk):
    b, s, _ = x.shape
    tm = 2 * NA_NQ
    groups = N_HEADS // NA_HEADS
    width = NA_HEADS * HEAD_DIM
    return pl.pallas_call(
        functools.partial(_qkv_b_kernel, tm=tm),
        grid=(b, s // tm),
        in_specs=[
            pl.BlockSpec((1, tm, D_MODEL), lambda bi, i: (bi, i, 0)),
            _resident((1, D_MODEL)),
            _resident((2 * HD, D_MODEL)),
            _resident((D_MODEL, HD)),
        ],
        out_specs=[
            pl.BlockSpec((1, 1, HD, tm), lambda bi, i: (bi, i, 0, 0)),
            pl.BlockSpec((1, groups, tm, width), lambda bi, i: (bi, 0, i, 0)),
            pl.BlockSpec((1, N_HEADS, tm // NA_NQ, HEAD_DIM, NA_NQ), lambda bi, i: (bi, 0, i, 0, 0)),
        ],
        out_shape=[
            jax.ShapeDtypeStruct((b, s // tm, HD, tm), BF16),
            jax.ShapeDtypeStruct((b, groups, s, width), BF16),
            jax.ShapeDtypeStruct((b, N_HEADS, s // NA_NQ, HEAD_DIM, NA_NQ), BF16),
        ],
        compiler_params=_params(2),
        name="qkv_b",
    )(x, gain, wqv_t, wk)


def _natten_kernel(q_ref, qn_ref, k_ref, v_ref, b_ref, o_ref, s_ref, cm_ref, *, rows):
    u = pl.program_id(2)
    n_blocks = rows // NA_Q_ROWS
    width = NA_HEADS * HEAD_DIM
    row_id = lax.broadcasted_iota(jnp.int32, (width, NA_NQ), 0)
    ones = jnp.ones((FLASH_L_ROWS, NA_NK), BF16)

    def key_base(blk):
        return jnp.clip(NA_Q_ROWS * blk - NA_WIN_H // 2, 0, rows - NA_K_ROWS)

    def scores(blk, q_blk, slot, e):
        variant = jnp.where(blk == 0, 0, jnp.where(blk == n_blocks - 1, 2, 1))
        start = pl.multiple_of(key_base(blk) * GRID_W, NA_NQ)
        k_w = k_ref[0, 0, pl.ds(start, NA_NK), :]
        mine = (row_id >= e * HEAD_DIM) & (row_id < (e + 1) * HEAD_DIM)
        q_e = jnp.where(mine, q_blk, jnp.zeros_like(q_blk))
        s_t = jnp.dot(k_w, q_e, preferred_element_type=F32) + b_ref[variant, e]
        s_ref[slot, e] = s_t
        cm_ref[slot, e] = jnp.max(s_t, axis=0, keepdims=True)

    def attend(blk, slot, e):
        chunk0 = key_base(blk) // NA_Q_ROWS
        s_t = s_ref[slot, e]
        p = jnp.exp2(s_t - cm_ref[slot, e]).astype(BF16)
        v_t = [v_ref[0, e, chunk0 + j] for j in range(NA_K_ROWS // NA_Q_ROWS)]
        v_ext = jnp.concatenate([jnp.concatenate(v_t, axis=1), ones], axis=0)
        acc = jnp.dot(v_ext, p, preferred_element_type=F32)
        return acc[:HEAD_DIM] / acc[HEAD_DIM:HEAD_DIM + 1]

    @pl.when(u == 0)
    def _():
        for e in range(NA_HEADS):
            scores(0, q_ref[0, 0, :, 0:NA_NQ], 0, e)

    def half(blk, slot, blk_next, q_next, row0):
        outs = []
        for e in range(NA_HEADS):
            scores(blk_next, q_next, 1 - slot, e)
            outs.append(attend(blk, slot, e))
        o_ref[0, 0, row0:row0 + NA_NQ, :] = jnp.concatenate(outs, axis=0).T.astype(BF16)

    half(2 * u, 0, 2 * u + 1, q_ref[0, 0, :, NA_NQ:2 * NA_NQ], 0)
    u_next = jnp.minimum(u + 1, n_blocks // 2 - 1)
    half(2 * u + 1, 1, 2 * u_next, qn_ref[0, 0, :, 0:NA_NQ], NA_NQ)


def _natten(q_t, k, v_t, bias):
    b, _, s, _ = k.shape
    rows = s // GRID_W
    n_blocks = rows // NA_Q_ROWS
    groups = N_HEADS // NA_HEADS
    width = NA_HEADS * HEAD_DIM
    return pl.pallas_call(
        functools.partial(_natten_kernel, rows=rows),
        grid=(groups, b, n_blocks // 2),
        in_specs=[
            pl.BlockSpec((1, 1, width, 2 * NA_NQ), lambda hg, bi, u: (bi, u, hg, 0)),
            pl.BlockSpec((1, 1, width, 2 * NA_NQ),
                         lambda hg, bi, u: (bi, jnp.minimum(u + 1, n_blocks // 2 - 1), hg, 0)),
            pl.BlockSpec((1, 1, s, width), lambda hg, bi, u: (bi, hg, 0, 0)),
            pl.BlockSpec((1, NA_HEADS, s // NA_NQ, HEAD_DIM, NA_NQ),
                         lambda hg, bi, u: (bi, hg, 0, 0, 0)),
            pl.BlockSpec((3, NA_HEADS, NA_NK, NA_NQ), lambda hg, bi, u: (0, hg, 0, 0),
                         pipeline_mode=pl.Buffered(1)),
        ],
        out_specs=pl.BlockSpec((1, 1, 2 * NA_NQ, width), lambda hg, bi, u: (bi, hg, u, 0)),
        out_shape=jax.ShapeDtypeStruct((b, groups, s, width), BF16),
        scratch_shapes=[pltpu.VMEM((2, NA_HEADS, NA_NK, NA_NQ), F32),
                        pltpu.VMEM((2, NA_HEADS, 1, NA_NQ), F32)],
        compiler_params=_params(3),
        name="natten_b",
    )(q_t, q_t, k, v_t, bias)


def _rope_tables_t(seq):
    t = jnp.arange(seq)
    row = (t // GRID_W).astype(F32)
    col = (t % GRID_W).astype(F32)
    inv = ROPE_THETA ** (-jnp.arange(0, AXIS_DIM, 2, dtype=F32) / AXIS_DIM)
    ang = jnp.concatenate([inv[:, None] * row[None, :], inv[:, None] * col[None, :]], axis=0)
    return jnp.cos(ang), jnp.sin(ang)


def _natten_bias(rel_bias, rows):
    n_blocks = rows // NA_Q_ROWS
    t = jnp.array([0, 1, n_blocks - 1])[:, None, None]
    a = jnp.arange(NA_Q_ROWS)[None, :, None]
    i = jnp.arange(NA_K_ROWS)[None, None, :]
    r = NA_Q_ROWS * t + a
    r0 = jnp.clip(r - NA_WIN_H // 2, 0, rows - NA_WIN_H)
    kr = jnp.clip(NA_Q_ROWS * t - NA_WIN_H // 2, 0, rows - NA_K_ROWS) + i
    valid_r = (kr >= r0) & (kr < r0 + NA_WIN_H)
    c = jnp.arange(GRID_W)[:, None]
    kc = jnp.arange(GRID_W)[None, :]
    c0 = jnp.clip(c - NA_WIN_W // 2, 0, GRID_W - NA_WIN_W)
    valid_c = (kc >= c0) & (kc < c0 + NA_WIN_W)
    pick_r = ((kr - r + NA_WIN_H - 1)[..., None] == jnp.arange(NA_BIAS_H)).astype(F32)
    pick_c = ((kc - c + NA_WIN_W - 1)[..., None] == jnp.arange(NA_BIAS_W)).astype(F32)
    vals = jnp.einsum("vaid,hde,cke->vhikac", pick_r, rel_bias.astype(F32), pick_c,
                      precision=lax.Precision.HIGHEST)
    valid = (valid_r.transpose(0, 2, 1)[:, None, :, None, :, None]
             & valid_c.T[None, None, None, :, None, :])
    vals = jnp.where(valid, vals * LOG2_E, -jnp.inf)
    return vals.reshape(3, N_HEADS, NA_NK, NA_NQ)


def _trunk(x, p):
    _, s, _ = x.shape
    depth = p["norm_mix"].shape[0]
    for i in range(depth):
        j = i // 2
        g_mix = p["norm_mix"][i][None, :]
        if i % 2 == 0:
            q_t, k, v_t = _qkv_a(x, g_mix, p["a_w_qkv_t"][j], p["a_q_gain"][j], p["a_k_gain"][j],
                                 p["cos_t"], p["sin_t"])
            o = _flash(q_t, k, v_t)
            w_o = p["a_w_o"][j]
        else:
            q_t, k, v_t = _qkv_b(x, g_mix, p["b_w_qv_t"][j], p["b_w_k"][j])
            o = _natten(q_t, k, v_t, p["b_bias"][j])
            w_o = p["b_w_o"][j]
        x = _mlp(x, o, w_o, p["norm_mlp"][i][None, :], p["mlp_w_in"][i], p["mlp_w_out"][i],
                 p["norm_final"][None, :], final=(i == depth - 1))
    return x


def kernel(x_prompt, x_sample, norm_mix, norm_mlp, norm_final, a_w_qkv, a_q_norm, a_k_norm, a_w_o,
           b_w_qkv, b_rel_bias, b_w_o, mlp_w_in, mlp_w_out):
    scale = HEAD_DIM ** -0.5 * LOG2_E
    q_bound = jnp.max(jnp.abs(a_q_norm), axis=1) * scale
    k_bound = jnp.max(jnp.abs(a_k_norm), axis=1)
    ratio = k_bound / q_bound
    ratio = jnp.where(jnp.isfinite(ratio) & (ratio > 0), ratio, 1.0)
    balance = jnp.exp2(jnp.round(0.5 * jnp.log2(ratio)))[:, None]
    shared = {
        "norm_mix": norm_mix, "norm_mlp": norm_mlp, "norm_final": norm_final,
        "a_w_qkv_t": jnp.swapaxes(a_w_qkv, 1, 2).astype(BF16),
        "a_q_gain": jnp.broadcast_to((a_q_norm * scale * balance)[:, :, None],
                                     a_q_norm.shape + (ROW_TILE,)),
        "a_k_gain": jnp.broadcast_to((a_k_norm / balance)[:, :, None],
                                     a_k_norm.shape + (ROW_TILE,)),
        "a_w_o": a_w_o.astype(BF16),
        "b_w_qv_t": jnp.swapaxes(jnp.concatenate([b_w_qkv[:, :, :HD], b_w_qkv[:, :, 2 * HD:]],
                                                 axis=2), 1, 2).astype(BF16),
        "b_w_k": b_w_qkv[:, :, HD:2 * HD].astype(BF16),
        "b_w_o": b_w_o.astype(BF16),
        "mlp_w_in": mlp_w_in.astype(BF16),
        "mlp_w_out": mlp_w_out.astype(BF16),
    }
    outs = []
    for x in (x_prompt, x_sample):
        s = x.shape[1]
        cos_t, sin_t = _rope_tables_t(s)
        p = dict(shared, cos_t=cos_t, sin_t=sin_t,
                 b_bias=jnp.stack([_natten_bias(rb, s // GRID_W) for rb in b_rel_bias]))
        outs.append(_trunk(x, p))
    return tuple(outs)
```

```python
import functools

import jax
import jax.numpy as jnp
from jax import lax
from jax.experimental import pallas as pl
from jax.experimental.pallas import tpu as pltpu

D_MODEL = 1024
GRID_W = 64
N_HEADS = 16
HEAD_DIM = 64
A_KV_HEADS = 4
A_GROUP = N_HEADS // A_KV_HEADS
ROPE_THETA = 10000.0
AXIS_DIM = HEAD_DIM // 2
ROPE_HALF = AXIS_DIM // 2
NA_WIN_H = 8
NA_WIN_W = 16
NA_BIAS_H = 2 * NA_WIN_H - 1
NA_BIAS_W = 2 * NA_WIN_W - 1
D_FF = 4 * D_MODEL
NORM_EPS = 1e-6
LOG2_E = 1.4426950408889634
NA_SCORE_SCALE = HEAD_DIM ** -0.5 * LOG2_E
HD = N_HEADS * HEAD_DIM
A_QKV = (N_HEADS + 2 * A_KV_HEADS) * HEAD_DIM

ROW_TILE = 512
FLASH_TQ = 256
FLASH_Q_TILES = 2
FLASH_TK = 512
FLASH_HEADS = 4
FLASH_L_ROWS = 16
NA_Q_ROWS = 4
NA_K_ROWS = 12
NA_HEADS = 4
NA_BLOCKS = 4
O_GROUPS = 4
O_GROUP_WIDTH = 256
NA_NQ = NA_Q_ROWS * GRID_W
NA_NK = NA_K_ROWS * GRID_W
FF_CHUNK = 1024
VMEM_LIMIT_BYTES = 56 * 1024 * 1024

BF16 = jnp.bfloat16
F32 = jnp.float32
F8 = jnp.float8_e4m3fn
F8_MAX = 448.0
QK_DEPTH = 4 * HEAD_DIM


def _params(n_axes):
    return pltpu.CompilerParams(
        dimension_semantics=("arbitrary",) * n_axes, vmem_limit_bytes=VMEM_LIMIT_BYTES)


def _resident(shape):
    zeros = (0,) * len(shape)
    return pl.BlockSpec(shape, lambda *_: zeros, pipeline_mode=pl.Buffered(1))


def _split_f8(x):
    x = jnp.clip(x, -F8_MAX, F8_MAX)
    hi = x.astype(F8)
    lo = (x - hi.astype(F32)).astype(F8)
    return hi, lo


def _rms(x, gain):
    ms = jnp.mean(x * x, axis=-1, keepdims=True)
    return x * lax.rsqrt(ms + NORM_EPS) * gain


def _qkv_a_kernel(x_ref, g_ref, wt_ref, qg_ref, kg_ref, cos_ref, sin_ref,
                  q_ref, k_ref, v_ref, s_ref, *, tm):
    h = _rms(x_ref[0], g_ref[...]).astype(BF16)
    qkv_t = lax.dot_general(wt_ref[...], h, (((1,), (1,)), ((), ())),
                            preferred_element_type=F32)
    n_slots = N_HEADS + 2 * A_KV_HEADS
    s_ref[...] = qkv_t.reshape(n_slots, HEAD_DIM, tm)

    def norm_rope(lo, hi, gain_ref, store):
        t = s_ref[lo:hi]
        r = lax.rsqrt(jnp.sum(t * t, axis=1, keepdims=True) * (1.0 / HEAD_DIM) + NORM_EPS)
        for a in range(2):
            d1 = a * AXIS_DIM
            d2 = d1 + ROPE_HALF
            x1 = s_ref[lo:hi, d1:d1 + ROPE_HALF, :] * r * gain_ref[d1:d1 + ROPE_HALF, :]
            x2 = s_ref[lo:hi, d2:d2 + ROPE_HALF, :] * r * gain_ref[d2:d2 + ROPE_HALF, :]
            c = cos_ref[a * ROPE_HALF:(a + 1) * ROPE_HALF, :]
            s = sin_ref[a * ROPE_HALF:(a + 1) * ROPE_HALF, :]
            store(d1, x1 * c - x2 * s)
            store(d2, x2 * c + x1 * s)

    def store_q(d, val):
        s_ref[0:N_HEADS, d:d + ROPE_HALF, :] = val

    def store_k(d, val):
        s_ref[N_HEADS:N_HEADS + A_KV_HEADS, d:d + ROPE_HALF, :] = val

    norm_rope(0, N_HEADS, qg_ref, store_q)
    norm_rope(N_HEADS, N_HEADS + A_KV_HEADS, kg_ref, store_k)
    q_hi, q_lo = _split_f8(s_ref[0:N_HEADS])
    for part, val in enumerate((q_hi, q_hi, q_lo, q_lo)):
        q_ref[0, :, part * HEAD_DIM:(part + 1) * HEAD_DIM, :] = val
    for g in range(A_KV_HEADS):
        k_hi, k_lo = _split_f8(s_ref[N_HEADS + g])
        k_hi, k_lo = k_hi.astype(F32), k_lo.astype(F32)
        k_ext_t = jnp.concatenate([k_hi, k_lo, k_hi, k_lo], axis=0)
        k_ref[0, g] = k_ext_t.T.astype(F8)
        for c in range(tm // FLASH_TK):
            v_ref[0, g, c] = s_ref[N_HEADS + A_KV_HEADS + g, :,
                                   c * FLASH_TK:(c + 1) * FLASH_TK].astype(BF16)


def _qkv_a(x, gain, w_t, q_gain, k_gain, cos_t, sin_t):
    b, s, _ = x.shape
    tm = ROW_TILE
    n_slots = N_HEADS + 2 * A_KV_HEADS
    return pl.pallas_call(
        functools.partial(_qkv_a_kernel, tm=tm),
        grid=(b, s // tm),
        in_specs=[
            pl.BlockSpec((1, tm, D_MODEL), lambda bi, i: (bi, i, 0)),
            _resident((1, D_MODEL)),
            _resident((A_QKV, D_MODEL)),
            _resident((HEAD_DIM, tm)),
            _resident((HEAD_DIM, tm)),
            pl.BlockSpec((AXIS_DIM, tm), lambda bi, i: (0, i)),
            pl.BlockSpec((AXIS_DIM, tm), lambda bi, i: (0, i)),
        ],
        out_specs=[
            pl.BlockSpec((1, N_HEADS, QK_DEPTH, tm), lambda bi, i: (bi, 0, 0, i)),
            pl.BlockSpec((1, A_KV_HEADS, tm, QK_DEPTH), lambda bi, i: (bi, 0, i, 0)),
            pl.BlockSpec((1, A_KV_HEADS, tm // FLASH_TK, HEAD_DIM, FLASH_TK),
                         lambda bi, i: (bi, 0, i, 0, 0)),
        ],
        out_shape=[
            jax.ShapeDtypeStruct((b, N_HEADS, QK_DEPTH, s), F8),
            jax.ShapeDtypeStruct((b, A_KV_HEADS, s, QK_DEPTH), F8),
            jax.ShapeDtypeStruct((b, A_KV_HEADS, s // FLASH_TK, HEAD_DIM, FLASH_TK), BF16),
        ],
        scratch_shapes=[pltpu.VMEM((n_slots, HEAD_DIM, tm), F32)],
        compiler_params=_params(2),
        name="qkv_a",
    )(x, gain, w_t, q_gain, k_gain, cos_t, sin_t)


def _flash_kernel(q_ref, k_ref, v_ref, qn_ref, kn_ref, o_ref, s_ref, cm_ref, *, n_chunks, tq):
    first = (pl.program_id(0) == 0) & (pl.program_id(1) == 0) & (pl.program_id(2) == 0)

    def scores(c, slot, e, j, q_src=q_ref, k_src=k_ref):
        k_c = k_src[0, 0, c * FLASH_TK:(c + 1) * FLASH_TK, :]
        q_e = q_src[0, e, :, j * tq:(j + 1) * tq]
        s_t = jnp.dot(k_c, q_e, preferred_element_type=F32)
        s_ref[slot, e] = s_t
        cm_ref[slot, e] = jnp.max(s_t, axis=0, keepdims=True)

    ones = jnp.ones((FLASH_L_ROWS, FLASH_TK), BF16)

    def update(c, slot, e, stat):
        m, acc = stat
        s_t = s_ref[slot, e]
        m_new = jnp.maximum(m, cm_ref[slot, e])
        alpha = jnp.exp2(m - m_new)
        p = jnp.exp2(s_t - m_new).astype(BF16)
        v_ext = jnp.concatenate([v_ref[0, 0, c], ones], axis=0)
        acc = alpha * acc + jnp.dot(v_ext, p, preferred_element_type=F32)
        return m_new, acc

    @pl.when(first)
    def _():
        for e in range(FLASH_HEADS):
            scores(0, 0, e, 0)

    for j in range(FLASH_Q_TILES):
        stats = tuple((jnp.full((1, tq), -jnp.inf, F32),
                       jnp.zeros((HEAD_DIM + FLASH_L_ROWS, tq), F32)) for _ in range(FLASH_HEADS))
        for c in range(n_chunks):
            new = []
            for e in range(FLASH_HEADS):
                if c + 1 < n_chunks:
                    scores(c + 1, (c + 1) % 2, e, j)
                elif j + 1 < FLASH_Q_TILES:
                    scores(0, 0, e, j + 1)
                else:
                    scores(0, 0, e, 0, qn_ref, kn_ref)
                new.append(update(c, c % 2, e, stats[e]))
            stats = tuple(new)
        o_t = jnp.concatenate([acc[:HEAD_DIM] / acc[HEAD_DIM:HEAD_DIM + 1] for (_, acc) in stats],
                              axis=0)
        o_ref[0, 0, j * tq:(j + 1) * tq, :] = o_t.T.astype(BF16)


def _flash(q_t, k, v_t):
    b, _, _, s = q_t.shape
    tq = FLASH_TQ
    n_chunks = s // FLASH_TK
    pairs = N_HEADS // FLASH_HEADS
    per_kv = A_GROUP // FLASH_HEADS
    tq_step = tq * FLASH_Q_TILES
    n_q = s // tq_step
    assert n_chunks % 2 == 0

    def following(bi, hp, i):
        flat = jnp.minimum((bi * pairs + hp) * n_q + i + 1, b * pairs * n_q - 1)
        return flat // (pairs * n_q), (flat // n_q) % pairs, flat % n_q

    def q_next(bi, hp, i):
        nb, nh, ni = following(bi, hp, i)
        return nb, nh, 0, ni

    def k_next(bi, hp, i):
        nb, nh, _ = following(bi, hp, i)
        return nb, nh // per_kv, 0, 0

    return pl.pallas_call(
        functools.partial(_flash_kernel, n_chunks=n_chunks, tq=tq),
        grid=(b, pairs, n_q),
        in_specs=[
            pl.BlockSpec((1, FLASH_HEADS, QK_DEPTH, tq_step), lambda bi, hp, i: (bi, hp, 0, i)),
            pl.BlockSpec((1, 1, s, QK_DEPTH), lambda bi, hp, i: (bi, hp // per_kv, 0, 0)),
            pl.BlockSpec((1, 1, n_chunks, HEAD_DIM, FLASH_TK),
                         lambda bi, hp, i: (bi, hp // per_kv, 0, 0, 0)),
            pl.BlockSpec((1, FLASH_HEADS, QK_DEPTH, tq_step), q_next),
            pl.BlockSpec((1, 1, s, QK_DEPTH), k_next),
        ],
        out_specs=pl.BlockSpec((1, 1, tq_step, O_GROUP_WIDTH), lambda bi, hp, i: (bi, hp, i, 0)),
        out_shape=jax.ShapeDtypeStruct((b, O_GROUPS, s, O_GROUP_WIDTH), BF16),
        scratch_shapes=[pltpu.VMEM((2, FLASH_HEADS, FLASH_TK, tq), F32),
                        pltpu.VMEM((2, FLASH_HEADS, 1, tq), F32)],
        compiler_params=_params(3),
        name="flash_a",
    )(q_t, k, v_t, q_t, k)


def _mlp_kernel(x_ref, a_ref, wo_ref, g_ref, win_ref, wout_ref, gf_ref, o_ref, *, final):
    a = jnp.concatenate([a_ref[0, g] for g in range(O_GROUPS)], axis=1)
    x = x_ref[0] + jnp.dot(a, wo_ref[...], preferred_element_type=F32)
    h = _rms(x, g_ref[...]).astype(BF16)
    acc = x
    for f in range(D_FF // FF_CHUNK):
        u = jnp.dot(h, win_ref[:, f * FF_CHUNK:(f + 1) * FF_CHUNK], preferred_element_type=F32)
        u = jnp.maximum(u, 0.0)
        acc = acc + jnp.dot((u * u).astype(BF16), wout_ref[f * FF_CHUNK:(f + 1) * FF_CHUNK, :],
                            preferred_element_type=F32)
    if final:
        acc = _rms(acc, gf_ref[...])
    o_ref[0] = acc


def _mlp(x, attn, w_o, gain, w_in, w_out, gain_final, final):
    b, s, _ = x.shape
    tm = ROW_TILE
    return pl.pallas_call(
        functools.partial(_mlp_kernel, final=final),
        grid=(b, s // tm),
        in_specs=[
            pl.BlockSpec((1, tm, D_MODEL), lambda bi, i: (bi, i, 0)),
            pl.BlockSpec((1, O_GROUPS, tm, O_GROUP_WIDTH), lambda bi, i: (bi, 0, i, 0)),
            _resident((HD, D_MODEL)),
            _resident((1, D_MODEL)),
            _resident((D_MODEL, D_FF)),
            _resident((D_FF, D_MODEL)),
            _resident((1, D_MODEL)),
        ],
        out_specs=pl.BlockSpec((1, tm, D_MODEL), lambda bi, i: (bi, i, 0)),
        out_shape=jax.ShapeDtypeStruct((b, s, D_MODEL), F32),
        compiler_params=_params(2),
        name="mlp_final" if final else "mlp",
    )(x, attn, w_o, gain, w_in, w_out, gain_final)


def _qkv_b_kernel(x_ref, g_ref, wqv_t_ref, wk_ref, q_ref, k_ref, v_ref, *, tm):
    h = _rms(x_ref[0], g_ref[...]).astype(BF16)
    k = jnp.dot(h, wk_ref[...], preferred_element_type=F32).astype(BF16)
    width = NA_HEADS * HEAD_DIM
    for g in range(N_HEADS // NA_HEADS):
        k_ref[0, g] = k[:, g * width:(g + 1) * width]
    nt = (((1,), (1,)), ((), ()))
    q_t = lax.dot_general(wqv_t_ref[0:HD, :], h, nt, preferred_element_type=F32)
    q_ref[0, 0] = (q_t * NA_SCORE_SCALE).astype(BF16)
    v_t = lax.dot_general(wqv_t_ref[HD:2 * HD, :], h, nt, preferred_element_type=F32)
    for hd in range(N_HEADS):
        for c in range(tm // NA_NQ):
            v_ref[0, hd, c] = v_t[hd * HEAD_DIM:(hd + 1) * HEAD_DIM,
                                  c * NA_NQ:(c + 1) * NA_NQ].astype(BF16)


def _qkv_b(x, gain, wqv_t, wk):
    b, s, _ = x.shape
    tm = 2 * NA_NQ
    groups = N_HEADS // NA_HEADS
    width = NA_HEADS * HEAD_DIM
    return pl.pallas_call(
        functools.partial(_qkv_b_kernel, tm=tm),
        grid=(b, s // tm),
        in_specs=[
            pl.BlockSpec((1, tm, D_MODEL), lambda bi, i: (bi, i, 0)),
            _resident((1, D_MODEL)),
            _resident((2 * HD, D_MODEL)),
            _resident((D_MODEL, HD)),
        ],
        out_specs=[
            pl.BlockSpec((1, 1, HD, tm), lambda bi, i: (bi, i, 0, 0)),
            pl.BlockSpec((1, groups, tm, width), lambda bi, i: (bi, 0, i, 0)),
            pl.BlockSpec((1, N_HEADS, tm // NA_NQ, HEAD_DIM, NA_NQ), lambda bi, i: (bi, 0, i, 0, 0)),
        ],
        out_shape=[
            jax.ShapeDtypeStruct((b, s // tm, HD, tm), BF16),
            jax.ShapeDtypeStruct((b, groups, s, width), BF16),
            jax.ShapeDtypeStruct((b, N_HEADS, s // NA_NQ, HEAD_DIM, NA_NQ), BF16),
        ],
        compiler_params=_params(2),
        name="qkv_b",
    )(x, gain, wqv_t, wk)


def _natten_kernel(q_ref, qn_ref, k_ref, v_ref, b_ref, o_ref, s_ref, cm_ref, *, rows):
    u = pl.program_id(2)
    n_blocks = rows // NA_Q_ROWS
    width = NA_HEADS * HEAD_DIM
    row_id = lax.broadcasted_iota(jnp.int32, (width, NA_NQ), 0)
    ones = jnp.ones((FLASH_L_ROWS, NA_NK), BF16)

    def key_base(blk):
        return jnp.clip(NA_Q_ROWS * blk - NA_WIN_H // 2, 0, rows - NA_K_ROWS)

    def scores(blk, q_blk, slot, e):
        variant = jnp.where(blk == 0, 0, jnp.where(blk == n_blocks - 1, 2, 1))
        start = pl.multiple_of(key_base(blk) * GRID_W, NA_NQ)
        k_w = k_ref[0, 0, pl.ds(start, NA_NK), :]
        mine = (row_id >= e * HEAD_DIM) & (row_id < (e + 1) * HEAD_DIM)
        q_e = jnp.where(mine, q_blk, jnp.zeros_like(q_blk))
        s_t = jnp.dot(k_w, q_e, preferred_element_type=F32) + b_ref[variant, e]
        s_ref[slot, e] = s_t
        cm_ref[slot, e] = jnp.max(s_t, axis=0, keepdims=True)

    def attend(blk, slot, e):
        chunk0 = key_base(blk) // NA_Q_ROWS
        s_t = s_ref[slot, e]
        p = jnp.exp2(s_t - cm_ref[slot, e]).astype(BF16)
        v_t = [v_ref[0, e, chunk0 + j] for j in range(NA_K_ROWS // NA_Q_ROWS)]
        v_ext = jnp.concatenate([jnp.concatenate(v_t, axis=1), ones], axis=0)
        acc = jnp.dot(v_ext, p, preferred_element_type=F32)
        return acc[:HEAD_DIM] / acc[HEAD_DIM:HEAD_DIM + 1]

    @pl.when(u == 0)
    def _():
        for e in range(NA_HEADS):
            scores(0, q_ref[0, 0, :, 0:NA_NQ], 0, e)

    def half(blk, slot, blk_next, q_next, row0):
        outs = []
        for e in range(NA_HEADS):
            scores(blk_next, q_next, 1 - slot, e)
            outs.append(attend(blk, slot, e))
        o_ref[0, 0, row0:row0 + NA_NQ, :] = jnp.concatenate(outs, axis=0).T.astype(BF16)

    def q_block(n):
        return q_ref[0, n // 2, :, (n % 2) * NA_NQ:(n % 2 + 1) * NA_NQ]

    for n in range(NA_BLOCKS - 1):
        half(NA_BLOCKS * u + n, n % 2, NA_BLOCKS * u + n + 1, q_block(n + 1), n * NA_NQ)
    u_next = jnp.minimum(u + 1, n_blocks // NA_BLOCKS - 1)
    half(NA_BLOCKS * u + NA_BLOCKS - 1, (NA_BLOCKS - 1) % 2, NA_BLOCKS * u_next,
         qn_ref[0, 0, :, 0:NA_NQ], (NA_BLOCKS - 1) * NA_NQ)


def _natten(q_t, k, v_t, bias):
    b, _, s, _ = k.shape
    rows = s // GRID_W
    n_blocks = rows // NA_Q_ROWS
    groups = N_HEADS // NA_HEADS
    width = NA_HEADS * HEAD_DIM
    steps = n_blocks // NA_BLOCKS
    tiles = NA_BLOCKS // 2
    return pl.pallas_call(
        functools.partial(_natten_kernel, rows=rows),
        grid=(groups, b, steps),
        in_specs=[
            pl.BlockSpec((1, tiles, width, 2 * NA_NQ), lambda hg, bi, u: (bi, u, hg, 0)),
            pl.BlockSpec((1, 1, width, 2 * NA_NQ),
                         lambda hg, bi, u: (bi, tiles * jnp.minimum(u + 1, steps - 1), hg, 0)),
            pl.BlockSpec((1, 1, s, width), lambda hg, bi, u: (bi, hg, 0, 0)),
            pl.BlockSpec((1, NA_HEADS, s // NA_NQ, HEAD_DIM, NA_NQ),
                         lambda hg, bi, u: (bi, hg, 0, 0, 0)),
            pl.BlockSpec((3, NA_HEADS, NA_NK, NA_NQ), lambda hg, bi, u: (0, hg, 0, 0),
                         pipeline_mode=pl.Buffered(1)),
        ],
        out_specs=pl.BlockSpec((1, 1, NA_BLOCKS * NA_NQ, width), lambda hg, bi, u: (bi, hg, u, 0)),
        out_shape=jax.ShapeDtypeStruct((b, groups, s, width), BF16),
        scratch_shapes=[pltpu.VMEM((2, NA_HEADS, NA_NK, NA_NQ), F32),
                        pltpu.VMEM((2, NA_HEADS, 1, NA_NQ), F32)],
        compiler_params=_params(3),
        name="natten_b",
    )(q_t, q_t, k, v_t, bias)


def _rope_tables_t(seq):
    t = jnp.arange(seq)
    row = (t // GRID_W).astype(F32)
    col = (t % GRID_W).astype(F32)
    inv = ROPE_THETA ** (-jnp.arange(0, AXIS_DIM, 2, dtype=F32) / AXIS_DIM)
    ang = jnp.concatenate([inv[:, None] * row[None, :], inv[:, None] * col[None, :]], axis=0)
    return jnp.cos(ang), jnp.sin(ang)


def _natten_bias(rel_bias, rows):
    n_blocks = rows // NA_Q_ROWS
    t = jnp.array([0, 1, n_blocks - 1])[:, None, None]
    a = jnp.arange(NA_Q_ROWS)[None, :, None]
    i = jnp.arange(NA_K_ROWS)[None, None, :]
    r = NA_Q_ROWS * t + a
    r0 = jnp.clip(r - NA_WIN_H // 2, 0, rows - NA_WIN_H)
    kr = jnp.clip(NA_Q_ROWS * t - NA_WIN_H // 2, 0, rows - NA_K_ROWS) + i
    valid_r = (kr >= r0) & (kr < r0 + NA_WIN_H)
    c = jnp.arange(GRID_W)[:, None]
    kc = jnp.arange(GRID_W)[None, :]
    c0 = jnp.clip(c - NA_WIN_W // 2, 0, GRID_W - NA_WIN_W)
    valid_c = (kc >= c0) & (kc < c0 + NA_WIN_W)
    pick_r = ((kr - r + NA_WIN_H - 1)[..., None] == jnp.arange(NA_BIAS_H)).astype(F32)
    pick_c = ((kc - c + NA_WIN_W - 1)[..., None] == jnp.arange(NA_BIAS_W)).astype(F32)
    vals = jnp.einsum("vaid,hde,cke->vhikac", pick_r, rel_bias.astype(F32), pick_c,
                      precision=lax.Precision.HIGHEST)
    valid = (valid_r.transpose(0, 2, 1)[:, None, :, None, :, None]
             & valid_c.T[None, None, None, :, None, :])
    vals = jnp.where(valid, vals * LOG2_E, -jnp.inf)
    return vals.reshape(3, N_HEADS, NA_NK, NA_NQ)


def _trunk(x, p):
    _, s, _ = x.shape
    depth = p["norm_mix"].shape[0]
    for i in range(depth):
        j = i // 2
        g_mix = p["norm_mix"][i][None, :]
        if i % 2 == 0:
            q_t, k, v_t = _qkv_a(x, g_mix, p["a_w_qkv_t"][j], p["a_q_gain"][j], p["a_k_gain"][j],
                                 p["cos_t"], p["sin_t"])
            o = _flash(q_t, k, v_t)
            w_o = p["a_w_o"][j]
        else:
            q_t, k, v_t = _qkv_b(x, g_mix, p["b_w_qv_t"][j], p["b_w_k"][j])
            o = _natten(q_t, k, v_t, p["b_bias"][j])
            w_o = p["b_w_o"][j]
        x = _mlp(x, o, w_o, p["norm_mlp"][i][None, :], p["mlp_w_in"][i], p["mlp_w_out"][i],
                 p["norm_final"][None, :], final=(i == depth - 1))
    return x


def kernel(x_prompt, x_sample, norm_mix, norm_mlp, norm_final, a_w_qkv, a_q_norm, a_k_norm, a_w_o,
           b_w_qkv, b_rel_bias, b_w_o, mlp_w_in, mlp_w_out):
    scale = HEAD_DIM ** -0.5 * LOG2_E
    q_bound = jnp.max(jnp.abs(a_q_norm), axis=1) * scale
    k_bound = jnp.max(jnp.abs(a_k_norm), axis=1)
    ratio = k_bound / q_bound
    ratio = jnp.where(jnp.isfinite(ratio) & (ratio > 0), ratio, 1.0)
    balance = jnp.exp2(jnp.round(0.5 * jnp.log2(ratio)))[:, None]
    shared = {
        "norm_mix": norm_mix, "norm_mlp": norm_mlp, "norm_final": norm_final,
        "a_w_qkv_t": jnp.swapaxes(a_w_qkv, 1, 2).astype(BF16),
        "a_q_gain": jnp.broadcast_to((a_q_norm * scale * balance)[:, :, None],
                                     a_q_norm.shape + (ROW_TILE,)),
        "a_k_gain": jnp.broadcast_to((a_k_norm / balance)[:, :, None],
                                     a_k_norm.shape + (ROW_TILE,)),
        "a_w_o": a_w_o.astype(BF16),
        "b_w_qv_t": jnp.swapaxes(jnp.concatenate([b_w_qkv[:, :, :HD], b_w_qkv[:, :, 2 * HD:]],
                                                 axis=2), 1, 2).astype(BF16),
        "b_w_k": b_w_qkv[:, :, HD:2 * HD].astype(BF16),
        "b_w_o": b_w_o.astype(BF16),
        "mlp_w_in": mlp_w_in.astype(BF16),
        "mlp_w_out": mlp_w_out.astype(BF16),
    }
    outs = []
    for x in (x_prompt, x_sample):
        s = x.shape[1]
        cos_t, sin_t = _rope_tables_t(s)
        p = dict(shared, cos_t=cos_t, sin_t=sin_t,
                 b_bias=jnp.stack([_natten_bias(rb, s // GRID_W) for rb in b_rel_bias]))
        outs.append(_trunk(x, p))
    return tuple(outs)
```

```python
import functools

import jax
import jax.numpy as jnp
from jax import lax
from jax.experimental import pallas as pl
from jax.experimental.pallas import tpu as pltpu

D_MODEL = 1024
GRID_W = 64
N_HEADS = 16
HEAD_DIM = 64
A_KV_HEADS = 4
A_GROUP = N_HEADS // A_KV_HEADS
ROPE_THETA = 10000.0
AXIS_DIM = HEAD_DIM // 2
ROPE_HALF = AXIS_DIM // 2
NA_WIN_H = 8
NA_WIN_W = 16
NA_BIAS_H = 2 * NA_WIN_H - 1
NA_BIAS_W = 2 * NA_WIN_W - 1
D_FF = 4 * D_MODEL
NORM_EPS = 1e-6
LOG2_E = 1.4426950408889634
NA_SCORE_SCALE = HEAD_DIM ** -0.5 * LOG2_E
HD = N_HEADS * HEAD_DIM
A_QKV = (N_HEADS + 2 * A_KV_HEADS) * HEAD_DIM

ROW_TILE = 512
FLASH_TQ = 256
FLASH_Q_TILES = 2
FLASH_TK = 512
FLASH_HEADS = 4
FLASH_L_ROWS = 16
NA_Q_ROWS = 4
NA_K_ROWS = 12
NA_HEADS = 4
NA_BLOCKS = 4
O_GROUPS = 4
O_GROUP_WIDTH = 256
NA_NQ = NA_Q_ROWS * GRID_W
NA_NK = NA_K_ROWS * GRID_W
FF_CHUNK = 1024
VMEM_LIMIT_BYTES = 56 * 1024 * 1024

BF16 = jnp.bfloat16
F32 = jnp.float32
F8 = jnp.float8_e4m3fn
F8_MAX = 448.0
QK_DEPTH = 4 * HEAD_DIM


def _params(n_axes):
    return pltpu.CompilerParams(
        dimension_semantics=("arbitrary",) * n_axes, vmem_limit_bytes=VMEM_LIMIT_BYTES)


def _resident(shape):
    zeros = (0,) * len(shape)
    return pl.BlockSpec(shape, lambda *_: zeros, pipeline_mode=pl.Buffered(1))


def _split_f8(x):
    x = jnp.clip(x, -F8_MAX, F8_MAX)
    hi = x.astype(F8)
    lo = (x - hi.astype(F32)).astype(F8)
    return hi, lo


def _rms(x, gain):
    ms = jnp.mean(x * x, axis=-1, keepdims=True)
    return x * lax.rsqrt(ms + NORM_EPS) * gain


def _qkv_a_kernel(x_ref, g_ref, wt_ref, qg_ref, kg_ref, cos_ref, sin_ref,
                  q_ref, k_ref, v_ref, s_ref, *, tm):
    h = _rms(x_ref[0], g_ref[...]).astype(BF16)
    qkv_t = lax.dot_general(wt_ref[...], h, (((1,), (1,)), ((), ())),
                            preferred_element_type=F32)
    n_slots = N_HEADS + 2 * A_KV_HEADS
    s_ref[...] = qkv_t.reshape(n_slots, HEAD_DIM, tm)

    def norm_rope(lo, hi, gain_ref, store):
        t = s_ref[lo:hi]
        r = lax.rsqrt(jnp.sum(t * t, axis=1, keepdims=True) * (1.0 / HEAD_DIM) + NORM_EPS)
        for a in range(2):
            d1 = a * AXIS_DIM
            d2 = d1 + ROPE_HALF
            x1 = s_ref[lo:hi, d1:d1 + ROPE_HALF, :] * r * gain_ref[d1:d1 + ROPE_HALF, :]
            x2 = s_ref[lo:hi, d2:d2 + ROPE_HALF, :] * r * gain_ref[d2:d2 + ROPE_HALF, :]
            c = cos_ref[a * ROPE_HALF:(a + 1) * ROPE_HALF, :]
            s = sin_ref[a * ROPE_HALF:(a + 1) * ROPE_HALF, :]
            store(d1, x1 * c - x2 * s)
            store(d2, x2 * c + x1 * s)

    def store_q(d, val):
        s_ref[0:N_HEADS, d:d + ROPE_HALF, :] = val

    def store_k(d, val):
        s_ref[N_HEADS:N_HEADS + A_KV_HEADS, d:d + ROPE_HALF, :] = val

    norm_rope(0, N_HEADS, qg_ref, store_q)
    norm_rope(N_HEADS, N_HEADS + A_KV_HEADS, kg_ref, store_k)
    q_hi, q_lo = _split_f8(s_ref[0:N_HEADS])
    for part, val in enumerate((q_hi, q_hi, q_lo, q_lo)):
        q_ref[0, :, part * HEAD_DIM:(part + 1) * HEAD_DIM, :] = val
    for g in range(A_KV_HEADS):
        k_hi, k_lo = _split_f8(s_ref[N_HEADS + g])
        k_hi, k_lo = k_hi.astype(F32), k_lo.astype(F32)
        k_ext_t = jnp.concatenate([k_hi, k_lo, k_hi, k_lo], axis=0)
        k_ref[0, g] = k_ext_t.T.astype(F8)
        for c in range(tm // FLASH_TK):
            v_ref[0, g, c] = s_ref[N_HEADS + A_KV_HEADS + g, :,
                                   c * FLASH_TK:(c + 1) * FLASH_TK].astype(BF16)


def _qkv_a(x, gain, w_t, q_gain, k_gain, cos_t, sin_t):
    b, s, _ = x.shape
    tm = ROW_TILE
    n_slots = N_HEADS + 2 * A_KV_HEADS
    return pl.pallas_call(
        functools.partial(_qkv_a_kernel, tm=tm),
        grid=(b, s // tm),
        in_specs=[
            pl.BlockSpec((1, tm, D_MODEL), lambda bi, i: (bi, i, 0)),
            _resident((1, D_MODEL)),
            _resident((A_QKV, D_MODEL)),
            _resident((HEAD_DIM, tm)),
            _resident((HEAD_DIM, tm)),
            pl.BlockSpec((AXIS_DIM, tm), lambda bi, i: (0, i)),
            pl.BlockSpec((AXIS_DIM, tm), lambda bi, i: (0, i)),
        ],
        out_specs=[
            pl.BlockSpec((1, N_HEADS, QK_DEPTH, tm), lambda bi, i: (bi, 0, 0, i)),
            pl.BlockSpec((1, A_KV_HEADS, tm, QK_DEPTH), lambda bi, i: (bi, 0, i, 0)),
            pl.BlockSpec((1, A_KV_HEADS, tm // FLASH_TK, HEAD_DIM, FLASH_TK),
                         lambda bi, i: (bi, 0, i, 0, 0)),
        ],
        out_shape=[
            jax.ShapeDtypeStruct((b, N_HEADS, QK_DEPTH, s), F8),
            jax.ShapeDtypeStruct((b, A_KV_HEADS, s, QK_DEPTH), F8),
            jax.ShapeDtypeStruct((b, A_KV_HEADS, s // FLASH_TK, HEAD_DIM, FLASH_TK), BF16),
        ],
        scratch_shapes=[pltpu.VMEM((n_slots, HEAD_DIM, tm), F32)],
        compiler_params=_params(2),
        name="qkv_a",
    )(x, gain, w_t, q_gain, k_gain, cos_t, sin_t)


def _flash_kernel(q_ref, k_ref, v_ref, qn_ref, kn_ref, o_ref, s_ref, cm_ref, *, n_chunks, tq):
    first = (pl.program_id(0) == 0) & (pl.program_id(1) == 0) & (pl.program_id(2) == 0)

    def scores(c, slot, e, j, q_src=q_ref, k_src=k_ref):
        k_c = k_src[0, 0, c * FLASH_TK:(c + 1) * FLASH_TK, :]
        q_e = q_src[0, e, :, j * tq:(j + 1) * tq]
        s_t = jnp.dot(k_c, q_e, preferred_element_type=F32)
        s_ref[slot, e] = s_t
        cm_ref[slot, e] = jnp.max(s_t, axis=0, keepdims=True)

    ones = jnp.ones((FLASH_L_ROWS, FLASH_TK), BF16)

    def update(c, slot, e, stat):
        m, acc = stat
        s_t = s_ref[slot, e]
        m_new = jnp.maximum(m, cm_ref[slot, e])
        alpha = jnp.exp2(m - m_new)
        p = jnp.exp2(s_t - m_new).astype(BF16)
        v_ext = jnp.concatenate([v_ref[0, 0, c], ones], axis=0)
        acc = alpha * acc + jnp.dot(v_ext, p, preferred_element_type=F32)
        return m_new, acc

    @pl.when(first)
    def _():
        for e in range(FLASH_HEADS):
            scores(0, 0, e, 0)

    for j in range(FLASH_Q_TILES):
        stats = tuple((jnp.full((1, tq), -jnp.inf, F32),
                       jnp.zeros((HEAD_DIM + FLASH_L_ROWS, tq), F32)) for _ in range(FLASH_HEADS))
        for c in range(n_chunks):
            new = []
            for e in range(FLASH_HEADS):
                if c + 1 < n_chunks:
                    scores(c + 1, (c + 1) % 2, e, j)
                elif j + 1 < FLASH_Q_TILES:
                    scores(0, 0, e, j + 1)
                else:
                    scores(0, 0, e, 0, qn_ref, kn_ref)
                new.append(update(c, c % 2, e, stats[e]))
            stats = tuple(new)
        o_t = jnp.concatenate([acc[:HEAD_DIM] / acc[HEAD_DIM:HEAD_DIM + 1] for (_, acc) in stats],
                              axis=0)
        o_ref[0, 0, j * tq:(j + 1) * tq, :] = o_t.T.astype(BF16)


def _flash(q_t, k, v_t):
    b, _, _, s = q_t.shape
    tq = FLASH_TQ
    n_chunks = s // FLASH_TK
    pairs = N_HEADS // FLASH_HEADS
    per_kv = A_GROUP // FLASH_HEADS
    tq_step = tq * FLASH_Q_TILES
    n_q = s // tq_step
    assert n_chunks % 2 == 0

    def following(bi, hp, i):
        flat = jnp.minimum((bi * pairs + hp) * n_q + i + 1, b * pairs * n_q - 1)
        return flat // (pairs * n_q), (flat // n_q) % pairs, flat % n_q

    def q_next(bi, hp, i):
        nb, nh, ni = following(bi, hp, i)
        return nb, nh, 0, ni

    def k_next(bi, hp, i):
        nb, nh, _ = following(bi, hp, i)
        return nb, nh // per_kv, 0, 0

    return pl.pallas_call(
        functools.partial(_flash_kernel, n_chunks=n_chunks, tq=tq),
        grid=(b, pairs, n_q),
        in_specs=[
            pl.BlockSpec((1, FLASH_HEADS, QK_DEPTH, tq_step), lambda bi, hp, i: (bi, hp, 0, i)),
            pl.BlockSpec((1, 1, s, QK_DEPTH), lambda bi, hp, i: (bi, hp // per_kv, 0, 0)),
            pl.BlockSpec((1, 1, n_chunks, HEAD_DIM, FLASH_TK),
                         lambda bi, hp, i: (bi, hp // per_kv, 0, 0, 0)),
            pl.BlockSpec((1, FLASH_HEADS, QK_DEPTH, tq_step), q_next),
            pl.BlockSpec((1, 1, s, QK_DEPTH), k_next),
        ],
        out_specs=pl.BlockSpec((1, 1, tq_step, O_GROUP_WIDTH), lambda bi, hp, i: (bi, hp, i, 0)),
        out_shape=jax.ShapeDtypeStruct((b, O_GROUPS, s, O_GROUP_WIDTH), BF16),
        scratch_shapes=[pltpu.VMEM((2, FLASH_HEADS, FLASH_TK, tq), F32),
                        pltpu.VMEM((2, FLASH_HEADS, 1, tq), F32)],
        compiler_params=_params(3),
        name="flash_a",
    )(q_t, k, v_t, q_t, k)


def _mlp_kernel(x_ref, a_ref, wo_ref, g_ref, win_ref, wout_ref, gf_ref, o_ref, *, final):
    a = jnp.concatenate([a_ref[0, g] for g in range(O_GROUPS)], axis=1)
    x = x_ref[0] + jnp.dot(a, wo_ref[...], preferred_element_type=F32)
    h = _rms(x, g_ref[...]).astype(BF16)
    acc = x
    for f in range(D_FF // FF_CHUNK):
        u = jnp.dot(h, win_ref[:, f * FF_CHUNK:(f + 1) * FF_CHUNK], preferred_element_type=F32)
        u = jnp.maximum(u, 0.0)
        acc = acc + jnp.dot((u * u).astype(BF16), wout_ref[f * FF_CHUNK:(f + 1) * FF_CHUNK, :],
                            preferred_element_type=F32)
    if final:
        acc = _rms(acc, gf_ref[...])
    o_ref[0] = acc


def _mlp(x, attn, w_o, gain, w_in, w_out, gain_final, final):
    b, s, _ = x.shape
    tm = ROW_TILE
    return pl.pallas_call(
        functools.partial(_mlp_kernel, final=final),
        grid=(b, s // tm),
        in_specs=[
            pl.BlockSpec((1, tm, D_MODEL), lambda bi, i: (bi, i, 0)),
            pl.BlockSpec((1, O_GROUPS, tm, O_GROUP_WIDTH), lambda bi, i: (bi, 0, i, 0)),
            _resident((HD, D_MODEL)),
            _resident((1, D_MODEL)),
            _resident((D_MODEL, D_FF)),
            _resident((D_FF, D_MODEL)),
            _resident((1, D_MODEL)),
        ],
        out_specs=pl.BlockSpec((1, tm, D_MODEL), lambda bi, i: (bi, i, 0)),
        out_shape=jax.ShapeDtypeStruct((b, s, D_MODEL), F32),
        compiler_params=_params(2),
        name="mlp_final" if final else "mlp",
    )(x, attn, w_o, gain, w_in, w_out, gain_final)


def _qkv_b_kernel(x_ref, g_ref, wqv_t_ref, wk_ref, q_ref, k_ref, v_ref, *, tm):
    h = _rms(x_ref[0], g_ref[...]).astype(BF16)
    k = jnp.dot(h, wk_ref[...], preferred_element_type=F32).astype(BF16)
    width = NA_HEADS * HEAD_DIM
    for g in range(N_HEADS // NA_HEADS):
        k_ref[0, g] = k[:, g * width:(g + 1) * width]
    nt = (((1,), (1,)), ((), ()))
    q_t = lax.dot_general(wqv_t_ref[0:HD, :], h, nt, preferred_element_type=F32)
    q_ref[0, 0] = (q_t * NA_SCORE_SCALE).astype(BF16)
    v_t = lax.dot_general(wqv_t_ref[HD:2 * HD, :], h, nt, preferred_element_type=F32)
    for hd in range(N_HEADS):
        for c in range(tm // NA_NQ):
            v_ref[0, hd, c] = v_t[hd * HEAD_DIM:(hd + 1) * HEAD_DIM,
                                  c * NA_NQ:(c + 1) * NA_NQ].astype(BF16)


def _qkv_b(x, gain, wqv_t, wk):
    b, s, _ = x.shape
    tm = 2 * NA_NQ
    groups = N_HEADS // NA_HEADS
    width = NA_HEADS * HEAD_DIM
    return pl.pallas_call(
        functools.partial(_qkv_b_kernel, tm=tm),
        grid=(b, s // tm),
        in_specs=[
            pl.BlockSpec((1, tm, D_MODEL), lambda bi, i: (bi, i, 0)),
            _resident((1, D_MODEL)),
            _resident((2 * HD, D_MODEL)),
            _resident((D_MODEL, HD)),
        ],
        out_specs=[
            pl.BlockSpec((1, 1, HD, tm), lambda bi, i: (bi, i, 0, 0)),
            pl.BlockSpec((1, groups, tm, width), lambda bi, i: (bi, 0, i, 0)),
            pl.BlockSpec((1, N_HEADS, tm // NA_NQ, HEAD_DIM, NA_NQ), lambda bi, i: (bi, 0, i, 0, 0)),
        ],
        out_shape=[
            jax.ShapeDtypeStruct((b, s // tm, HD, tm), BF16),
            jax.ShapeDtypeStruct((b, groups, s, width), BF16),
            jax.ShapeDtypeStruct((b, N_HEADS, s // NA_NQ, HEAD_DIM, NA_NQ), BF16),
        ],
        compiler_params=_params(2),
        name="qkv_b",
    )(x, gain, wqv_t, wk)


def _natten_kernel(q_ref, qn_ref, k_ref, v_ref, b_ref, o_ref, s_ref, cm_ref, *, rows):
    u = pl.program_id(2)
    n_blocks = rows // NA_Q_ROWS
    width = NA_HEADS * HEAD_DIM
    row_id = lax.broadcasted_iota(jnp.int32, (width, NA_NQ), 0)
    ones = jnp.ones((FLASH_L_ROWS, NA_NK), BF16)

    def key_base(blk):
        return jnp.clip(NA_Q_ROWS * blk - NA_WIN_H // 2, 0, rows - NA_K_ROWS)

    def scores(blk, q_blk, slot, e):
        variant = jnp.where(blk == 0, 0, jnp.where(blk == n_blocks - 1, 2, 1))
        start = pl.multiple_of(key_base(blk) * GRID_W, NA_NQ)
        k_w = k_ref[0, 0, pl.ds(start, NA_NK), :]
        mine = (row_id >= e * HEAD_DIM) & (row_id < (e + 1) * HEAD_DIM)
        q_e = jnp.where(mine, q_blk, jnp.zeros_like(q_blk))
        s_t = jnp.dot(k_w, q_e, preferred_element_type=F32) + b_ref[variant, e]
        s_ref[slot, e] = s_t
        cm_ref[slot, e] = jnp.max(s_t, axis=0, keepdims=True)

    def attend(blk, slot, e):
        chunk0 = key_base(blk) // NA_Q_ROWS
        s_t = s_ref[slot, e]
        p = jnp.exp2(s_t - cm_ref[slot, e]).astype(BF16)
        v_t = [v_ref[0, e, chunk0 + j] for j in range(NA_K_ROWS // NA_Q_ROWS)]
        v_ext = jnp.concatenate([jnp.concatenate(v_t, axis=1), ones], axis=0)
        acc = jnp.dot(v_ext, p, preferred_element_type=F32)
        return acc[:HEAD_DIM] / acc[HEAD_DIM:HEAD_DIM + 1]

    @pl.when(u == 0)
    def _():
        for e in range(NA_HEADS):
            scores(0, q_ref[0, 0, :, 0:NA_NQ], 0, e)

    def half(blk, slot, blk_next, q_next, row0):
        outs = []
        for e in range(NA_HEADS):
            scores(blk_next, q_next, 1 - slot, e)
            outs.append(attend(blk, slot, e))
        o_ref[0, 0, row0:row0 + NA_NQ, :] = jnp.concatenate(outs, axis=0).T.astype(BF16)

    def q_block(n):
        return q_ref[0, n // 2, :, (n % 2) * NA_NQ:(n % 2 + 1) * NA_NQ]

    for n in range(NA_BLOCKS - 1):
        half(NA_BLOCKS * u + n, n % 2, NA_BLOCKS * u + n + 1, q_block(n + 1), n * NA_NQ)
    u_next = jnp.minimum(u + 1, n_blocks // NA_BLOCKS - 1)
    half(NA_BLOCKS * u + NA_BLOCKS - 1, (NA_BLOCKS - 1) % 2, NA_BLOCKS * u_next,
         qn_ref[0, 0, :, 0:NA_NQ], (NA_BLOCKS - 1) * NA_NQ)


def _natten(q_t, k, v_t, bias):
    b, _, s, _ = k.shape
    rows = s // GRID_W
    n_blocks = rows // NA_Q_ROWS
    groups = N_HEADS // NA_HEADS
    width = NA_HEADS * HEAD_DIM
    steps = n_blocks // NA_BLOCKS
    tiles = NA_BLOCKS // 2
    assert rows >= NA_K_ROWS + NA_Q_ROWS
    return pl.pallas_call(
        functools.partial(_natten_kernel, rows=rows),
        grid=(groups, b, steps),
        in_specs=[
            pl.BlockSpec((1, tiles, width, 2 * NA_NQ), lambda hg, bi, u: (bi, u, hg, 0)),
            pl.BlockSpec((1, 1, width, 2 * NA_NQ),
                         lambda hg, bi, u: (bi, tiles * jnp.minimum(u + 1, steps - 1), hg, 0)),
            pl.BlockSpec((1, 1, s, width), lambda hg, bi, u: (bi, hg, 0, 0)),
            pl.BlockSpec((1, NA_HEADS, s // NA_NQ, HEAD_DIM, NA_NQ),
                         lambda hg, bi, u: (bi, hg, 0, 0, 0)),
            pl.BlockSpec((3, NA_HEADS, NA_NK, NA_NQ), lambda hg, bi, u: (0, hg, 0, 0),
                         pipeline_mode=pl.Buffered(1)),
        ],
        out_specs=pl.BlockSpec((1, 1, NA_BLOCKS * NA_NQ, width), lambda hg, bi, u: (bi, hg, u, 0)),
        out_shape=jax.ShapeDtypeStruct((b, groups, s, width), BF16),
        scratch_shapes=[pltpu.VMEM((2, NA_HEADS, NA_NK, NA_NQ), F32),
                        pltpu.VMEM((2, NA_HEADS, 1, NA_NQ), F32)],
        compiler_params=_params(3),
        name="natten_b",
    )(q_t, q_t, k, v_t, bias)


def _rope_tables_t(seq):
    t = jnp.arange(seq)
    row = (t // GRID_W).astype(F32)
    col = (t % GRID_W).astype(F32)
    inv = ROPE_THETA ** (-jnp.arange(0, AXIS_DIM, 2, dtype=F32) / AXIS_DIM)
    ang = jnp.concatenate([inv[:, None] * row[None, :], inv[:, None] * col[None, :]], axis=0)
    return jnp.cos(ang), jnp.sin(ang)


NA_ROW_OFFSET = (NA_WIN_H - 1, NA_WIN_H - 1 - NA_WIN_H // 2,
                 NA_WIN_H - 1 - (NA_K_ROWS - NA_Q_ROWS))


def _na_window_start(variant, a):
    return (0, a, NA_K_ROWS - NA_WIN_H)[variant]


def _natten_bias_kernel(even_ref, odd_ref, mask_ref, o_ref):
    lanes = 2 * GRID_W
    lane = lax.broadcasted_iota(jnp.int32, (GRID_W, lanes), 1)
    masked = jnp.full((GRID_W, lanes), -jnp.inf, F32)
    for variant in range(3):
        for i in range(NA_K_ROWS):
            for pair in range(NA_Q_ROWS // 2):
                rows_a = (2 * pair, 2 * pair + 1)
                inside = [0 <= i - _na_window_start(variant, a) < NA_WIN_H for a in rows_a]
                blk = masked
                if any(inside):
                    base = jnp.zeros((1, lanes), F32)
                    for a, ok, src in zip(rows_a, inside, (even_ref, odd_ref)):
                        if ok:
                            d = i - a + NA_ROW_OFFSET[variant]
                            base = base + src[0, d:d + 1, :]
                    blk = pltpu.roll(jnp.broadcast_to(base, (GRID_W, lanes)), 0, 1,
                                     stride=1, stride_axis=0) + mask_ref[...]
                    if not inside[0]:
                        blk = jnp.where(lane < GRID_W, masked, blk)
                    if not inside[1]:
                        blk = jnp.where(lane >= GRID_W, masked, blk)
                o_ref[variant, 0, i * GRID_W:(i + 1) * GRID_W, pair * lanes:(pair + 1) * lanes] = blk


def _natten_bias(rel_bias):
    table = rel_bias.astype(F32) * LOG2_E
    lanes = 2 * GRID_W
    m = jnp.arange(lanes)
    reach = NA_WIN_W - 1
    col_even = jnp.where(m < GRID_W, reach - m, reach - (m - lanes))
    col_odd = reach - (m - GRID_W)

    def place(col, keep):
        vals = jnp.take(table, jnp.clip(col, 0, NA_BIAS_W - 1), axis=2)
        vals = jnp.where((keep & (col >= 0) & (col < NA_BIAS_W))[None, None, :], vals, 0.0)
        return jnp.pad(vals, ((0, 0), (0, 1), (0, 0)))

    even = place(col_even, (m <= reach) | (m >= lanes - reach))
    odd = place(col_odd, (m >= GRID_W - reach) & (m <= GRID_W + reach))
    c = jnp.arange(GRID_W)[None, :]
    kc = jnp.arange(GRID_W)[:, None]
    c0 = jnp.clip(c - NA_WIN_W // 2, 0, GRID_W - NA_WIN_W)
    valid_c = (kc >= c0) & (kc < c0 + NA_WIN_W)
    mask = jnp.where(jnp.concatenate([valid_c, valid_c], axis=1), 0.0, -jnp.inf).astype(F32)
    return pl.pallas_call(
        _natten_bias_kernel,
        grid=(N_HEADS,),
        in_specs=[
            pl.BlockSpec((1, NA_BIAS_H + 1, lanes), lambda h: (h, 0, 0)),
            pl.BlockSpec((1, NA_BIAS_H + 1, lanes), lambda h: (h, 0, 0)),
            pl.BlockSpec((GRID_W, lanes), lambda h: (0, 0)),
        ],
        out_specs=pl.BlockSpec((3, 1, NA_NK, NA_NQ), lambda h: (0, h, 0, 0)),
        out_shape=jax.ShapeDtypeStruct((3, N_HEADS, NA_NK, NA_NQ), F32),
        compiler_params=_params(1),
        name="natten_bias",
    )(even, odd, mask)


def _trunk(x, p):
    _, s, _ = x.shape
    depth = p["norm_mix"].shape[0]
    for i in range(depth):
        j = i // 2
        g_mix = p["norm_mix"][i][None, :]
        if i % 2 == 0:
            q_t, k, v_t = _qkv_a(x, g_mix, p["a_w_qkv_t"][j], p["a_q_gain"][j], p["a_k_gain"][j],
                                 p["cos_t"], p["sin_t"])
            o = _flash(q_t, k, v_t)
            w_o = p["a_w_o"][j]
        else:
            q_t, k, v_t = _qkv_b(x, g_mix, p["b_w_qv_t"][j], p["b_w_k"][j])
            o = _natten(q_t, k, v_t, p["b_bias"][j])
            w_o = p["b_w_o"][j]
        x = _mlp(x, o, w_o, p["norm_mlp"][i][None, :], p["mlp_w_in"][i], p["mlp_w_out"][i],
                 p["norm_final"][None, :], final=(i == depth - 1))
    return x


def kernel(x_prompt, x_sample, norm_mix, norm_mlp, norm_final, a_w_qkv, a_q_norm, a_k_norm, a_w_o,
           b_w_qkv, b_rel_bias, b_w_o, mlp_w_in, mlp_w_out):
    scale = HEAD_DIM ** -0.5 * LOG2_E
    q_bound = jnp.max(jnp.abs(a_q_norm), axis=1) * scale
    k_bound = jnp.max(jnp.abs(a_k_norm), axis=1)
    ratio = k_bound / q_bound
    ratio = jnp.where(jnp.isfinite(ratio) & (ratio > 0), ratio, 1.0)
    balance = jnp.exp2(jnp.round(0.5 * jnp.log2(ratio)))[:, None]
    shared = {
        "norm_mix": norm_mix, "norm_mlp": norm_mlp, "norm_final": norm_final,
        "a_w_qkv_t": jnp.swapaxes(a_w_qkv, 1, 2).astype(BF16),
        "a_q_gain": jnp.broadcast_to((a_q_norm * scale * balance)[:, :, None],
                                     a_q_norm.shape + (ROW_TILE,)),
        "a_k_gain": jnp.broadcast_to((a_k_norm / balance)[:, :, None],
                                     a_k_norm.shape + (ROW_TILE,)),
        "a_w_o": a_w_o.astype(BF16),
        "b_w_qv_t": jnp.swapaxes(jnp.concatenate([b_w_qkv[:, :, :HD], b_w_qkv[:, :, 2 * HD:]],
                                                 axis=2), 1, 2).astype(BF16),
        "b_w_k": b_w_qkv[:, :, HD:2 * HD].astype(BF16),
        "b_w_o": b_w_o.astype(BF16),
        "mlp_w_in": mlp_w_in.astype(BF16),
        "mlp_w_out": mlp_w_out.astype(BF16),
        "b_bias": [_natten_bias(rb) for rb in b_rel_bias],
    }
    outs = []
    for x in (x_prompt, x_sample):
        cos_t, sin_t = _rope_tables_t(x.shape[1])
        outs.append(_trunk(x, dict(shared, cos_t=cos_t, sin_t=sin_t)))
    return tuple(outs)
```

```python
import functools

import jax
import jax.numpy as jnp
from jax import lax
from jax.experimental import pallas as pl
from jax.experimental.pallas import tpu as pltpu

D_MODEL = 1024
GRID_W = 64
N_HEADS = 16
HEAD_DIM = 64
A_KV_HEADS = 4
A_GROUP = N_HEADS // A_KV_HEADS
ROPE_THETA = 10000.0
AXIS_DIM = HEAD_DIM // 2
ROPE_HALF = AXIS_DIM // 2
NA_WIN_H = 8
NA_WIN_W = 16
NA_BIAS_H = 2 * NA_WIN_H - 1
NA_BIAS_W = 2 * NA_WIN_W - 1
D_FF = 4 * D_MODEL
NORM_EPS = 1e-6
LOG2_E = 1.4426950408889634
NA_SCORE_SCALE = HEAD_DIM ** -0.5 * LOG2_E
HD = N_HEADS * HEAD_DIM
A_QKV = (N_HEADS + 2 * A_KV_HEADS) * HEAD_DIM

ROW_TILE = 512
FLASH_TQ = 256
FLASH_Q_TILES = 2
FLASH_TK = 512
FLASH_HEADS = 4
FLASH_L_ROWS = 16
NA_Q_ROWS = 4
NA_K_ROWS = 12
NA_HEADS = 4
NA_BLOCKS = 8
O_GROUPS = 4
O_GROUP_WIDTH = 256
NA_NQ = NA_Q_ROWS * GRID_W
NA_NK = NA_K_ROWS * GRID_W
FF_CHUNK = 1024
VMEM_LIMIT_BYTES = 56 * 1024 * 1024

BF16 = jnp.bfloat16
F32 = jnp.float32
F8 = jnp.float8_e4m3fn
F8_MAX = 448.0
QK_DEPTH = 4 * HEAD_DIM


def _params(n_axes):
    return pltpu.CompilerParams(
        dimension_semantics=("arbitrary",) * n_axes, vmem_limit_bytes=VMEM_LIMIT_BYTES)


def _resident(shape):
    zeros = (0,) * len(shape)
    return pl.BlockSpec(shape, lambda *_: zeros, pipeline_mode=pl.Buffered(1))


def _split_f8(x):
    x = jnp.clip(x, -F8_MAX, F8_MAX)
    hi = x.astype(F8)
    lo = (x - hi.astype(F32)).astype(F8)
    return hi, lo


def _rms(x, gain):
    ms = jnp.mean(x * x, axis=-1, keepdims=True)
    return x * lax.rsqrt(ms + NORM_EPS) * gain


def _qkv_a_kernel(x_ref, g_ref, wt_ref, qg_ref, kg_ref, cos_ref, sin_ref,
                  q_ref, k_ref, v_ref, s_ref, *, tm):
    h = _rms(x_ref[0], g_ref[...]).astype(BF16)
    qkv_t = lax.dot_general(wt_ref[...], h, (((1,), (1,)), ((), ())),
                            preferred_element_type=F32)
    n_slots = N_HEADS + 2 * A_KV_HEADS
    s_ref[...] = qkv_t.reshape(n_slots, HEAD_DIM, tm)

    def norm_rope(lo, hi, gain_ref, store):
        t = s_ref[lo:hi]
        r = lax.rsqrt(jnp.sum(t * t, axis=1, keepdims=True) * (1.0 / HEAD_DIM) + NORM_EPS)
        for a in range(2):
            d1 = a * AXIS_DIM
            d2 = d1 + ROPE_HALF
            x1 = s_ref[lo:hi, d1:d1 + ROPE_HALF, :] * r * gain_ref[d1:d1 + ROPE_HALF, :]
            x2 = s_ref[lo:hi, d2:d2 + ROPE_HALF, :] * r * gain_ref[d2:d2 + ROPE_HALF, :]
            c = cos_ref[a * ROPE_HALF:(a + 1) * ROPE_HALF, :]
            s = sin_ref[a * ROPE_HALF:(a + 1) * ROPE_HALF, :]
            store(d1, x1 * c - x2 * s)
            store(d2, x2 * c + x1 * s)

    def store_q(d, val):
        s_ref[0:N_HEADS, d:d + ROPE_HALF, :] = val

    def store_k(d, val):
        s_ref[N_HEADS:N_HEADS + A_KV_HEADS, d:d + ROPE_HALF, :] = val

    norm_rope(0, N_HEADS, qg_ref, store_q)
    norm_rope(N_HEADS, N_HEADS + A_KV_HEADS, kg_ref, store_k)
    q_hi, q_lo = _split_f8(s_ref[0:N_HEADS])
    for part, val in enumerate((q_hi, q_hi, q_lo, q_lo)):
        q_ref[0, :, part * HEAD_DIM:(part + 1) * HEAD_DIM, :] = val
    for g in range(A_KV_HEADS):
        k_hi, k_lo = _split_f8(s_ref[N_HEADS + g])
        k_hi, k_lo = k_hi.astype(F32), k_lo.astype(F32)
        k_ext_t = jnp.concatenate([k_hi, k_lo, k_hi, k_lo], axis=0)
        k_ref[0, g] = k_ext_t.T.astype(F8)
        for c in range(tm // FLASH_TK):
            v_ref[0, g, c] = s_ref[N_HEADS + A_KV_HEADS + g, :,
                                   c * FLASH_TK:(c + 1) * FLASH_TK].astype(BF16)


def _qkv_a(x, gain, w_t, q_gain, k_gain, cos_t, sin_t):
    b, s, _ = x.shape
    tm = ROW_TILE
    n_slots = N_HEADS + 2 * A_KV_HEADS
    return pl.pallas_call(
        functools.partial(_qkv_a_kernel, tm=tm),
        grid=(b, s // tm),
        in_specs=[
            pl.BlockSpec((1, tm, D_MODEL), lambda bi, i: (bi, i, 0)),
            _resident((1, D_MODEL)),
            _resident((A_QKV, D_MODEL)),
            _resident((HEAD_DIM, tm)),
            _resident((HEAD_DIM, tm)),
            pl.BlockSpec((AXIS_DIM, tm), lambda bi, i: (0, i)),
            pl.BlockSpec((AXIS_DIM, tm), lambda bi, i: (0, i)),
        ],
        out_specs=[
            pl.BlockSpec((1, N_HEADS, QK_DEPTH, tm), lambda bi, i: (bi, 0, 0, i)),
            pl.BlockSpec((1, A_KV_HEADS, tm, QK_DEPTH), lambda bi, i: (bi, 0, i, 0)),
            pl.BlockSpec((1, A_KV_HEADS, tm // FLASH_TK, HEAD_DIM, FLASH_TK),
                         lambda bi, i: (bi, 0, i, 0, 0)),
        ],
        out_shape=[
            jax.ShapeDtypeStruct((b, N_HEADS, QK_DEPTH, s), F8),
            jax.ShapeDtypeStruct((b, A_KV_HEADS, s, QK_DEPTH), F8),
            jax.ShapeDtypeStruct((b, A_KV_HEADS, s // FLASH_TK, HEAD_DIM, FLASH_TK), BF16),
        ],
        scratch_shapes=[pltpu.VMEM((n_slots, HEAD_DIM, tm), F32)],
        compiler_params=_params(2),
        name="qkv_a",
    )(x, gain, w_t, q_gain, k_gain, cos_t, sin_t)


def _flash_kernel(q_ref, k_ref, v_ref, qn_ref, kn_ref, o_ref, s_ref, cm_ref, *, n_chunks, tq):
    first = (pl.program_id(0) == 0) & (pl.program_id(1) == 0) & (pl.program_id(2) == 0)

    def scores(c, slot, e, j, q_src=q_ref, k_src=k_ref):
        k_c = k_src[0, 0, c * FLASH_TK:(c + 1) * FLASH_TK, :]
        q_e = q_src[0, e, :, j * tq:(j + 1) * tq]
        s_t = jnp.dot(k_c, q_e, preferred_element_type=F32)
        s_ref[slot, e] = s_t
        cm_ref[slot, e] = jnp.max(s_t, axis=0, keepdims=True)

    ones = jnp.ones((FLASH_L_ROWS, FLASH_TK), BF16)

    def update(c, slot, e, stat):
        m, acc = stat
        s_t = s_ref[slot, e]
        m_new = jnp.maximum(m, cm_ref[slot, e])
        alpha = jnp.exp2(m - m_new)
        p = jnp.exp2(s_t - m_new).astype(BF16)
        v_ext = jnp.concatenate([v_ref[0, 0, c], ones], axis=0)
        acc = alpha * acc + jnp.dot(v_ext, p, preferred_element_type=F32)
        return m_new, acc

    @pl.when(first)
    def _():
        for e in range(FLASH_HEADS):
            scores(0, 0, e, 0)

    for j in range(FLASH_Q_TILES):
        stats = tuple((jnp.full((1, tq), -jnp.inf, F32),
                       jnp.zeros((HEAD_DIM + FLASH_L_ROWS, tq), F32)) for _ in range(FLASH_HEADS))
        for c in range(n_chunks):
            new = []
            for e in range(FLASH_HEADS):
                if c + 1 < n_chunks:
                    scores(c + 1, (c + 1) % 2, e, j)
                elif j + 1 < FLASH_Q_TILES:
                    scores(0, 0, e, j + 1)
                else:
                    scores(0, 0, e, 0, qn_ref, kn_ref)
                new.append(update(c, c % 2, e, stats[e]))
            stats = tuple(new)
        o_t = jnp.concatenate([acc[:HEAD_DIM] / acc[HEAD_DIM:HEAD_DIM + 1] for (_, acc) in stats],
                              axis=0)
        o_ref[0, 0, j * tq:(j + 1) * tq, :] = o_t.T.astype(BF16)


def _flash(q_t, k, v_t):
    b, _, _, s = q_t.shape
    tq = FLASH_TQ
    n_chunks = s // FLASH_TK
    pairs = N_HEADS // FLASH_HEADS
    per_kv = A_GROUP // FLASH_HEADS
    tq_step = tq * FLASH_Q_TILES
    n_q = s // tq_step
    assert n_chunks % 2 == 0

    def following(bi, hp, i):
        flat = jnp.minimum((bi * pairs + hp) * n_q + i + 1, b * pairs * n_q - 1)
        return flat // (pairs * n_q), (flat // n_q) % pairs, flat % n_q

    def q_next(bi, hp, i):
        nb, nh, ni = following(bi, hp, i)
        return nb, nh, 0, ni

    def k_next(bi, hp, i):
        nb, nh, _ = following(bi, hp, i)
        return nb, nh // per_kv, 0, 0

    return pl.pallas_call(
        functools.partial(_flash_kernel, n_chunks=n_chunks, tq=tq),
        grid=(b, pairs, n_q),
        in_specs=[
            pl.BlockSpec((1, FLASH_HEADS, QK_DEPTH, tq_step), lambda bi, hp, i: (bi, hp, 0, i)),
            pl.BlockSpec((1, 1, s, QK_DEPTH), lambda bi, hp, i: (bi, hp // per_kv, 0, 0)),
            pl.BlockSpec((1, 1, n_chunks, HEAD_DIM, FLASH_TK),
                         lambda bi, hp, i: (bi, hp // per_kv, 0, 0, 0)),
            pl.BlockSpec((1, FLASH_HEADS, QK_DEPTH, tq_step), q_next),
            pl.BlockSpec((1, 1, s, QK_DEPTH), k_next),
        ],
        out_specs=pl.BlockSpec((1, 1, tq_step, O_GROUP_WIDTH), lambda bi, hp, i: (bi, hp, i, 0)),
        out_shape=jax.ShapeDtypeStruct((b, O_GROUPS, s, O_GROUP_WIDTH), BF16),
        scratch_shapes=[pltpu.VMEM((2, FLASH_HEADS, FLASH_TK, tq), F32),
                        pltpu.VMEM((2, FLASH_HEADS, 1, tq), F32)],
        compiler_params=_params(3),
        name="flash_a",
    )(q_t, k, v_t, q_t, k)


def _mlp_kernel(x_ref, a_ref, wo_ref, g_ref, win_ref, wout_ref, gf_ref, o_ref, *, final):
    a = jnp.concatenate([a_ref[0, g] for g in range(O_GROUPS)], axis=1)
    x = x_ref[0] + jnp.dot(a, wo_ref[...], preferred_element_type=F32)
    h = _rms(x, g_ref[...]).astype(BF16)
    acc = x
    for f in range(D_FF // FF_CHUNK):
        u = jnp.dot(h, win_ref[:, f * FF_CHUNK:(f + 1) * FF_CHUNK], preferred_element_type=F32)
        u = jnp.maximum(u, 0.0)
        acc = acc + jnp.dot((u * u).astype(BF16), wout_ref[f * FF_CHUNK:(f + 1) * FF_CHUNK, :],
                            preferred_element_type=F32)
    if final:
        acc = _rms(acc, gf_ref[...])
    o_ref[0] = acc


def _mlp(x, attn, w_o, gain, w_in, w_out, gain_final, final):
    b, s, _ = x.shape
    tm = ROW_TILE
    return pl.pallas_call(
        functools.partial(_mlp_kernel, final=final),
        grid=(b, s // tm),
        in_specs=[
            pl.BlockSpec((1, tm, D_MODEL), lambda bi, i: (bi, i, 0)),
            pl.BlockSpec((1, O_GROUPS, tm, O_GROUP_WIDTH), lambda bi, i: (bi, 0, i, 0)),
            _resident((HD, D_MODEL)),
            _resident((1, D_MODEL)),
            _resident((D_MODEL, D_FF)),
            _resident((D_FF, D_MODEL)),
            _resident((1, D_MODEL)),
        ],
        out_specs=pl.BlockSpec((1, tm, D_MODEL), lambda bi, i: (bi, i, 0)),
        out_shape=jax.ShapeDtypeStruct((b, s, D_MODEL), F32),
        compiler_params=_params(2),
        name="mlp_final" if final else "mlp",
    )(x, attn, w_o, gain, w_in, w_out, gain_final)


def _qkv_b_kernel(x_ref, g_ref, wqv_t_ref, wk_ref, q_ref, k_ref, v_ref, *, tm):
    h = _rms(x_ref[0], g_ref[...]).astype(BF16)
    k = jnp.dot(h, wk_ref[...], preferred_element_type=F32).astype(BF16)
    width = NA_HEADS * HEAD_DIM
    for g in range(N_HEADS // NA_HEADS):
        k_ref[0, g] = k[:, g * width:(g + 1) * width]
    nt = (((1,), (1,)), ((), ()))
    q_t = lax.dot_general(wqv_t_ref[0:HD, :], h, nt, preferred_element_type=F32)
    q_ref[0, 0] = (q_t * NA_SCORE_SCALE).astype(BF16)
    v_t = lax.dot_general(wqv_t_ref[HD:2 * HD, :], h, nt, preferred_element_type=F32)
    for hd in range(N_HEADS):
        for c in range(tm // NA_NQ):
            v_ref[0, hd, c] = v_t[hd * HEAD_DIM:(hd + 1) * HEAD_DIM,
                                  c * NA_NQ:(c + 1) * NA_NQ].astype(BF16)


def _qkv_b(x, gain, wqv_t, wk):
    b, s, _ = x.shape
    tm = 2 * NA_NQ
    groups = N_HEADS // NA_HEADS
    width = NA_HEADS * HEAD_DIM
    return pl.pallas_call(
        functools.partial(_qkv_b_kernel, tm=tm),
        grid=(b, s // tm),
        in_specs=[
            pl.BlockSpec((1, tm, D_MODEL), lambda bi, i: (bi, i, 0)),
            _resident((1, D_MODEL)),
            _resident((2 * HD, D_MODEL)),
            _resident((D_MODEL, HD)),
        ],
        out_specs=[
            pl.BlockSpec((1, 1, HD, tm), lambda bi, i: (bi, i, 0, 0)),
            pl.BlockSpec((1, groups, tm, width), lambda bi, i: (bi, 0, i, 0)),
            pl.BlockSpec((1, N_HEADS, tm // NA_NQ, HEAD_DIM, NA_NQ), lambda bi, i: (bi, 0, i, 0, 0)),
        ],
        out_shape=[
            jax.ShapeDtypeStruct((b, s // tm, HD, tm), BF16),
            jax.ShapeDtypeStruct((b, groups, s, width), BF16),
            jax.ShapeDtypeStruct((b, N_HEADS, s // NA_NQ, HEAD_DIM, NA_NQ), BF16),
        ],
        compiler_params=_params(2),
        name="qkv_b",
    )(x, gain, wqv_t, wk)


def _natten_kernel(q_ref, qn_ref, k_ref, v_ref, b_ref, o_ref, s_ref, cm_ref, *, rows):
    u = pl.program_id(2)
    n_blocks = rows // NA_Q_ROWS
    width = NA_HEADS * HEAD_DIM
    row_id = lax.broadcasted_iota(jnp.int32, (width, NA_NQ), 0)
    ones = jnp.ones((FLASH_L_ROWS, NA_NK), BF16)

    def key_base(blk):
        return jnp.clip(NA_Q_ROWS * blk - NA_WIN_H // 2, 0, rows - NA_K_ROWS)

    def scores(blk, q_blk, slot, e):
        variant = jnp.where(blk == 0, 0, jnp.where(blk == n_blocks - 1, 2, 1))
        start = pl.multiple_of(key_base(blk) * GRID_W, NA_NQ)
        k_w = k_ref[0, 0, pl.ds(start, NA_NK), :]
        mine = (row_id >= e * HEAD_DIM) & (row_id < (e + 1) * HEAD_DIM)
        q_e = jnp.where(mine, q_blk, jnp.zeros_like(q_blk))
        s_t = jnp.dot(k_w, q_e, preferred_element_type=F32) + b_ref[variant, e]
        s_ref[slot, e] = s_t
        cm_ref[slot, e] = jnp.max(s_t, axis=0, keepdims=True)

    def attend(blk, slot, e):
        chunk0 = key_base(blk) // NA_Q_ROWS
        s_t = s_ref[slot, e]
        p = jnp.exp2(s_t - cm_ref[slot, e]).astype(BF16)
        v_t = [v_ref[0, e, chunk0 + j] for j in range(NA_K_ROWS // NA_Q_ROWS)]
        v_ext = jnp.concatenate([jnp.concatenate(v_t, axis=1), ones], axis=0)
        acc = jnp.dot(v_ext, p, preferred_element_type=F32)
        return acc[:HEAD_DIM] / acc[HEAD_DIM:HEAD_DIM + 1]

    @pl.when(u == 0)
    def _():
        for e in range(NA_HEADS):
            scores(0, q_ref[0, 0, :, 0:NA_NQ], 0, e)

    def half(blk, slot, blk_next, q_next, row0):
        outs = []
        for e in range(NA_HEADS):
            scores(blk_next, q_next, 1 - slot, e)
            outs.append(attend(blk, slot, e))
        o_ref[0, 0, row0:row0 + NA_NQ, :] = jnp.concatenate(outs, axis=0).T.astype(BF16)

    def q_block(n):
        return q_ref[0, n // 2, :, (n % 2) * NA_NQ:(n % 2 + 1) * NA_NQ]

    for n in range(NA_BLOCKS - 1):
        half(NA_BLOCKS * u + n, n % 2, NA_BLOCKS * u + n + 1, q_block(n + 1), n * NA_NQ)
    u_next = jnp.minimum(u + 1, n_blocks // NA_BLOCKS - 1)
    half(NA_BLOCKS * u + NA_BLOCKS - 1, (NA_BLOCKS - 1) % 2, NA_BLOCKS * u_next,
         qn_ref[0, 0, :, 0:NA_NQ], (NA_BLOCKS - 1) * NA_NQ)


def _natten(q_t, k, v_t, bias):
    b, _, s, _ = k.shape
    rows = s // GRID_W
    n_blocks = rows // NA_Q_ROWS
    groups = N_HEADS // NA_HEADS
    width = NA_HEADS * HEAD_DIM
    steps = n_blocks // NA_BLOCKS
    tiles = NA_BLOCKS // 2
    assert rows >= NA_K_ROWS + NA_Q_ROWS
    return pl.pallas_call(
        functools.partial(_natten_kernel, rows=rows),
        grid=(groups, b, steps),
        in_specs=[
            pl.BlockSpec((1, tiles, width, 2 * NA_NQ), lambda hg, bi, u: (bi, u, hg, 0)),
            pl.BlockSpec((1, 1, width, 2 * NA_NQ),
                         lambda hg, bi, u: (bi, tiles * jnp.minimum(u + 1, steps - 1), hg, 0)),
            pl.BlockSpec((1, 1, s, width), lambda hg, bi, u: (bi, hg, 0, 0)),
            pl.BlockSpec((1, NA_HEADS, s // NA_NQ, HEAD_DIM, NA_NQ),
                         lambda hg, bi, u: (bi, hg, 0, 0, 0)),
            pl.BlockSpec((3, NA_HEADS, NA_NK, NA_NQ), lambda hg, bi, u: (0, hg, 0, 0),
                         pipeline_mode=pl.Buffered(1)),
        ],
        out_specs=pl.BlockSpec((1, 1, NA_BLOCKS * NA_NQ, width), lambda hg, bi, u: (bi, hg, u, 0)),
        out_shape=jax.ShapeDtypeStruct((b, groups, s, width), BF16),
        scratch_shapes=[pltpu.VMEM((2, NA_HEADS, NA_NK, NA_NQ), F32),
                        pltpu.VMEM((2, NA_HEADS, 1, NA_NQ), F32)],
        compiler_params=_params(3),
        name="natten_b",
    )(q_t, q_t, k, v_t, bias)


def _rope_tables_t(seq):
    t = jnp.arange(seq)
    row = (t // GRID_W).astype(F32)
    col = (t % GRID_W).astype(F32)
    inv = ROPE_THETA ** (-jnp.arange(0, AXIS_DIM, 2, dtype=F32) / AXIS_DIM)
    ang = jnp.concatenate([inv[:, None] * row[None, :], inv[:, None] * col[None, :]], axis=0)
    return jnp.cos(ang), jnp.sin(ang)


NA_ROW_OFFSET = (NA_WIN_H - 1, NA_WIN_H - 1 - NA_WIN_H // 2,
                 NA_WIN_H - 1 - (NA_K_ROWS - NA_Q_ROWS))


def _na_window_start(variant, a):
    return (0, a, NA_K_ROWS - NA_WIN_H)[variant]


def _natten_bias_kernel(even_ref, odd_ref, mask_ref, o_ref):
    lanes = 2 * GRID_W
    lane = lax.broadcasted_iota(jnp.int32, (GRID_W, lanes), 1)
    masked = jnp.full((GRID_W, lanes), -jnp.inf, F32)
    for variant in range(3):
        for i in range(NA_K_ROWS):
            for pair in range(NA_Q_ROWS // 2):
                rows_a = (2 * pair, 2 * pair + 1)
                inside = [0 <= i - _na_window_start(variant, a) < NA_WIN_H for a in rows_a]
                blk = masked
                if any(inside):
                    base = jnp.zeros((1, lanes), F32)
                    for a, ok, src in zip(rows_a, inside, (even_ref, odd_ref)):
                        if ok:
                            d = i - a + NA_ROW_OFFSET[variant]
                            base = base + src[0, d:d + 1, :]
                    blk = pltpu.roll(jnp.broadcast_to(base, (GRID_W, lanes)), 0, 1,
                                     stride=1, stride_axis=0) + mask_ref[...]
                    if not inside[0]:
                        blk = jnp.where(lane < GRID_W, masked, blk)
                    if not inside[1]:
                        blk = jnp.where(lane >= GRID_W, masked, blk)
                o_ref[variant, 0, i * GRID_W:(i + 1) * GRID_W, pair * lanes:(pair + 1) * lanes] = blk


def _natten_bias(rel_bias):
    table = rel_bias.astype(F32) * LOG2_E
    lanes = 2 * GRID_W
    m = jnp.arange(lanes)
    reach = NA_WIN_W - 1
    col_even = jnp.where(m < GRID_W, reach - m, reach - (m - lanes))
    col_odd = reach - (m - GRID_W)

    def place(col, keep):
        vals = jnp.take(table, jnp.clip(col, 0, NA_BIAS_W - 1), axis=2)
        vals = jnp.where((keep & (col >= 0) & (col < NA_BIAS_W))[None, None, :], vals, 0.0)
        return jnp.pad(vals, ((0, 0), (0, 1), (0, 0)))

    even = place(col_even, (m <= reach) | (m >= lanes - reach))
    odd = place(col_odd, (m >= GRID_W - reach) & (m <= GRID_W + reach))
    c = jnp.arange(GRID_W)[None, :]
    kc = jnp.arange(GRID_W)[:, None]
    c0 = jnp.clip(c - NA_WIN_W // 2, 0, GRID_W - NA_WIN_W)
    valid_c = (kc >= c0) & (kc < c0 + NA_WIN_W)
    mask = jnp.where(jnp.concatenate([valid_c, valid_c], axis=1), 0.0, -jnp.inf).astype(F32)
    return pl.pallas_call(
        _natten_bias_kernel,
        grid=(N_HEADS,),
        in_specs=[
            pl.BlockSpec((1, NA_BIAS_H + 1, lanes), lambda h: (h, 0, 0)),
            pl.BlockSpec((1, NA_BIAS_H + 1, lanes), lambda h: (h, 0, 0)),
            pl.BlockSpec((GRID_W, lanes), lambda h: (0, 0)),
        ],
        out_specs=pl.BlockSpec((3, 1, NA_NK, NA_NQ), lambda h: (0, h, 0, 0)),
        out_shape=jax.ShapeDtypeStruct((3, N_HEADS, NA_NK, NA_NQ), F32),
        compiler_params=_params(1),
        name="natten_bias",
    )(even, odd, mask)


def _trunk(x, p):
    _, s, _ = x.shape
    depth = p["norm_mix"].shape[0]
    for i in range(depth):
        j = i // 2
        g_mix = p["norm_mix"][i][None, :]
        if i % 2 == 0:
            q_t, k, v_t = _qkv_a(x, g_mix, p["a_w_qkv_t"][j], p["a_q_gain"][j], p["a_k_gain"][j],
                                 p["cos_t"], p["sin_t"])
            o = _flash(q_t, k, v_t)
            w_o = p["a_w_o"][j]
        else:
            q_t, k, v_t = _qkv_b(x, g_mix, p["b_w_qv_t"][j], p["b_w_k"][j])
            o = _natten(q_t, k, v_t, p["b_bias"][j])
            w_o = p["b_w_o"][j]
        x = _mlp(x, o, w_o, p["norm_mlp"][i][None, :], p["mlp_w_in"][i], p["mlp_w_out"][i],
                 p["norm_final"][None, :], final=(i == depth - 1))
    return x


def kernel(x_prompt, x_sample, norm_mix, norm_mlp, norm_final, a_w_qkv, a_q_norm, a_k_norm, a_w_o,
           b_w_qkv, b_rel_bias, b_w_o, mlp_w_in, mlp_w_out):
    scale = HEAD_DIM ** -0.5 * LOG2_E
    q_bound = jnp.max(jnp.abs(a_q_norm), axis=1) * scale
    k_bound = jnp.max(jnp.abs(a_k_norm), axis=1)
    ratio = k_bound / q_bound
    ratio = jnp.where(jnp.isfinite(ratio) & (ratio > 0), ratio, 1.0)
    balance = jnp.exp2(jnp.round(0.5 * jnp.log2(ratio)))[:, None]
    shared = {
        "norm_mix": norm_mix, "norm_mlp": norm_mlp, "norm_final": norm_final,
        "a_w_qkv_t": jnp.swapaxes(a_w_qkv, 1, 2).astype(BF16),
        "a_q_gain": jnp.broadcast_to((a_q_norm * scale * balance)[:, :, None],
                                     a_q_norm.shape + (ROW_TILE,)),
        "a_k_gain": jnp.broadcast_to((a_k_norm / balance)[:, :, None],
                                     a_k_norm.shape + (ROW_TILE,)),
        "a_w_o": a_w_o.astype(BF16),
        "b_w_qv_t": jnp.swapaxes(jnp.concatenate([b_w_qkv[:, :, :HD], b_w_qkv[:, :, 2 * HD:]],
                                                 axis=2), 1, 2).astype(BF16),
        "b_w_k": b_w_qkv[:, :, HD:2 * HD].astype(BF16),
        "b_w_o": b_w_o.astype(BF16),
        "mlp_w_in": mlp_w_in.astype(BF16),
        "mlp_w_out": mlp_w_out.astype(BF16),
        "b_bias": [_natten_bias(rb) for rb in b_rel_bias],
    }
    outs = []
    for x in (x_prompt, x_sample):
        cos_t, sin_t = _rope_tables_t(x.shape[1])
        outs.append(_trunk(x, dict(shared, cos_t=cos_t, sin_t=sin_t)))
    return tuple(outs)
```

```python
import functools

import jax
import jax.numpy as jnp
from jax import lax
from jax.experimental import pallas as pl
from jax.experimental.pallas import tpu as pltpu

D_MODEL = 1024
GRID_W = 64
N_HEADS = 16
HEAD_DIM = 64
A_KV_HEADS = 4
A_GROUP = N_HEADS // A_KV_HEADS
ROPE_THETA = 10000.0
AXIS_DIM = HEAD_DIM // 2
ROPE_HALF = AXIS_DIM // 2
NA_WIN_H = 8
NA_WIN_W = 16
NA_BIAS_H = 2 * NA_WIN_H - 1
NA_BIAS_W = 2 * NA_WIN_W - 1
D_FF = 4 * D_MODEL
NORM_EPS = 1e-6
LOG2_E = 1.4426950408889634
NA_SCORE_SCALE = HEAD_DIM ** -0.5 * LOG2_E
HD = N_HEADS * HEAD_DIM
A_QKV = (N_HEADS + 2 * A_KV_HEADS) * HEAD_DIM

ROW_TILE = 512
FLASH_TQ = 256
FLASH_Q_TILES = 2
FLASH_TK = 512
FLASH_HEADS = 4
FLASH_L_ROWS = 16
NA_Q_ROWS = 4
NA_K_ROWS = 12
NA_HEADS = 4
NA_BLOCKS = 8
O_GROUPS = 4
O_GROUP_WIDTH = 256
NA_NQ = NA_Q_ROWS * GRID_W
NA_NK = NA_K_ROWS * GRID_W
FF_CHUNK = 1024
VMEM_LIMIT_BYTES = 56 * 1024 * 1024

BF16 = jnp.bfloat16
F32 = jnp.float32
F8 = jnp.float8_e4m3fn
F8_MAX = 448.0
QK_DEPTH = 4 * HEAD_DIM


def _params(n_axes):
    return pltpu.CompilerParams(
        dimension_semantics=("arbitrary",) * n_axes, vmem_limit_bytes=VMEM_LIMIT_BYTES)


def _resident(shape):
    zeros = (0,) * len(shape)
    return pl.BlockSpec(shape, lambda *_: zeros, pipeline_mode=pl.Buffered(1))


def _split_f8(x):
    x = jnp.clip(x, -F8_MAX, F8_MAX)
    hi = x.astype(F8)
    lo = (x - hi.astype(F32)).astype(F8)
    return hi, lo


def _rms(x, gain):
    ms = jnp.mean(x * x, axis=-1, keepdims=True)
    return x * lax.rsqrt(ms + NORM_EPS) * gain


def _qkv_a_kernel(x_ref, g_ref, wt_ref, qg_ref, kg_ref, cos_ref, sin_ref,
                  q_ref, k_ref, v_ref, s_ref, *, tm):
    h = _rms(x_ref[0], g_ref[...]).astype(BF16)
    qkv_t = lax.dot_general(wt_ref[...], h, (((1,), (1,)), ((), ())),
                            preferred_element_type=F32)
    n_slots = N_HEADS + 2 * A_KV_HEADS
    s_ref[...] = qkv_t.reshape(n_slots, HEAD_DIM, tm)

    def norm_rope(lo, hi, gain_ref, store):
        t = s_ref[lo:hi]
        r = lax.rsqrt(jnp.sum(t * t, axis=1, keepdims=True) * (1.0 / HEAD_DIM) + NORM_EPS)
        for a in range(2):
            d1 = a * AXIS_DIM
            d2 = d1 + ROPE_HALF
            x1 = s_ref[lo:hi, d1:d1 + ROPE_HALF, :] * r * gain_ref[d1:d1 + ROPE_HALF, :]
            x2 = s_ref[lo:hi, d2:d2 + ROPE_HALF, :] * r * gain_ref[d2:d2 + ROPE_HALF, :]
            c = cos_ref[a * ROPE_HALF:(a + 1) * ROPE_HALF, :]
            s = sin_ref[a * ROPE_HALF:(a + 1) * ROPE_HALF, :]
            store(d1, x1 * c - x2 * s)
            store(d2, x2 * c + x1 * s)

    def store_q(d, val):
        s_ref[0:N_HEADS, d:d + ROPE_HALF, :] = val

    def store_k(d, val):
        s_ref[N_HEADS:N_HEADS + A_KV_HEADS, d:d + ROPE_HALF, :] = val

    norm_rope(0, N_HEADS, qg_ref, store_q)
    norm_rope(N_HEADS, N_HEADS + A_KV_HEADS, kg_ref, store_k)
    q_hi, q_lo = _split_f8(s_ref[0:N_HEADS])
    for part, val in enumerate((q_hi, q_hi, q_lo, q_lo)):
        q_ref[0, :, part * HEAD_DIM:(part + 1) * HEAD_DIM, :] = val
    for g in range(A_KV_HEADS):
        k_hi, k_lo = _split_f8(s_ref[N_HEADS + g])
        k_hi, k_lo = k_hi.astype(F32), k_lo.astype(F32)
        k_ext_t = jnp.concatenate([k_hi, k_lo, k_hi, k_lo], axis=0)
        k_ref[0, g] = k_ext_t.T.astype(F8)
        for c in range(tm // FLASH_TK):
            v_ref[0, g, c] = s_ref[N_HEADS + A_KV_HEADS + g, :,
                                   c * FLASH_TK:(c + 1) * FLASH_TK].astype(BF16)


def _qkv_a(x, gain, w_t, q_gain, k_gain, cos_t, sin_t):
    b, s, _ = x.shape
    tm = ROW_TILE
    n_slots = N_HEADS + 2 * A_KV_HEADS
    return pl.pallas_call(
        functools.partial(_qkv_a_kernel, tm=tm),
        grid=(b, s // tm),
        in_specs=[
            pl.BlockSpec((1, tm, D_MODEL), lambda bi, i: (bi, i, 0)),
            _resident((1, D_MODEL)),
            _resident((A_QKV, D_MODEL)),
            _resident((HEAD_DIM, tm)),
            _resident((HEAD_DIM, tm)),
            pl.BlockSpec((AXIS_DIM, tm), lambda bi, i: (0, i)),
            pl.BlockSpec((AXIS_DIM, tm), lambda bi, i: (0, i)),
        ],
        out_specs=[
            pl.BlockSpec((1, N_HEADS, QK_DEPTH, tm), lambda bi, i: (bi, 0, 0, i)),
            pl.BlockSpec((1, A_KV_HEADS, tm, QK_DEPTH), lambda bi, i: (bi, 0, i, 0)),
            pl.BlockSpec((1, A_KV_HEADS, tm // FLASH_TK, HEAD_DIM, FLASH_TK),
                         lambda bi, i: (bi, 0, i, 0, 0)),
        ],
        out_shape=[
            jax.ShapeDtypeStruct((b, N_HEADS, QK_DEPTH, s), F8),
            jax.ShapeDtypeStruct((b, A_KV_HEADS, s, QK_DEPTH), F8),
            jax.ShapeDtypeStruct((b, A_KV_HEADS, s // FLASH_TK, HEAD_DIM, FLASH_TK), BF16),
        ],
        scratch_shapes=[pltpu.VMEM((n_slots, HEAD_DIM, tm), F32)],
        compiler_params=_params(2),
        name="qkv_a",
    )(x, gain, w_t, q_gain, k_gain, cos_t, sin_t)


def _flash_kernel(q_ref, k_ref, v_ref, qn_ref, kn_ref, o_ref, s_ref, cm_ref, *, n_chunks, tq):
    first = (pl.program_id(0) == 0) & (pl.program_id(1) == 0) & (pl.program_id(2) == 0)

    def scores(c, slot, e, j, q_src=q_ref, k_src=k_ref):
        k_c = k_src[0, 0, c * FLASH_TK:(c + 1) * FLASH_TK, :]
        q_e = q_src[0, e, :, j * tq:(j + 1) * tq]
        s_t = jnp.dot(k_c, q_e, preferred_element_type=F32)
        s_ref[slot, e] = s_t
        cm_ref[slot, e] = jnp.max(s_t, axis=0, keepdims=True)

    ones = jnp.ones((FLASH_L_ROWS, FLASH_TK), BF16)

    def update(c, slot, e, stat):
        m, acc = stat
        s_t = s_ref[slot, e]
        m_new = jnp.maximum(m, cm_ref[slot, e])
        alpha = jnp.exp2(m - m_new)
        p = jnp.exp2(s_t - m_new).astype(BF16)
        v_ext = jnp.concatenate([v_ref[0, 0, c], ones], axis=0)
        acc = alpha * acc + jnp.dot(v_ext, p, preferred_element_type=F32)
        return m_new, acc

    @pl.when(first)
    def _():
        for e in range(FLASH_HEADS):
            scores(0, 0, e, 0)

    for j in range(FLASH_Q_TILES):
        stats = tuple((jnp.full((1, tq), -jnp.inf, F32),
                       jnp.zeros((HEAD_DIM + FLASH_L_ROWS, tq), F32)) for _ in range(FLASH_HEADS))
        for c in range(n_chunks):
            new = []
            for e in range(FLASH_HEADS):
                if c + 1 < n_chunks:
                    scores(c + 1, (c + 1) % 2, e, j)
                elif j + 1 < FLASH_Q_TILES:
                    scores(0, 0, e, j + 1)
                else:
                    scores(0, 0, e, 0, qn_ref, kn_ref)
                new.append(update(c, c % 2, e, stats[e]))
            stats = tuple(new)
        o_t = jnp.concatenate([acc[:HEAD_DIM] / acc[HEAD_DIM:HEAD_DIM + 1] for (_, acc) in stats],
                              axis=0)
        o_ref[0, 0, j * tq:(j + 1) * tq, :] = o_t.T.astype(BF16)


def _flash(q_t, k, v_t):
    b, _, _, s = q_t.shape
    tq = FLASH_TQ
    n_chunks = s // FLASH_TK
    pairs = N_HEADS // FLASH_HEADS
    per_kv = A_GROUP // FLASH_HEADS
    tq_step = tq * FLASH_Q_TILES
    n_q = s // tq_step
    assert n_chunks % 2 == 0

    def following(bi, hp, i):
        flat = jnp.minimum((bi * pairs + hp) * n_q + i + 1, b * pairs * n_q - 1)
        return flat // (pairs * n_q), (flat // n_q) % pairs, flat % n_q

    def q_next(bi, hp, i):
        nb, nh, ni = following(bi, hp, i)
        return nb, nh, 0, ni

    def k_next(bi, hp, i):
        nb, nh, _ = following(bi, hp, i)
        return nb, nh // per_kv, 0, 0

    return pl.pallas_call(
        functools.partial(_flash_kernel, n_chunks=n_chunks, tq=tq),
        grid=(b, pairs, n_q),
        in_specs=[
            pl.BlockSpec((1, FLASH_HEADS, QK_DEPTH, tq_step), lambda bi, hp, i: (bi, hp, 0, i)),
            pl.BlockSpec((1, 1, s, QK_DEPTH), lambda bi, hp, i: (bi, hp // per_kv, 0, 0)),
            pl.BlockSpec((1, 1, n_chunks, HEAD_DIM, FLASH_TK),
                         lambda bi, hp, i: (bi, hp // per_kv, 0, 0, 0)),
            pl.BlockSpec((1, FLASH_HEADS, QK_DEPTH, tq_step), q_next),
            pl.BlockSpec((1, 1, s, QK_DEPTH), k_next),
        ],
        out_specs=pl.BlockSpec((1, 1, tq_step, O_GROUP_WIDTH), lambda bi, hp, i: (bi, hp, i, 0)),
        out_shape=jax.ShapeDtypeStruct((b, O_GROUPS, s, O_GROUP_WIDTH), BF16),
        scratch_shapes=[pltpu.VMEM((2, FLASH_HEADS, FLASH_TK, tq), F32),
                        pltpu.VMEM((2, FLASH_HEADS, 1, tq), F32)],
        compiler_params=_params(3),
        name="flash_a",
    )(q_t, k, v_t, q_t, k)


def _mlp_kernel(x_ref, a_ref, wo_ref, g_ref, win_ref, wout_ref, gf_ref, o_ref, *, final):
    a = jnp.concatenate([a_ref[0, g] for g in range(O_GROUPS)], axis=1)
    x = x_ref[0] + jnp.dot(a, wo_ref[...], preferred_element_type=F32)
    h = _rms(x, g_ref[...]).astype(BF16)
    acc = x
    for f in range(D_FF // FF_CHUNK):
        u = jnp.dot(h, win_ref[:, f * FF_CHUNK:(f + 1) * FF_CHUNK], preferred_element_type=F32)
        u = jnp.maximum(u, 0.0)
        acc = acc + jnp.dot((u * u).astype(BF16), wout_ref[f * FF_CHUNK:(f + 1) * FF_CHUNK, :],
                            preferred_element_type=F32)
    if final:
        acc = _rms(acc, gf_ref[...])
    o_ref[0] = acc


def _mlp(x, attn, w_o, gain, w_in, w_out, gain_final, final):
    b, s, _ = x.shape
    tm = ROW_TILE
    return pl.pallas_call(
        functools.partial(_mlp_kernel, final=final),
        grid=(b, s // tm),
        in_specs=[
            pl.BlockSpec((1, tm, D_MODEL), lambda bi, i: (bi, i, 0)),
            pl.BlockSpec((1, O_GROUPS, tm, O_GROUP_WIDTH), lambda bi, i: (bi, 0, i, 0)),
            _resident((HD, D_MODEL)),
            _resident((1, D_MODEL)),
            _resident((D_MODEL, D_FF)),
            _resident((D_FF, D_MODEL)),
            _resident((1, D_MODEL)),
        ],
        out_specs=pl.BlockSpec((1, tm, D_MODEL), lambda bi, i: (bi, i, 0)),
        out_shape=jax.ShapeDtypeStruct((b, s, D_MODEL), F32),
        compiler_params=_params(2),
        name="mlp_final" if final else "mlp",
    )(x, attn, w_o, gain, w_in, w_out, gain_final)


def _qkv_b_kernel(x_ref, g_ref, wqv_t_ref, wk_ref, q_ref, k_ref, v_ref, *, tm):
    h = _rms(x_ref[0], g_ref[...]).astype(BF16)
    k = jnp.dot(h, wk_ref[...], preferred_element_type=F32).astype(BF16)
    width = NA_HEADS * HEAD_DIM
    for g in range(N_HEADS // NA_HEADS):
        k_ref[0, g] = k[:, g * width:(g + 1) * width]
    nt = (((1,), (1,)), ((), ()))
    q_t = lax.dot_general(wqv_t_ref[0:HD, :], h, nt, preferred_element_type=F32)
    q_ref[0, 0] = (q_t * NA_SCORE_SCALE).astype(BF16)
    v_t = lax.dot_general(wqv_t_ref[HD:2 * HD, :], h, nt, preferred_element_type=F32)
    for hd in range(N_HEADS):
        for c in range(tm // NA_NQ):
            v_ref[0, hd, c] = v_t[hd * HEAD_DIM:(hd + 1) * HEAD_DIM,
                                  c * NA_NQ:(c + 1) * NA_NQ].astype(BF16)


def _qkv_b(x, gain, wqv_t, wk):
    b, s, _ = x.shape
    tm = 2 * NA_NQ
    groups = N_HEADS // NA_HEADS
    width = NA_HEADS * HEAD_DIM
    return pl.pallas_call(
        functools.partial(_qkv_b_kernel, tm=tm),
        grid=(b, s // tm),
        in_specs=[
            pl.BlockSpec((1, tm, D_MODEL), lambda bi, i: (bi, i, 0)),
            _resident((1, D_MODEL)),
            _resident((2 * HD, D_MODEL)),
            _resident((D_MODEL, HD)),
        ],
        out_specs=[
            pl.BlockSpec((1, 1, HD, tm), lambda bi, i: (bi, i, 0, 0)),
            pl.BlockSpec((1, groups, tm, width), lambda bi, i: (bi, 0, i, 0)),
            pl.BlockSpec((1, N_HEADS, tm // NA_NQ, HEAD_DIM, NA_NQ), lambda bi, i: (bi, 0, i, 0, 0)),
        ],
        out_shape=[
            jax.ShapeDtypeStruct((b, s // tm, HD, tm), BF16),
            jax.ShapeDtypeStruct((b, groups, s, width), BF16),
            jax.ShapeDtypeStruct((b, N_HEADS, s // NA_NQ, HEAD_DIM, NA_NQ), BF16),
        ],
        compiler_params=_params(2),
        name="qkv_b",
    )(x, gain, wqv_t, wk)


def _natten_kernel(q_ref, qn_ref, k_hbm, v_hbm, b_ref, o_ref, s_ref, cm_ref, k_buf, v_buf, sem,
                   *, rows, batch):
    u = pl.program_id(2)
    steps = rows // NA_Q_ROWS // NA_BLOCKS
    sweep = pl.program_id(0) * batch + pl.program_id(1)
    n_sweeps = pl.num_programs(0) * batch
    kv_slot = sweep % 2

    def kv_copies(sw, sl):
        hg, bi = sw // batch, sw % batch
        return (pltpu.make_async_copy(k_hbm.at[bi, hg], k_buf.at[sl], sem.at[0, sl]),
                pltpu.make_async_copy(v_hbm.at[bi, pl.ds(hg * NA_HEADS, NA_HEADS)], v_buf.at[sl],
                                      sem.at[1, sl]))

    @pl.when((u == 0) & (sweep == 0))
    def _():
        for copy in kv_copies(0, 0):
            copy.start()
        for copy in kv_copies(0, 0):
            copy.wait()

    @pl.when((u == 0) & (sweep + 1 < n_sweeps))
    def _():
        for copy in kv_copies(sweep + 1, 1 - kv_slot):
            copy.start()

    n_blocks = rows // NA_Q_ROWS
    width = NA_HEADS * HEAD_DIM
    row_id = lax.broadcasted_iota(jnp.int32, (width, NA_NQ), 0)
    ones = jnp.ones((FLASH_L_ROWS, NA_NK), BF16)

    def key_base(blk):
        return jnp.clip(NA_Q_ROWS * blk - NA_WIN_H // 2, 0, rows - NA_K_ROWS)

    def scores(blk, q_blk, slot, e):
        variant = jnp.where(blk == 0, 0, jnp.where(blk == n_blocks - 1, 2, 1))
        start = pl.multiple_of(key_base(blk) * GRID_W, NA_NQ)
        k_w = k_buf[kv_slot, pl.ds(start, NA_NK), :]
        mine = (row_id >= e * HEAD_DIM) & (row_id < (e + 1) * HEAD_DIM)
        q_e = jnp.where(mine, q_blk, jnp.zeros_like(q_blk))
        s_t = jnp.dot(k_w, q_e, preferred_element_type=F32) + b_ref[variant, e]
        s_ref[slot, e] = s_t
        cm_ref[slot, e] = jnp.max(s_t, axis=0, keepdims=True)

    def attend(blk, slot, e):
        chunk0 = key_base(blk) // NA_Q_ROWS
        s_t = s_ref[slot, e]
        p = jnp.exp2(s_t - cm_ref[slot, e]).astype(BF16)
        v_t = [v_buf[kv_slot, e, chunk0 + j] for j in range(NA_K_ROWS // NA_Q_ROWS)]
        v_ext = jnp.concatenate([jnp.concatenate(v_t, axis=1), ones], axis=0)
        acc = jnp.dot(v_ext, p, preferred_element_type=F32)
        return acc[:HEAD_DIM] / acc[HEAD_DIM:HEAD_DIM + 1]

    @pl.when(u == 0)
    def _():
        for e in range(NA_HEADS):
            scores(0, q_ref[0, 0, :, 0:NA_NQ], 0, e)

    def half(blk, slot, blk_next, q_next, row0):
        outs = []
        for e in range(NA_HEADS):
            scores(blk_next, q_next, 1 - slot, e)
            outs.append(attend(blk, slot, e))
        o_ref[0, 0, row0:row0 + NA_NQ, :] = jnp.concatenate(outs, axis=0).T.astype(BF16)

    def q_block(n):
        return q_ref[0, n // 2, :, (n % 2) * NA_NQ:(n % 2 + 1) * NA_NQ]

    for n in range(NA_BLOCKS - 1):
        half(NA_BLOCKS * u + n, n % 2, NA_BLOCKS * u + n + 1, q_block(n + 1), n * NA_NQ)
    u_next = jnp.minimum(u + 1, n_blocks // NA_BLOCKS - 1)
    half(NA_BLOCKS * u + NA_BLOCKS - 1, (NA_BLOCKS - 1) % 2, NA_BLOCKS * u_next,
         qn_ref[0, 0, :, 0:NA_NQ], (NA_BLOCKS - 1) * NA_NQ)

    @pl.when((u == steps - 1) & (sweep + 1 < n_sweeps))
    def _():
        for copy in kv_copies(sweep + 1, 1 - kv_slot):
            copy.wait()


def _natten(q_t, k, v_t, bias):
    b, _, s, _ = k.shape
    rows = s // GRID_W
    n_blocks = rows // NA_Q_ROWS
    groups = N_HEADS // NA_HEADS
    width = NA_HEADS * HEAD_DIM
    steps = n_blocks // NA_BLOCKS
    tiles = NA_BLOCKS // 2
    assert rows >= NA_K_ROWS + NA_Q_ROWS
    return pl.pallas_call(
        functools.partial(_natten_kernel, rows=rows, batch=b),
        grid=(groups, b, steps),
        in_specs=[
            pl.BlockSpec((1, tiles, width, 2 * NA_NQ), lambda hg, bi, u: (bi, u, hg, 0)),
            pl.BlockSpec((1, 1, width, 2 * NA_NQ),
                         lambda hg, bi, u: (bi, tiles * jnp.minimum(u + 1, steps - 1), hg, 0)),
            pl.BlockSpec(memory_space=pl.ANY),
            pl.BlockSpec(memory_space=pl.ANY),
            pl.BlockSpec((3, NA_HEADS, NA_NK, NA_NQ), lambda hg, bi, u: (0, hg, 0, 0),
                         pipeline_mode=pl.Buffered(1)),
        ],
        out_specs=pl.BlockSpec((1, 1, NA_BLOCKS * NA_NQ, width), lambda hg, bi, u: (bi, hg, u, 0)),
        out_shape=jax.ShapeDtypeStruct((b, groups, s, width), BF16),
        scratch_shapes=[pltpu.VMEM((2, NA_HEADS, NA_NK, NA_NQ), F32),
                        pltpu.VMEM((2, NA_HEADS, 1, NA_NQ), F32),
                        pltpu.VMEM((2, s, width), BF16),
                        pltpu.VMEM((2, NA_HEADS, s // NA_NQ, HEAD_DIM, NA_NQ), BF16),
                        pltpu.SemaphoreType.DMA((2, 2))],
        compiler_params=_params(3),
        name="natten_b",
    )(q_t, q_t, k, v_t, bias)


def _rope_tables_t(seq):
    t = jnp.arange(seq)
    row = (t // GRID_W).astype(F32)
    col = (t % GRID_W).astype(F32)
    inv = ROPE_THETA ** (-jnp.arange(0, AXIS_DIM, 2, dtype=F32) / AXIS_DIM)
    ang = jnp.concatenate([inv[:, None] * row[None, :], inv[:, None] * col[None, :]], axis=0)
    return jnp.cos(ang), jnp.sin(ang)


NA_ROW_OFFSET = (NA_WIN_H - 1, NA_WIN_H - 1 - NA_WIN_H // 2,
                 NA_WIN_H - 1 - (NA_K_ROWS - NA_Q_ROWS))


def _na_window_start(variant, a):
    return (0, a, NA_K_ROWS - NA_WIN_H)[variant]


def _natten_bias_kernel(even_ref, odd_ref, mask_ref, o_ref):
    lanes = 2 * GRID_W
    lane = lax.broadcasted_iota(jnp.int32, (GRID_W, lanes), 1)
    masked = jnp.full((GRID_W, lanes), -jnp.inf, F32)
    for variant in range(3):
        for i in range(NA_K_ROWS):
            for pair in range(NA_Q_ROWS // 2):
                rows_a = (2 * pair, 2 * pair + 1)
                inside = [0 <= i - _na_window_start(variant, a) < NA_WIN_H for a in rows_a]
                blk = masked
                if any(inside):
                    base = jnp.zeros((1, lanes), F32)
                    for a, ok, src in zip(rows_a, inside, (even_ref, odd_ref)):
                        if ok:
                            d = i - a + NA_ROW_OFFSET[variant]
                            base = base + src[0, d:d + 1, :]
                    blk = pltpu.roll(jnp.broadcast_to(base, (GRID_W, lanes)), 0, 1,
                                     stride=1, stride_axis=0) + mask_ref[...]
                    if not inside[0]:
                        blk = jnp.where(lane < GRID_W, masked, blk)
                    if not inside[1]:
                        blk = jnp.where(lane >= GRID_W, masked, blk)
                o_ref[variant, 0, i * GRID_W:(i + 1) * GRID_W, pair * lanes:(pair + 1) * lanes] = blk


def _natten_bias(rel_bias):
    table = rel_bias.astype(F32) * LOG2_E
    lanes = 2 * GRID_W
    m = jnp.arange(lanes)
    reach = NA_WIN_W - 1
    col_even = jnp.where(m < GRID_W, reach - m, reach - (m - lanes))
    col_odd = reach - (m - GRID_W)

    def place(col, keep):
        vals = jnp.take(table, jnp.clip(col, 0, NA_BIAS_W - 1), axis=2)
        vals = jnp.where((keep & (col >= 0) & (col < NA_BIAS_W))[None, None, :], vals, 0.0)
        return jnp.pad(vals, ((0, 0), (0, 1), (0, 0)))

    even = place(col_even, (m <= reach) | (m >= lanes - reach))
    odd = place(col_odd, (m >= GRID_W - reach) & (m <= GRID_W + reach))
    c = jnp.arange(GRID_W)[None, :]
    kc = jnp.arange(GRID_W)[:, None]
    c0 = jnp.clip(c - NA_WIN_W // 2, 0, GRID_W - NA_WIN_W)
    valid_c = (kc >= c0) & (kc < c0 + NA_WIN_W)
    mask = jnp.where(jnp.concatenate([valid_c, valid_c], axis=1), 0.0, -jnp.inf).astype(F32)
    return pl.pallas_call(
        _natten_bias_kernel,
        grid=(N_HEADS,),
        in_specs=[
            pl.BlockSpec((1, NA_BIAS_H + 1, lanes), lambda h: (h, 0, 0)),
            pl.BlockSpec((1, NA_BIAS_H + 1, lanes), lambda h: (h, 0, 0)),
            pl.BlockSpec((GRID_W, lanes), lambda h: (0, 0)),
        ],
        out_specs=pl.BlockSpec((3, 1, NA_NK, NA_NQ), lambda h: (0, h, 0, 0)),
        out_shape=jax.ShapeDtypeStruct((3, N_HEADS, NA_NK, NA_NQ), F32),
        compiler_params=_params(1),
        name="natten_bias",
    )(even, odd, mask)


def _trunk(x, p):
    _, s, _ = x.shape
    depth = p["norm_mix"].shape[0]
    for i in range(depth):
        j = i // 2
        g_mix = p["norm_mix"][i][None, :]
        if i % 2 == 0:
            q_t, k, v_t = _qkv_a(x, g_mix, p["a_w_qkv_t"][j], p["a_q_gain"][j], p["a_k_gain"][j],
                                 p["cos_t"], p["sin_t"])
            o = _flash(q_t, k, v_t)
            w_o = p["a_w_o"][j]
        else:
            q_t, k, v_t = _qkv_b(x, g_mix, p["b_w_qv_t"][j], p["b_w_k"][j])
            o = _natten(q_t, k, v_t, p["b_bias"][j])
            w_o = p["b_w_o"][j]
        x = _mlp(x, o, w_o, p["norm_mlp"][i][None, :], p["mlp_w_in"][i], p["mlp_w_out"][i],
                 p["norm_final"][None, :], final=(i == depth - 1))
    return x


def kernel(x_prompt, x_sample, norm_mix, norm_mlp, norm_final, a_w_qkv, a_q_norm, a_k_norm, a_w_o,
           b_w_qkv, b_rel_bias, b_w_o, mlp_w_in, mlp_w_out):
    scale = HEAD_DIM ** -0.5 * LOG2_E
    q_bound = jnp.max(jnp.abs(a_q_norm), axis=1) * scale
    k_bound = jnp.max(jnp.abs(a_k_norm), axis=1)
    ratio = k_bound / q_bound
    ratio = jnp.where(jnp.isfinite(ratio) & (ratio > 0), ratio, 1.0)
    balance = jnp.exp2(jnp.round(0.5 * jnp.log2(ratio)))[:, None]
    shared = {
        "norm_mix": norm_mix, "norm_mlp": norm_mlp, "norm_final": norm_final,
        "a_w_qkv_t": jnp.swapaxes(a_w_qkv, 1, 2).astype(BF16),
        "a_q_gain": jnp.broadcast_to((a_q_norm * scale * balance)[:, :, None],
                                     a_q_norm.shape + (ROW_TILE,)),
        "a_k_gain": jnp.broadcast_to((a_k_norm / balance)[:, :, None],
                                     a_k_norm.shape + (ROW_TILE,)),
        "a_w_o": a_w_o.astype(BF16),
        "b_w_qv_t": jnp.swapaxes(jnp.concatenate([b_w_qkv[:, :, :HD], b_w_qkv[:, :, 2 * HD:]],
                                                 axis=2), 1, 2).astype(BF16),
        "b_w_k": b_w_qkv[:, :, HD:2 * HD].astype(BF16),
        "b_w_o": b_w_o.astype(BF16),
        "mlp_w_in": mlp_w_in.astype(BF16),
        "mlp_w_out": mlp_w_out.astype(BF16),
        "b_bias": [_natten_bias(rb) for rb in b_rel_bias],
    }
    outs = []
    for x in (x_prompt, x_sample):
        cos_t, sin_t = _rope_tables_t(x.shape[1])
        outs.append(_trunk(x, dict(shared, cos_t=cos_t, sin_t=sin_t)))
    return tuple(outs)
```

```python
import functools

import jax
import jax.numpy as jnp
from jax import lax
from jax.experimental import pallas as pl
from jax.experimental.pallas import tpu as pltpu

D_MODEL = 1024
GRID_W = 64
N_HEADS = 16
HEAD_DIM = 64
A_KV_HEADS = 4
A_GROUP = N_HEADS // A_KV_HEADS
ROPE_THETA = 10000.0
AXIS_DIM = HEAD_DIM // 2
ROPE_HALF = AXIS_DIM // 2
NA_WIN_H = 8
NA_WIN_W = 16
NA_BIAS_H = 2 * NA_WIN_H - 1
NA_BIAS_W = 2 * NA_WIN_W - 1
D_FF = 4 * D_MODEL
NORM_EPS = 1e-6
LOG2_E = 1.4426950408889634
NA_SCORE_SCALE = HEAD_DIM ** -0.5 * LOG2_E
HD = N_HEADS * HEAD_DIM
A_QKV = (N_HEADS + 2 * A_KV_HEADS) * HEAD_DIM

ROW_TILE = 512
MLP_ROW_TILE = 1024
FLASH_TQ = 256
FLASH_Q_TILES = 2
FLASH_TK = 512
FLASH_HEADS = 4
FLASH_L_ROWS = 16
NA_Q_ROWS = 4
NA_K_ROWS = 12
NA_HEADS = 4
NA_BLOCKS = 8
O_GROUPS = 4
O_GROUP_WIDTH = 256
NA_NQ = NA_Q_ROWS * GRID_W
NA_NK = NA_K_ROWS * GRID_W
FF_CHUNK = 1024
VMEM_LIMIT_BYTES = 56 * 1024 * 1024

BF16 = jnp.bfloat16
F32 = jnp.float32
F8 = jnp.float8_e4m3fn
F8_MAX = 448.0
QK_DEPTH = 4 * HEAD_DIM


def _params(n_axes):
    return pltpu.CompilerParams(
        dimension_semantics=("arbitrary",) * n_axes, vmem_limit_bytes=VMEM_LIMIT_BYTES)


def _resident(shape):
    zeros = (0,) * len(shape)
    return pl.BlockSpec(shape, lambda *_: zeros, pipeline_mode=pl.Buffered(1))


def _split_f8(x):
    x = jnp.clip(x, -F8_MAX, F8_MAX)
    hi = x.astype(F8)
    lo = (x - hi.astype(F32)).astype(F8)
    return hi, lo


def _rms(x, gain):
    ms = jnp.mean(x * x, axis=-1, keepdims=True)
    return x * lax.rsqrt(ms + NORM_EPS) * gain


def _qkv_a_kernel(x_ref, g_ref, wt_ref, qg_ref, kg_ref, cos_ref, sin_ref,
                  q_ref, k_ref, v_ref, s_ref, *, tm):
    h = _rms(x_ref[0], g_ref[...]).astype(BF16)
    qkv_t = lax.dot_general(wt_ref[...], h, (((1,), (1,)), ((), ())),
                            preferred_element_type=F32)
    n_slots = N_HEADS + 2 * A_KV_HEADS
    s_ref[...] = qkv_t.reshape(n_slots, HEAD_DIM, tm)

    def norm_rope(lo, hi, gain_ref, store):
        t = s_ref[lo:hi]
        r = lax.rsqrt(jnp.sum(t * t, axis=1, keepdims=True) * (1.0 / HEAD_DIM) + NORM_EPS)
        for a in range(2):
            d1 = a * AXIS_DIM
            d2 = d1 + ROPE_HALF
            x1 = s_ref[lo:hi, d1:d1 + ROPE_HALF, :] * r * gain_ref[d1:d1 + ROPE_HALF, :]
            x2 = s_ref[lo:hi, d2:d2 + ROPE_HALF, :] * r * gain_ref[d2:d2 + ROPE_HALF, :]
            c = cos_ref[a * ROPE_HALF:(a + 1) * ROPE_HALF, :]
            s = sin_ref[a * ROPE_HALF:(a + 1) * ROPE_HALF, :]
            store(d1, x1 * c - x2 * s)
            store(d2, x2 * c + x1 * s)

    def store_q(d, val):
        s_ref[0:N_HEADS, d:d + ROPE_HALF, :] = val

    def store_k(d, val):
        s_ref[N_HEADS:N_HEADS + A_KV_HEADS, d:d + ROPE_HALF, :] = val

    norm_rope(0, N_HEADS, qg_ref, store_q)
    norm_rope(N_HEADS, N_HEADS + A_KV_HEADS, kg_ref, store_k)
    q_hi, q_lo = _split_f8(s_ref[0:N_HEADS])
    for part, val in enumerate((q_hi, q_hi, q_lo, q_lo)):
        q_ref[0, :, part * HEAD_DIM:(part + 1) * HEAD_DIM, :] = val
    for g in range(A_KV_HEADS):
        k_hi, k_lo = _split_f8(s_ref[N_HEADS + g])
        k_hi, k_lo = k_hi.astype(F32), k_lo.astype(F32)
        k_ext_t = jnp.concatenate([k_hi, k_lo, k_hi, k_lo], axis=0)
        k_ref[0, g] = k_ext_t.T.astype(F8)
        for c in range(tm // FLASH_TK):
            v_ref[0, g, c] = s_ref[N_HEADS + A_KV_HEADS + g, :,
                                   c * FLASH_TK:(c + 1) * FLASH_TK].astype(BF16)


def _qkv_a(x, gain, w_t, q_gain, k_gain, cos_t, sin_t):
    b, s, _ = x.shape
    tm = ROW_TILE
    n_slots = N_HEADS + 2 * A_KV_HEADS
    return pl.pallas_call(
        functools.partial(_qkv_a_kernel, tm=tm),
        grid=(b, s // tm),
        in_specs=[
            pl.BlockSpec((1, tm, D_MODEL), lambda bi, i: (bi, i, 0)),
            _resident((1, D_MODEL)),
            _resident((A_QKV, D_MODEL)),
            _resident((HEAD_DIM, tm)),
            _resident((HEAD_DIM, tm)),
            pl.BlockSpec((AXIS_DIM, tm), lambda bi, i: (0, i)),
            pl.BlockSpec((AXIS_DIM, tm), lambda bi, i: (0, i)),
        ],
        out_specs=[
            pl.BlockSpec((1, N_HEADS, QK_DEPTH, tm), lambda bi, i: (bi, 0, 0, i)),
            pl.BlockSpec((1, A_KV_HEADS, tm, QK_DEPTH), lambda bi, i: (bi, 0, i, 0)),
            pl.BlockSpec((1, A_KV_HEADS, tm // FLASH_TK, HEAD_DIM, FLASH_TK),
                         lambda bi, i: (bi, 0, i, 0, 0)),
        ],
        out_shape=[
            jax.ShapeDtypeStruct((b, N_HEADS, QK_DEPTH, s), F8),
            jax.ShapeDtypeStruct((b, A_KV_HEADS, s, QK_DEPTH), F8),
            jax.ShapeDtypeStruct((b, A_KV_HEADS, s // FLASH_TK, HEAD_DIM, FLASH_TK), BF16),
        ],
        scratch_shapes=[pltpu.VMEM((n_slots, HEAD_DIM, tm), F32)],
        compiler_params=_params(2),
        name="qkv_a",
    )(x, gain, w_t, q_gain, k_gain, cos_t, sin_t)


def _flash_kernel(q_ref, k_ref, v_ref, qn_ref, kn_ref, o_ref, s_ref, cm_ref, *, n_chunks, tq):
    first = (pl.program_id(0) == 0) & (pl.program_id(1) == 0) & (pl.program_id(2) == 0)

    def scores(c, slot, e, j, q_src=q_ref, k_src=k_ref):
        k_c = k_src[0, 0, c * FLASH_TK:(c + 1) * FLASH_TK, :]
        q_e = q_src[0, e, :, j * tq:(j + 1) * tq]
        s_t = jnp.dot(k_c, q_e, preferred_element_type=F32)
        s_ref[slot, e] = s_t
        cm_ref[slot, e] = jnp.max(s_t, axis=0, keepdims=True)

    ones = jnp.ones((FLASH_L_ROWS, FLASH_TK), BF16)

    def update(c, slot, e, stat):
        m, acc = stat
        s_t = s_ref[slot, e]
        m_new = jnp.maximum(m, cm_ref[slot, e])
        alpha = jnp.exp2(m - m_new)
        p = jnp.exp2(s_t - m_new).astype(BF16)
        v_ext = jnp.concatenate([v_ref[0, 0, c], ones], axis=0)
        acc = alpha * acc + jnp.dot(v_ext, p, preferred_element_type=F32)
        return m_new, acc

    @pl.when(first)
    def _():
        for e in range(FLASH_HEADS):
            scores(0, 0, e, 0)

    for j in range(FLASH_Q_TILES):
        stats = tuple((jnp.full((1, tq), -jnp.inf, F32),
                       jnp.zeros((HEAD_DIM + FLASH_L_ROWS, tq), F32)) for _ in range(FLASH_HEADS))
        for c in range(n_chunks):
            new = []
            for e in range(FLASH_HEADS):
                if c + 1 < n_chunks:
                    scores(c + 1, (c + 1) % 2, e, j)
                elif j + 1 < FLASH_Q_TILES:
                    scores(0, 0, e, j + 1)
                else:
                    scores(0, 0, e, 0, qn_ref, kn_ref)
                new.append(update(c, c % 2, e, stats[e]))
            stats = tuple(new)
        o_t = jnp.concatenate([acc[:HEAD_DIM] / acc[HEAD_DIM:HEAD_DIM + 1] for (_, acc) in stats],
                              axis=0)
        o_ref[0, 0, j * tq:(j + 1) * tq, :] = o_t.T.astype(BF16)


def _flash(q_t, k, v_t):
    b, _, _, s = q_t.shape
    tq = FLASH_TQ
    n_chunks = s // FLASH_TK
    pairs = N_HEADS // FLASH_HEADS
    per_kv = A_GROUP // FLASH_HEADS
    tq_step = tq * FLASH_Q_TILES
    n_q = s // tq_step
    assert n_chunks % 2 == 0

    def following(bi, hp, i):
        flat = jnp.minimum((bi * pairs + hp) * n_q + i + 1, b * pairs * n_q - 1)
        return flat // (pairs * n_q), (flat // n_q) % pairs, flat % n_q

    def q_next(bi, hp, i):
        nb, nh, ni = following(bi, hp, i)
        return nb, nh, 0, ni

    def k_next(bi, hp, i):
        nb, nh, _ = following(bi, hp, i)
        return nb, nh // per_kv, 0, 0

    return pl.pallas_call(
        functools.partial(_flash_kernel, n_chunks=n_chunks, tq=tq),
        grid=(b, pairs, n_q),
        in_specs=[
            pl.BlockSpec((1, FLASH_HEADS, QK_DEPTH, tq_step), lambda bi, hp, i: (bi, hp, 0, i)),
            pl.BlockSpec((1, 1, s, QK_DEPTH), lambda bi, hp, i: (bi, hp // per_kv, 0, 0)),
            pl.BlockSpec((1, 1, n_chunks, HEAD_DIM, FLASH_TK),
                         lambda bi, hp, i: (bi, hp // per_kv, 0, 0, 0)),
            pl.BlockSpec((1, FLASH_HEADS, QK_DEPTH, tq_step), q_next),
            pl.BlockSpec((1, 1, s, QK_DEPTH), k_next),
        ],
        out_specs=pl.BlockSpec((1, 1, tq_step, O_GROUP_WIDTH), lambda bi, hp, i: (bi, hp, i, 0)),
        out_shape=jax.ShapeDtypeStruct((b, O_GROUPS, s, O_GROUP_WIDTH), BF16),
        scratch_shapes=[pltpu.VMEM((2, FLASH_HEADS, FLASH_TK, tq), F32),
                        pltpu.VMEM((2, FLASH_HEADS, 1, tq), F32)],
        compiler_params=_params(3),
        name="flash_a",
    )(q_t, k, v_t, q_t, k)


def _mlp_kernel(x_ref, a_ref, wo_ref, g_ref, win_ref, wout_ref, gf_ref, o_ref, *, final):
    a = jnp.concatenate([a_ref[0, g] for g in range(O_GROUPS)], axis=1)
    x = x_ref[0] + jnp.dot(a, wo_ref[...], preferred_element_type=F32)
    h = _rms(x, g_ref[...]).astype(BF16)
    acc = x
    for f in range(D_FF // FF_CHUNK):
        u = jnp.dot(h, win_ref[:, f * FF_CHUNK:(f + 1) * FF_CHUNK], preferred_element_type=F32)
        u = jnp.maximum(u, 0.0)
        acc = acc + jnp.dot((u * u).astype(BF16), wout_ref[f * FF_CHUNK:(f + 1) * FF_CHUNK, :],
                            preferred_element_type=F32)
    if final:
        acc = _rms(acc, gf_ref[...])
    o_ref[0] = acc


def _mlp(x, attn, w_o, gain, w_in, w_out, gain_final, final):
    b, s, _ = x.shape
    tm = MLP_ROW_TILE
    return pl.pallas_call(
        functools.partial(_mlp_kernel, final=final),
        grid=(b, s // tm),
        in_specs=[
            pl.BlockSpec((1, tm, D_MODEL), lambda bi, i: (bi, i, 0)),
            pl.BlockSpec((1, O_GROUPS, tm, O_GROUP_WIDTH), lambda bi, i: (bi, 0, i, 0)),
            _resident((HD, D_MODEL)),
            _resident((1, D_MODEL)),
            _resident((D_MODEL, D_FF)),
            _resident((D_FF, D_MODEL)),
            _resident((1, D_MODEL)),
        ],
        out_specs=pl.BlockSpec((1, tm, D_MODEL), lambda bi, i: (bi, i, 0)),
        out_shape=jax.ShapeDtypeStruct((b, s, D_MODEL), F32),
        compiler_params=_params(2),
        name="mlp_final" if final else "mlp",
    )(x, attn, w_o, gain, w_in, w_out, gain_final)


def _qkv_b_kernel(x_ref, g_ref, wqv_t_ref, wk_ref, q_ref, k_ref, v_ref, *, tm):
    h = _rms(x_ref[0], g_ref[...]).astype(BF16)
    k = jnp.dot(h, wk_ref[...], preferred_element_type=F32).astype(BF16)
    width = NA_HEADS * HEAD_DIM
    for g in range(N_HEADS // NA_HEADS):
        k_ref[0, g] = k[:, g * width:(g + 1) * width]
    nt = (((1,), (1,)), ((), ()))
    q_t = lax.dot_general(wqv_t_ref[0:HD, :], h, nt, preferred_element_type=F32)
    q_ref[0, 0] = (q_t * NA_SCORE_SCALE).astype(BF16)
    v_t = lax.dot_general(wqv_t_ref[HD:2 * HD, :], h, nt, preferred_element_type=F32)
    for hd in range(N_HEADS):
        for c in range(tm // NA_NQ):
            v_ref[0, hd, c] = v_t[hd * HEAD_DIM:(hd + 1) * HEAD_DIM,
                                  c * NA_NQ:(c + 1) * NA_NQ].astype(BF16)


def _qkv_b(x, gain, wqv_t, wk):
    b, s, _ = x.shape
    tm = 2 * NA_NQ
    groups = N_HEADS // NA_HEADS
    width = NA_HEADS * HEAD_DIM
    return pl.pallas_call(
        functools.partial(_qkv_b_kernel, tm=tm),
        grid=(b, s // tm),
        in_specs=[
            pl.BlockSpec((1, tm, D_MODEL), lambda bi, i: (bi, i, 0)),
            _resident((1, D_MODEL)),
            _resident((2 * HD, D_MODEL)),
            _resident((D_MODEL, HD)),
        ],
        out_specs=[
            pl.BlockSpec((1, 1, HD, tm), lambda bi, i: (bi, i, 0, 0)),
            pl.BlockSpec((1, groups, tm, width), lambda bi, i: (bi, 0, i, 0)),
            pl.BlockSpec((1, N_HEADS, tm // NA_NQ, HEAD_DIM, NA_NQ), lambda bi, i: (bi, 0, i, 0, 0)),
        ],
        out_shape=[
            jax.ShapeDtypeStruct((b, s // tm, HD, tm), BF16),
            jax.ShapeDtypeStruct((b, groups, s, width), BF16),
            jax.ShapeDtypeStruct((b, N_HEADS, s // NA_NQ, HEAD_DIM, NA_NQ), BF16),
        ],
        compiler_params=_params(2),
        name="qkv_b",
    )(x, gain, wqv_t, wk)


def _natten_kernel(q_ref, qn_ref, k_ref, v_ref, b_ref, o_ref, s_ref, cm_ref, *, rows):
    u = pl.program_id(2)
    n_blocks = rows // NA_Q_ROWS
    width = NA_HEADS * HEAD_DIM
    row_id = lax.broadcasted_iota(jnp.int32, (width, NA_NQ), 0)
    ones = jnp.ones((FLASH_L_ROWS, NA_NK), BF16)

    def key_base(blk):
        return jnp.clip(NA_Q_ROWS * blk - NA_WIN_H // 2, 0, rows - NA_K_ROWS)

    def scores(blk, q_blk, slot, e):
        variant = jnp.where(blk == 0, 0, jnp.where(blk == n_blocks - 1, 2, 1))
        start = pl.multiple_of(key_base(blk) * GRID_W, NA_NQ)
        k_w = k_ref[0, 0, pl.ds(start, NA_NK), :]
        mine = (row_id >= e * HEAD_DIM) & (row_id < (e + 1) * HEAD_DIM)
        q_e = jnp.where(mine, q_blk, jnp.zeros_like(q_blk))
        s_t = jnp.dot(k_w, q_e, preferred_element_type=F32) + b_ref[variant, e]
        s_ref[slot, e] = s_t
        cm_ref[slot, e] = jnp.max(s_t, axis=0, keepdims=True)

    def attend(blk, slot, e):
        chunk0 = key_base(blk) // NA_Q_ROWS
        s_t = s_ref[slot, e]
        p = jnp.exp2(s_t - cm_ref[slot, e]).astype(BF16)
        v_t = [v_ref[0, e, chunk0 + j] for j in range(NA_K_ROWS // NA_Q_ROWS)]
        v_ext = jnp.concatenate([jnp.concatenate(v_t, axis=1), ones], axis=0)
        acc = jnp.dot(v_ext, p, preferred_element_type=F32)
        return acc[:HEAD_DIM] / acc[HEAD_DIM:HEAD_DIM + 1]

    @pl.when(u == 0)
    def _():
        for e in range(NA_HEADS):
            scores(0, q_ref[0, 0, :, 0:NA_NQ], 0, e)

    def half(blk, slot, blk_next, q_next, row0):
        outs = []
        for e in range(NA_HEADS):
            scores(blk_next, q_next, 1 - slot, e)
            outs.append(attend(blk, slot, e))
        o_ref[0, 0, row0:row0 + NA_NQ, :] = jnp.concatenate(outs, axis=0).T.astype(BF16)

    def q_block(n):
        return q_ref[0, n // 2, :, (n % 2) * NA_NQ:(n % 2 + 1) * NA_NQ]

    for n in range(NA_BLOCKS - 1):
        half(NA_BLOCKS * u + n, n % 2, NA_BLOCKS * u + n + 1, q_block(n + 1), n * NA_NQ)
    u_next = jnp.minimum(u + 1, n_blocks // NA_BLOCKS - 1)
    half(NA_BLOCKS * u + NA_BLOCKS - 1, (NA_BLOCKS - 1) % 2, NA_BLOCKS * u_next,
         qn_ref[0, 0, :, 0:NA_NQ], (NA_BLOCKS - 1) * NA_NQ)


def _natten(q_t, k, v_t, bias):
    b, _, s, _ = k.shape
    rows = s // GRID_W
    n_blocks = rows // NA_Q_ROWS
    groups = N_HEADS // NA_HEADS
    width = NA_HEADS * HEAD_DIM
    steps = n_blocks // NA_BLOCKS
    tiles = NA_BLOCKS // 2
    assert rows >= NA_K_ROWS + NA_Q_ROWS
    return pl.pallas_call(
        functools.partial(_natten_kernel, rows=rows),
        grid=(groups, b, steps),
        in_specs=[
            pl.BlockSpec((1, tiles, width, 2 * NA_NQ), lambda hg, bi, u: (bi, u, hg, 0)),
            pl.BlockSpec((1, 1, width, 2 * NA_NQ),
                         lambda hg, bi, u: (bi, tiles * jnp.minimum(u + 1, steps - 1), hg, 0)),
            pl.BlockSpec((1, 1, s, width), lambda hg, bi, u: (bi, hg, 0, 0)),
            pl.BlockSpec((1, NA_HEADS, s // NA_NQ, HEAD_DIM, NA_NQ),
                         lambda hg, bi, u: (bi, hg, 0, 0, 0)),
            pl.BlockSpec((3, NA_HEADS, NA_NK, NA_NQ), lambda hg, bi, u: (0, hg, 0, 0),
                         pipeline_mode=pl.Buffered(1)),
        ],
        out_specs=pl.BlockSpec((1, 1, NA_BLOCKS * NA_NQ, width), lambda hg, bi, u: (bi, hg, u, 0)),
        out_shape=jax.ShapeDtypeStruct((b, groups, s, width), BF16),
        scratch_shapes=[pltpu.VMEM((2, NA_HEADS, NA_NK, NA_NQ), F32),
                        pltpu.VMEM((2, NA_HEADS, 1, NA_NQ), F32)],
        compiler_params=_params(3),
        name="natten_b",
    )(q_t, q_t, k, v_t, bias)


def _rope_tables_t(seq):
    t = jnp.arange(seq)
    row = (t // GRID_W).astype(F32)
    col = (t % GRID_W).astype(F32)
    inv = ROPE_THETA ** (-jnp.arange(0, AXIS_DIM, 2, dtype=F32) / AXIS_DIM)
    ang = jnp.concatenate([inv[:, None] * row[None, :], inv[:, None] * col[None, :]], axis=0)
    return jnp.cos(ang), jnp.sin(ang)


NA_ROW_OFFSET = (NA_WIN_H - 1, NA_WIN_H - 1 - NA_WIN_H // 2,
                 NA_WIN_H - 1 - (NA_K_ROWS - NA_Q_ROWS))


def _na_window_start(variant, a):
    return (0, a, NA_K_ROWS - NA_WIN_H)[variant]


def _natten_bias_kernel(even_ref, odd_ref, mask_ref, o_ref):
    lanes = 2 * GRID_W
    lane = lax.broadcasted_iota(jnp.int32, (GRID_W, lanes), 1)
    masked = jnp.full((GRID_W, lanes), -jnp.inf, F32)
    for variant in range(3):
        for i in range(NA_K_ROWS):
            for pair in range(NA_Q_ROWS // 2):
                rows_a = (2 * pair, 2 * pair + 1)
                inside = [0 <= i - _na_window_start(variant, a) < NA_WIN_H for a in rows_a]
                blk = masked
                if any(inside):
                    base = jnp.zeros((1, lanes), F32)
                    for a, ok, src in zip(rows_a, inside, (even_ref, odd_ref)):
                        if ok:
                            d = i - a + NA_ROW_OFFSET[variant]
                            base = base + src[0, d:d + 1, :]
                    blk = pltpu.roll(jnp.broadcast_to(base, (GRID_W, lanes)), 0, 1,
                                     stride=1, stride_axis=0) + mask_ref[...]
                    if not inside[0]:
                        blk = jnp.where(lane < GRID_W, masked, blk)
                    if not inside[1]:
                        blk = jnp.where(lane >= GRID_W, masked, blk)
                o_ref[variant, 0, i * GRID_W:(i + 1) * GRID_W, pair * lanes:(pair + 1) * lanes] = blk


def _natten_bias(rel_bias):
    table = rel_bias.astype(F32) * LOG2_E
    lanes = 2 * GRID_W
    m = jnp.arange(lanes)
    reach = NA_WIN_W - 1
    col_even = jnp.where(m < GRID_W, reach - m, reach - (m - lanes))
    col_odd = reach - (m - GRID_W)

    def place(col, keep):
        vals = jnp.take(table, jnp.clip(col, 0, NA_BIAS_W - 1), axis=2)
        vals = jnp.where((keep & (col >= 0) & (col < NA_BIAS_W))[None, None, :], vals, 0.0)
        return jnp.pad(vals, ((0, 0), (0, 1), (0, 0)))

    even = place(col_even, (m <= reach) | (m >= lanes - reach))
    odd = place(col_odd, (m >= GRID_W - reach) & (m <= GRID_W + reach))
    c = jnp.arange(GRID_W)[None, :]
    kc = jnp.arange(GRID_W)[:, None]
    c0 = jnp.clip(c - NA_WIN_W // 2, 0, GRID_W - NA_WIN_W)
    valid_c = (kc >= c0) & (kc < c0 + NA_WIN_W)
    mask = jnp.where(jnp.concatenate([valid_c, valid_c], axis=1), 0.0, -jnp.inf).astype(F32)
    return pl.pallas_call(
        _natten_bias_kernel,
        grid=(N_HEADS,),
        in_specs=[
            pl.BlockSpec((1, NA_BIAS_H + 1, lanes), lambda h: (h, 0, 0)),
            pl.BlockSpec((1, NA_BIAS_H + 1, lanes), lambda h: (h, 0, 0)),
            pl.BlockSpec((GRID_W, lanes), lambda h: (0, 0)),
        ],
        out_specs=pl.BlockSpec((3, 1, NA_NK, NA_NQ), lambda h: (0, h, 0, 0)),
        out_shape=jax.ShapeDtypeStruct((3, N_HEADS, NA_NK, NA_NQ), F32),
        compiler_params=_params(1),
        name="natten_bias",
    )(even, odd, mask)


def _trunk(x, p):
    _, s, _ = x.shape
    depth = p["norm_mix"].shape[0]
    for i in range(depth):
        j = i // 2
        g_mix = p["norm_mix"][i][None, :]
        if i % 2 == 0:
            q_t, k, v_t = _qkv_a(x, g_mix, p["a_w_qkv_t"][j], p["a_q_gain"][j], p["a_k_gain"][j],
                                 p["cos_t"], p["sin_t"])
            o = _flash(q_t, k, v_t)
            w_o = p["a_w_o"][j]
        else:
            q_t, k, v_t = _qkv_b(x, g_mix, p["b_w_qv_t"][j], p["b_w_k"][j])
            o = _natten(q_t, k, v_t, p["b_bias"][j])
            w_o = p["b_w_o"][j]
        x = _mlp(x, o, w_o, p["norm_mlp"][i][None, :], p["mlp_w_in"][i], p["mlp_w_out"][i],
                 p["norm_final"][None, :], final=(i == depth - 1))
    return x


def kernel(x_prompt, x_sample, norm_mix, norm_mlp, norm_final, a_w_qkv, a_q_norm, a_k_norm, a_w_o,
           b_w_qkv, b_rel_bias, b_w_o, mlp_w_in, mlp_w_out):
    scale = HEAD_DIM ** -0.5 * LOG2_E
    q_bound = jnp.max(jnp.abs(a_q_norm), axis=1) * scale
    k_bound = jnp.max(jnp.abs(a_k_norm), axis=1)
    ratio = k_bound / q_bound
    ratio = jnp.where(jnp.isfinite(ratio) & (ratio > 0), ratio, 1.0)
    balance = jnp.exp2(jnp.round(0.5 * jnp.log2(ratio)))[:, None]
    shared = {
        "norm_mix": norm_mix, "norm_mlp": norm_mlp, "norm_final": norm_final,
        "a_w_qkv_t": jnp.swapaxes(a_w_qkv, 1, 2).astype(BF16),
        "a_q_gain": jnp.broadcast_to((a_q_norm * scale * balance)[:, :, None],
                                     a_q_norm.shape + (ROW_TILE,)),
        "a_k_gain": jnp.broadcast_to((a_k_norm / balance)[:, :, None],
                                     a_k_norm.shape + (ROW_TILE,)),
        "a_w_o": a_w_o.astype(BF16),
        "b_w_qv_t": jnp.swapaxes(jnp.concatenate([b_w_qkv[:, :, :HD], b_w_qkv[:, :, 2 * HD:]],
                                                 axis=2), 1, 2).astype(BF16),
        "b_w_k": b_w_qkv[:, :, HD:2 * HD].astype(BF16),
        "b_w_o": b_w_o.astype(BF16),
        "mlp_w_in": mlp_w_in.astype(BF16),
        "mlp_w_out": mlp_w_out.astype(BF16),
        "b_bias": [_natten_bias(rb) for rb in b_rel_bias],
    }
    outs = []
    for x in (x_prompt, x_sample):
        cos_t, sin_t = _rope_tables_t(x.shape[1])
        outs.append(_trunk(x, dict(shared, cos_t=cos_t, sin_t=sin_t)))
    return tuple(outs)
```

```python
import functools

import jax
import jax.numpy as jnp
from jax import lax
from jax.experimental import pallas as pl
from jax.experimental.pallas import tpu as pltpu

D_MODEL = 1024
GRID_W = 64
N_HEADS = 16
HEAD_DIM = 64
A_KV_HEADS = 4
A_GROUP = N_HEADS // A_KV_HEADS
ROPE_THETA = 10000.0
AXIS_DIM = HEAD_DIM // 2
ROPE_HALF = AXIS_DIM // 2
NA_WIN_H = 8
NA_WIN_W = 16
NA_BIAS_H = 2 * NA_WIN_H - 1
NA_BIAS_W = 2 * NA_WIN_W - 1
D_FF = 4 * D_MODEL
NORM_EPS = 1e-6
LOG2_E = 1.4426950408889634
NA_SCORE_SCALE = HEAD_DIM ** -0.5 * LOG2_E
HD = N_HEADS * HEAD_DIM
A_QKV = (N_HEADS + 2 * A_KV_HEADS) * HEAD_DIM

ROW_TILE = 1024
QKV_B_ROW_TILE = 1024
MLP_ROW_TILE = 1024
FLASH_TQ = 256
FLASH_Q_TILES = 2
FLASH_TK = 512
FLASH_HEADS = 4
FLASH_L_ROWS = 16
NA_Q_ROWS = 4
NA_K_ROWS = 12
NA_HEADS = 4
NA_BLOCKS = 8
O_GROUPS = 4
O_GROUP_WIDTH = 256
NA_NQ = NA_Q_ROWS * GRID_W
NA_Q_TILE = 2 * NA_NQ
NA_NK = NA_K_ROWS * GRID_W
FF_CHUNK = 1024
VMEM_LIMIT_BYTES = 56 * 1024 * 1024

BF16 = jnp.bfloat16
F32 = jnp.float32
F8 = jnp.float8_e4m3fn
F8_MAX = 448.0
QK_DEPTH = 4 * HEAD_DIM


def _params(n_axes):
    return pltpu.CompilerParams(
        dimension_semantics=("arbitrary",) * n_axes, vmem_limit_bytes=VMEM_LIMIT_BYTES)


def _resident(shape):
    zeros = (0,) * len(shape)
    return pl.BlockSpec(shape, lambda *_: zeros, pipeline_mode=pl.Buffered(1))


def _split_f8(x):
    x = jnp.clip(x, -F8_MAX, F8_MAX)
    hi = x.astype(F8)
    lo = (x - hi.astype(F32)).astype(F8)
    return hi, lo


def _rms(x, gain):
    ms = jnp.mean(x * x, axis=-1, keepdims=True)
    return x * lax.rsqrt(ms + NORM_EPS) * gain


def _qkv_a_kernel(x_ref, g_ref, wt_ref, qg_ref, kg_ref, cos_ref, sin_ref,
                  q_ref, k_ref, v_ref, s_ref, *, tm):
    h = _rms(x_ref[0], g_ref[...]).astype(BF16)
    qkv_t = lax.dot_general(wt_ref[...], h, (((1,), (1,)), ((), ())),
                            preferred_element_type=F32)
    n_slots = N_HEADS + 2 * A_KV_HEADS
    s_ref[...] = qkv_t.reshape(n_slots, HEAD_DIM, tm)

    def norm_rope(lo, hi, gain_ref, store):
        t = s_ref[lo:hi]
        r = lax.rsqrt(jnp.sum(t * t, axis=1, keepdims=True) * (1.0 / HEAD_DIM) + NORM_EPS)
        for a in range(2):
            d1 = a * AXIS_DIM
            d2 = d1 + ROPE_HALF
            x1 = s_ref[lo:hi, d1:d1 + ROPE_HALF, :] * r * gain_ref[d1:d1 + ROPE_HALF, :]
            x2 = s_ref[lo:hi, d2:d2 + ROPE_HALF, :] * r * gain_ref[d2:d2 + ROPE_HALF, :]
            c = cos_ref[a * ROPE_HALF:(a + 1) * ROPE_HALF, :]
            s = sin_ref[a * ROPE_HALF:(a + 1) * ROPE_HALF, :]
            store(d1, x1 * c - x2 * s)
            store(d2, x2 * c + x1 * s)

    def store_q(d, val):
        s_ref[0:N_HEADS, d:d + ROPE_HALF, :] = val

    def store_k(d, val):
        s_ref[N_HEADS:N_HEADS + A_KV_HEADS, d:d + ROPE_HALF, :] = val

    norm_rope(0, N_HEADS, qg_ref, store_q)
    norm_rope(N_HEADS, N_HEADS + A_KV_HEADS, kg_ref, store_k)
    q_hi, q_lo = _split_f8(s_ref[0:N_HEADS])
    for part, val in enumerate((q_hi, q_hi, q_lo, q_lo)):
        q_ref[0, :, part * HEAD_DIM:(part + 1) * HEAD_DIM, :] = val
    for g in range(A_KV_HEADS):
        k_hi, k_lo = _split_f8(s_ref[N_HEADS + g])
        k_hi, k_lo = k_hi.astype(F32), k_lo.astype(F32)
        k_ext_t = jnp.concatenate([k_hi, k_lo, k_hi, k_lo], axis=0)
        k_ref[0, g] = k_ext_t.T.astype(F8)
        for c in range(tm // FLASH_TK):
            v_ref[0, g, c] = s_ref[N_HEADS + A_KV_HEADS + g, :,
                                   c * FLASH_TK:(c + 1) * FLASH_TK].astype(BF16)


def _qkv_a(x, gain, w_t, q_gain, k_gain, cos_t, sin_t):
    b, s, _ = x.shape
    tm = ROW_TILE
    n_slots = N_HEADS + 2 * A_KV_HEADS
    return pl.pallas_call(
        functools.partial(_qkv_a_kernel, tm=tm),
        grid=(b, s // tm),
        in_specs=[
            pl.BlockSpec((1, tm, D_MODEL), lambda bi, i: (bi, i, 0)),
            _resident((1, D_MODEL)),
            _resident((A_QKV, D_MODEL)),
            _resident((HEAD_DIM, tm)),
            _resident((HEAD_DIM, tm)),
            pl.BlockSpec((AXIS_DIM, tm), lambda bi, i: (0, i)),
            pl.BlockSpec((AXIS_DIM, tm), lambda bi, i: (0, i)),
        ],
        out_specs=[
            pl.BlockSpec((1, N_HEADS, QK_DEPTH, tm), lambda bi, i: (bi, 0, 0, i)),
            pl.BlockSpec((1, A_KV_HEADS, tm, QK_DEPTH), lambda bi, i: (bi, 0, i, 0)),
            pl.BlockSpec((1, A_KV_HEADS, tm // FLASH_TK, HEAD_DIM, FLASH_TK),
                         lambda bi, i: (bi, 0, i, 0, 0)),
        ],
        out_shape=[
            jax.ShapeDtypeStruct((b, N_HEADS, QK_DEPTH, s), F8),
            jax.ShapeDtypeStruct((b, A_KV_HEADS, s, QK_DEPTH), F8),
            jax.ShapeDtypeStruct((b, A_KV_HEADS, s // FLASH_TK, HEAD_DIM, FLASH_TK), BF16),
        ],
        scratch_shapes=[pltpu.VMEM((n_slots, HEAD_DIM, tm), F32)],
        compiler_params=_params(2),
        name="qkv_a",
    )(x, gain, w_t, q_gain, k_gain, cos_t, sin_t)


def _flash_kernel(q_ref, k_ref, v_ref, qn_ref, kn_ref, o_ref, s_ref, cm_ref, *, n_chunks, tq):
    first = (pl.program_id(0) == 0) & (pl.program_id(1) == 0) & (pl.program_id(2) == 0)

    def scores(c, slot, e, j, q_src=q_ref, k_src=k_ref):
        k_c = k_src[0, 0, c * FLASH_TK:(c + 1) * FLASH_TK, :]
        q_e = q_src[0, e, :, j * tq:(j + 1) * tq]
        s_t = jnp.dot(k_c, q_e, preferred_element_type=F32)
        s_ref[slot, e] = s_t
        cm_ref[slot, e] = jnp.max(s_t, axis=0, keepdims=True)

    ones = jnp.ones((FLASH_L_ROWS, FLASH_TK), BF16)

    def update(c, slot, e, stat):
        m, acc = stat
        s_t = s_ref[slot, e]
        m_new = jnp.maximum(m, cm_ref[slot, e])
        alpha = jnp.exp2(m - m_new)
        p = jnp.exp2(s_t - m_new).astype(BF16)
        v_ext = jnp.concatenate([v_ref[0, 0, c], ones], axis=0)
        acc = alpha * acc + jnp.dot(v_ext, p, preferred_element_type=F32)
        return m_new, acc

    @pl.when(first)
    def _():
        for e in range(FLASH_HEADS):
            scores(0, 0, e, 0)

    for j in range(FLASH_Q_TILES):
        stats = tuple((jnp.full((1, tq), -jnp.inf, F32),
                       jnp.zeros((HEAD_DIM + FLASH_L_ROWS, tq), F32)) for _ in range(FLASH_HEADS))
        for c in range(n_chunks):
            new = []
            for e in range(FLASH_HEADS):
                if c + 1 < n_chunks:
                    scores(c + 1, (c + 1) % 2, e, j)
                elif j + 1 < FLASH_Q_TILES:
                    scores(0, 0, e, j + 1)
                else:
                    scores(0, 0, e, 0, qn_ref, kn_ref)
                new.append(update(c, c % 2, e, stats[e]))
            stats = tuple(new)
        o_t = jnp.concatenate([acc[:HEAD_DIM] / acc[HEAD_DIM:HEAD_DIM + 1] for (_, acc) in stats],
                              axis=0)
        o_ref[0, 0, j * tq:(j + 1) * tq, :] = o_t.T.astype(BF16)


def _flash(q_t, k, v_t):
    b, _, _, s = q_t.shape
    tq = FLASH_TQ
    n_chunks = s // FLASH_TK
    pairs = N_HEADS // FLASH_HEADS
    per_kv = A_GROUP // FLASH_HEADS
    tq_step = tq * FLASH_Q_TILES
    n_q = s // tq_step
    assert n_chunks % 2 == 0

    def following(bi, hp, i):
        flat = jnp.minimum((bi * pairs + hp) * n_q + i + 1, b * pairs * n_q - 1)
        return flat // (pairs * n_q), (flat // n_q) % pairs, flat % n_q

    def q_next(bi, hp, i):
        nb, nh, ni = following(bi, hp, i)
        return nb, nh, 0, ni

    def k_next(bi, hp, i):
        nb, nh, _ = following(bi, hp, i)
        return nb, nh // per_kv, 0, 0

    return pl.pallas_call(
        functools.partial(_flash_kernel, n_chunks=n_chunks, tq=tq),
        grid=(b, pairs, n_q),
        in_specs=[
            pl.BlockSpec((1, FLASH_HEADS, QK_DEPTH, tq_step), lambda bi, hp, i: (bi, hp, 0, i)),
            pl.BlockSpec((1, 1, s, QK_DEPTH), lambda bi, hp, i: (bi, hp // per_kv, 0, 0)),
            pl.BlockSpec((1, 1, n_chunks, HEAD_DIM, FLASH_TK),
                         lambda bi, hp, i: (bi, hp // per_kv, 0, 0, 0)),
            pl.BlockSpec((1, FLASH_HEADS, QK_DEPTH, tq_step), q_next),
            pl.BlockSpec((1, 1, s, QK_DEPTH), k_next),
        ],
        out_specs=pl.BlockSpec((1, 1, tq_step, O_GROUP_WIDTH), lambda bi, hp, i: (bi, hp, i, 0)),
        out_shape=jax.ShapeDtypeStruct((b, O_GROUPS, s, O_GROUP_WIDTH), BF16),
        scratch_shapes=[pltpu.VMEM((2, FLASH_HEADS, FLASH_TK, tq), F32),
                        pltpu.VMEM((2, FLASH_HEADS, 1, tq), F32)],
        compiler_params=_params(3),
        name="flash_a",
    )(q_t, k, v_t, q_t, k)


def _mlp_kernel(x_ref, a_ref, wo_ref, g_ref, win_ref, wout_ref, gf_ref, o_ref, *, final):
    a = jnp.concatenate([a_ref[0, g] for g in range(O_GROUPS)], axis=1)
    x = x_ref[0] + jnp.dot(a, wo_ref[...], preferred_element_type=F32)
    h = _rms(x, g_ref[...]).astype(BF16)
    acc = x
    for f in range(D_FF // FF_CHUNK):
        u = jnp.dot(h, win_ref[:, f * FF_CHUNK:(f + 1) * FF_CHUNK], preferred_element_type=F32)
        u = jnp.maximum(u, 0.0)
        acc = acc + jnp.dot((u * u).astype(BF16), wout_ref[f * FF_CHUNK:(f + 1) * FF_CHUNK, :],
                            preferred_element_type=F32)
    if final:
        acc = _rms(acc, gf_ref[...])
    o_ref[0] = acc


def _mlp(x, attn, w_o, gain, w_in, w_out, gain_final, final):
    b, s, _ = x.shape
    tm = MLP_ROW_TILE
    return pl.pallas_call(
        functools.partial(_mlp_kernel, final=final),
        grid=(b, s // tm),
        in_specs=[
            pl.BlockSpec((1, tm, D_MODEL), lambda bi, i: (bi, i, 0)),
            pl.BlockSpec((1, O_GROUPS, tm, O_GROUP_WIDTH), lambda bi, i: (bi, 0, i, 0)),
            _resident((HD, D_MODEL)),
            _resident((1, D_MODEL)),
            _resident((D_MODEL, D_FF)),
            _resident((D_FF, D_MODEL)),
            _resident((1, D_MODEL)),
        ],
        out_specs=pl.BlockSpec((1, tm, D_MODEL), lambda bi, i: (bi, i, 0)),
        out_shape=jax.ShapeDtypeStruct((b, s, D_MODEL), F32),
        compiler_params=_params(2),
        name="mlp_final" if final else "mlp",
    )(x, attn, w_o, gain, w_in, w_out, gain_final)


def _qkv_b_kernel(x_ref, g_ref, wqv_t_ref, wk_ref, q_ref, k_ref, v_ref, *, tm):
    h = _rms(x_ref[0], g_ref[...]).astype(BF16)
    k = jnp.dot(h, wk_ref[...], preferred_element_type=F32).astype(BF16)
    width = NA_HEADS * HEAD_DIM
    for g in range(N_HEADS // NA_HEADS):
        k_ref[0, g] = k[:, g * width:(g + 1) * width]
    nt = (((1,), (1,)), ((), ()))
    q_t = lax.dot_general(wqv_t_ref[0:HD, :], h, nt, preferred_element_type=F32)
    for t in range(tm // NA_Q_TILE):
        q_ref[0, t] = (q_t[:, t * NA_Q_TILE:(t + 1) * NA_Q_TILE] * NA_SCORE_SCALE).astype(BF16)
    v_t = lax.dot_general(wqv_t_ref[HD:2 * HD, :], h, nt, preferred_element_type=F32)
    for hd in range(N_HEADS):
        for c in range(tm // NA_NQ):
            v_ref[0, hd, c] = v_t[hd * HEAD_DIM:(hd + 1) * HEAD_DIM,
                                  c * NA_NQ:(c + 1) * NA_NQ].astype(BF16)


def _qkv_b(x, gain, wqv_t, wk):
    b, s, _ = x.shape
    tm = QKV_B_ROW_TILE
    groups = N_HEADS // NA_HEADS
    width = NA_HEADS * HEAD_DIM
    return pl.pallas_call(
        functools.partial(_qkv_b_kernel, tm=tm),
        grid=(b, s // tm),
        in_specs=[
            pl.BlockSpec((1, tm, D_MODEL), lambda bi, i: (bi, i, 0)),
            _resident((1, D_MODEL)),
            _resident((2 * HD, D_MODEL)),
            _resident((D_MODEL, HD)),
        ],
        out_specs=[
            pl.BlockSpec((1, tm // NA_Q_TILE, HD, NA_Q_TILE), lambda bi, i: (bi, i, 0, 0)),
            pl.BlockSpec((1, groups, tm, width), lambda bi, i: (bi, 0, i, 0)),
            pl.BlockSpec((1, N_HEADS, tm // NA_NQ, HEAD_DIM, NA_NQ), lambda bi, i: (bi, 0, i, 0, 0)),
        ],
        out_shape=[
            jax.ShapeDtypeStruct((b, s // NA_Q_TILE, HD, NA_Q_TILE), BF16),
            jax.ShapeDtypeStruct((b, groups, s, width), BF16),
            jax.ShapeDtypeStruct((b, N_HEADS, s // NA_NQ, HEAD_DIM, NA_NQ), BF16),
        ],
        compiler_params=_params(2),
        name="qkv_b",
    )(x, gain, wqv_t, wk)


def _natten_kernel(q_ref, qn_ref, k_ref, v_ref, b_ref, o_ref, s_ref, cm_ref, *, rows):
    u = pl.program_id(2)
    n_blocks = rows // NA_Q_ROWS
    width = NA_HEADS * HEAD_DIM
    row_id = lax.broadcasted_iota(jnp.int32, (width, NA_NQ), 0)
    ones = jnp.ones((FLASH_L_ROWS, NA_NK), BF16)

    def key_base(blk):
        return jnp.clip(NA_Q_ROWS * blk - NA_WIN_H // 2, 0, rows - NA_K_ROWS)

    def scores(blk, q_blk, slot, e):
        variant = jnp.where(blk == 0, 0, jnp.where(blk == n_blocks - 1, 2, 1))
        start = pl.multiple_of(key_base(blk) * GRID_W, NA_NQ)
        k_w = k_ref[0, 0, pl.ds(start, NA_NK), :]
        mine = (row_id >= e * HEAD_DIM) & (row_id < (e + 1) * HEAD_DIM)
        q_e = jnp.where(mine, q_blk, jnp.zeros_like(q_blk))
        s_t = jnp.dot(k_w, q_e, preferred_element_type=F32) + b_ref[variant, e]
        s_ref[slot, e] = s_t
        cm_ref[slot, e] = jnp.max(s_t, axis=0, keepdims=True)

    def attend(blk, slot, e):
        chunk0 = key_base(blk) // NA_Q_ROWS
        s_t = s_ref[slot, e]
        p = jnp.exp2(s_t - cm_ref[slot, e]).astype(BF16)
        v_t = [v_ref[0, e, chunk0 + j] for j in range(NA_K_ROWS // NA_Q_ROWS)]
        v_ext = jnp.concatenate([jnp.concatenate(v_t, axis=1), ones], axis=0)
        acc = jnp.dot(v_ext, p, preferred_element_type=F32)
        return acc[:HEAD_DIM] / acc[HEAD_DIM:HEAD_DIM + 1]

    @pl.when(u == 0)
    def _():
        for e in range(NA_HEADS):
            scores(0, q_ref[0, 0, :, 0:NA_NQ], 0, e)

    def half(blk, slot, blk_next, q_next, row0):
        outs = []
        for e in range(NA_HEADS):
            scores(blk_next, q_next, 1 - slot, e)
            outs.append(attend(blk, slot, e))
        o_ref[0, 0, row0:row0 + NA_NQ, :] = jnp.concatenate(outs, axis=0).T.astype(BF16)

    def q_block(n):
        return q_ref[0, n // 2, :, (n % 2) * NA_NQ:(n % 2 + 1) * NA_NQ]

    for n in range(NA_BLOCKS - 1):
        half(NA_BLOCKS * u + n, n % 2, NA_BLOCKS * u + n + 1, q_block(n + 1), n * NA_NQ)
    u_next = jnp.minimum(u + 1, n_blocks // NA_BLOCKS - 1)
    half(NA_BLOCKS * u + NA_BLOCKS - 1, (NA_BLOCKS - 1) % 2, NA_BLOCKS * u_next,
         qn_ref[0, 0, :, 0:NA_NQ], (NA_BLOCKS - 1) * NA_NQ)


def _natten(q_t, k, v_t, bias):
    b, _, s, _ = k.shape
    rows = s // GRID_W
    n_blocks = rows // NA_Q_ROWS
    groups = N_HEADS // NA_HEADS
    width = NA_HEADS * HEAD_DIM
    steps = n_blocks // NA_BLOCKS
    tiles = NA_BLOCKS // 2
    assert rows >= NA_K_ROWS + NA_Q_ROWS
    return pl.pallas_call(
        functools.partial(_natten_kernel, rows=rows),
        grid=(groups, b, steps),
        in_specs=[
            pl.BlockSpec((1, tiles, width, 2 * NA_NQ), lambda hg, bi, u: (bi, u, hg, 0)),
            pl.BlockSpec((1, 1, width, 2 * NA_NQ),
                         lambda hg, bi, u: (bi, tiles * jnp.minimum(u + 1, steps - 1), hg, 0)),
            pl.BlockSpec((1, 1, s, width), lambda hg, bi, u: (bi, hg, 0, 0)),
            pl.BlockSpec((1, NA_HEADS, s // NA_NQ, HEAD_DIM, NA_NQ),
                         lambda hg, bi, u: (bi, hg, 0, 0, 0)),
            pl.BlockSpec((3, NA_HEADS, NA_NK, NA_NQ), lambda hg, bi, u: (0, hg, 0, 0),
                         pipeline_mode=pl.Buffered(1)),
        ],
        out_specs=pl.BlockSpec((1, 1, NA_BLOCKS * NA_NQ, width), lambda hg, bi, u: (bi, hg, u, 0)),
        out_shape=jax.ShapeDtypeStruct((b, groups, s, width), BF16),
        scratch_shapes=[pltpu.VMEM((2, NA_HEADS, NA_NK, NA_NQ), F32),
                        pltpu.VMEM((2, NA_HEADS, 1, NA_NQ), F32)],
        compiler_params=_params(3),
        name="natten_b",
    )(q_t, q_t, k, v_t, bias)


def _rope_tables_t(seq):
    t = jnp.arange(seq)
    row = (t // GRID_W).astype(F32)
    col = (t % GRID_W).astype(F32)
    inv = ROPE_THETA ** (-jnp.arange(0, AXIS_DIM, 2, dtype=F32) / AXIS_DIM)
    ang = jnp.concatenate([inv[:, None] * row[None, :], inv[:, None] * col[None, :]], axis=0)
    return jnp.cos(ang), jnp.sin(ang)


NA_ROW_OFFSET = (NA_WIN_H - 1, NA_WIN_H - 1 - NA_WIN_H // 2,
                 NA_WIN_H - 1 - (NA_K_ROWS - NA_Q_ROWS))


def _na_window_start(variant, a):
    return (0, a, NA_K_ROWS - NA_WIN_H)[variant]


def _natten_bias_kernel(even_ref, odd_ref, mask_ref, o_ref):
    lanes = 2 * GRID_W
    lane = lax.broadcasted_iota(jnp.int32, (GRID_W, lanes), 1)
    masked = jnp.full((GRID_W, lanes), -jnp.inf, F32)
    for variant in range(3):
        for i in range(NA_K_ROWS):
            for pair in range(NA_Q_ROWS // 2):
                rows_a = (2 * pair, 2 * pair + 1)
                inside = [0 <= i - _na_window_start(variant, a) < NA_WIN_H for a in rows_a]
                blk = masked
                if any(inside):
                    base = jnp.zeros((1, lanes), F32)
                    for a, ok, src in zip(rows_a, inside, (even_ref, odd_ref)):
                        if ok:
                            d = i - a + NA_ROW_OFFSET[variant]
                            base = base + src[0, d:d + 1, :]
                    blk = pltpu.roll(jnp.broadcast_to(base, (GRID_W, lanes)), 0, 1,
                                     stride=1, stride_axis=0) + mask_ref[...]
                    if not inside[0]:
                        blk = jnp.where(lane < GRID_W, masked, blk)
                    if not inside[1]:
                        blk = jnp.where(lane >= GRID_W, masked, blk)
                o_ref[variant, 0, i * GRID_W:(i + 1) * GRID_W, pair * lanes:(pair + 1) * lanes] = blk


def _natten_bias(rel_bias):
    table = rel_bias.astype(F32) * LOG2_E
    lanes = 2 * GRID_W
    m = jnp.arange(lanes)
    reach = NA_WIN_W - 1
    col_even = jnp.where(m < GRID_W, reach - m, reach - (m - lanes))
    col_odd = reach - (m - GRID_W)

    def place(col, keep):
        vals = jnp.take(table, jnp.clip(col, 0, NA_BIAS_W - 1), axis=2)
        vals = jnp.where((keep & (col >= 0) & (col < NA_BIAS_W))[None, None, :], vals, 0.0)
        return jnp.pad(vals, ((0, 0), (0, 1), (0, 0)))

    even = place(col_even, (m <= reach) | (m >= lanes - reach))
    odd = place(col_odd, (m >= GRID_W - reach) & (m <= GRID_W + reach))
    c = jnp.arange(GRID_W)[None, :]
    kc = jnp.arange(GRID_W)[:, None]
    c0 = jnp.clip(c - NA_WIN_W // 2, 0, GRID_W - NA_WIN_W)
    valid_c = (kc >= c0) & (kc < c0 + NA_WIN_W)
    mask = jnp.where(jnp.concatenate([valid_c, valid_c], axis=1), 0.0, -jnp.inf).astype(F32)
    return pl.pallas_call(
        _natten_bias_kernel,
        grid=(N_HEADS,),
        in_specs=[
            pl.BlockSpec((1, NA_BIAS_H + 1, lanes), lambda h: (h, 0, 0)),
            pl.BlockSpec((1, NA_BIAS_H + 1, lanes), lambda h: (h, 0, 0)),
            pl.BlockSpec((GRID_W, lanes), lambda h: (0, 0)),
        ],
        out_specs=pl.BlockSpec((3, 1, NA_NK, NA_NQ), lambda h: (0, h, 0, 0)),
        out_shape=jax.ShapeDtypeStruct((3, N_HEADS, NA_NK, NA_NQ), F32),
        compiler_params=_params(1),
        name="natten_bias",
    )(even, odd, mask)


def _trunk(x, p):
    _, s, _ = x.shape
    depth = p["norm_mix"].shape[0]
    for i in range(depth):
        j = i // 2
        g_mix = p["norm_mix"][i][None, :]
        if i % 2 == 0:
            q_t, k, v_t = _qkv_a(x, g_mix, p["a_w_qkv_t"][j], p["a_q_gain"][j], p["a_k_gain"][j],
                                 p["cos_t"], p["sin_t"])
            o = _flash(q_t, k, v_t)
            w_o = p["a_w_o"][j]
        else:
            q_t, k, v_t = _qkv_b(x, g_mix, p["b_w_qv_t"][j], p["b_w_k"][j])
            o = _natten(q_t, k, v_t, p["b_bias"][j])
            w_o = p["b_w_o"][j]
        x = _mlp(x, o, w_o, p["norm_mlp"][i][None, :], p["mlp_w_in"][i], p["mlp_w_out"][i],
                 p["norm_final"][None, :], final=(i == depth - 1))
    return x


def kernel(x_prompt, x_sample, norm_mix, norm_mlp, norm_final, a_w_qkv, a_q_norm, a_k_norm, a_w_o,
           b_w_qkv, b_rel_bias, b_w_o, mlp_w_in, mlp_w_out):
    scale = HEAD_DIM ** -0.5 * LOG2_E
    q_bound = jnp.max(jnp.abs(a_q_norm), axis=1) * scale
    k_bound = jnp.max(jnp.abs(a_k_norm), axis=1)
    ratio = k_bound / q_bound
    ratio = jnp.where(jnp.isfinite(ratio) & (ratio > 0), ratio, 1.0)
    balance = jnp.exp2(jnp.round(0.5 * jnp.log2(ratio)))[:, None]
    shared = {
        "norm_mix": norm_mix, "norm_mlp": norm_mlp, "norm_final": norm_final,
        "a_w_qkv_t": jnp.swapaxes(a_w_qkv, 1, 2).astype(BF16),
        "a_q_gain": jnp.broadcast_to((a_q_norm * scale * balance)[:, :, None],
                                     a_q_norm.shape + (ROW_TILE,)),
        "a_k_gain": jnp.broadcast_to((a_k_norm / balance)[:, :, None],
                                     a_k_norm.shape + (ROW_TILE,)),
        "a_w_o": a_w_o.astype(BF16),
        "b_w_qv_t": jnp.swapaxes(jnp.concatenate([b_w_qkv[:, :, :HD], b_w_qkv[:, :, 2 * HD:]],
                                                 axis=2), 1, 2).astype(BF16),
        "b_w_k": b_w_qkv[:, :, HD:2 * HD].astype(BF16),
        "b_w_o": b_w_o.astype(BF16),
        "mlp_w_in": mlp_w_in.astype(BF16),
        "mlp_w_out": mlp_w_out.astype(BF16),
        "b_bias": [_natten_bias(rb) for rb in b_rel_bias],
    }
    outs = []
    for x in (x_prompt, x_sample):
        cos_t, sin_t = _rope_tables_t(x.shape[1])
        outs.append(_trunk(x, dict(shared, cos_t=cos_t, sin_t=sin_t)))
    return tuple(outs)
```

```python
import functools

import jax
import jax.numpy as jnp
from jax import lax
from jax.experimental import pallas as pl
from jax.experimental.pallas import tpu as pltpu

D_MODEL = 1024
GRID_W = 64
N_HEADS = 16
HEAD_DIM = 64
A_KV_HEADS = 4
A_GROUP = N_HEADS // A_KV_HEADS
ROPE_THETA = 10000.0
AXIS_DIM = HEAD_DIM // 2
ROPE_HALF = AXIS_DIM // 2
NA_WIN_H = 8
NA_WIN_W = 16
NA_BIAS_H = 2 * NA_WIN_H - 1
NA_BIAS_W = 2 * NA_WIN_W - 1
D_FF = 4 * D_MODEL
NORM_EPS = 1e-6
LOG2_E = 1.4426950408889634
NA_SCORE_SCALE = HEAD_DIM ** -0.5 * LOG2_E
HD = N_HEADS * HEAD_DIM
A_QKV = (N_HEADS + 2 * A_KV_HEADS) * HEAD_DIM

ROW_TILE = 1024
QKV_B_ROW_TILE = 1024
MLP_ROW_TILE = 1024
FLASH_TQ = 256
FLASH_Q_TILES = 2
FLASH_TK = 512
FLASH_HEADS = 4
FLASH_L_ROWS = 16
NA_Q_ROWS = 4
NA_K_ROWS = 12
NA_HEADS = 4
NA_BLOCKS = 16
O_GROUPS = 4
O_GROUP_WIDTH = 256
NA_NQ = NA_Q_ROWS * GRID_W
NA_Q_TILE = 2 * NA_NQ
NA_NK = NA_K_ROWS * GRID_W
FF_CHUNK = 1024
VMEM_LIMIT_BYTES = 56 * 1024 * 1024

BF16 = jnp.bfloat16
F32 = jnp.float32
F8 = jnp.float8_e4m3fn
F8_MAX = 448.0
QK_DEPTH = 4 * HEAD_DIM


def _params(n_axes):
    return pltpu.CompilerParams(
        dimension_semantics=("arbitrary",) * n_axes, vmem_limit_bytes=VMEM_LIMIT_BYTES)


def _resident(shape):
    zeros = (0,) * len(shape)
    return pl.BlockSpec(shape, lambda *_: zeros, pipeline_mode=pl.Buffered(1))


def _split_f8(x):
    x = jnp.clip(x, -F8_MAX, F8_MAX)
    hi = x.astype(F8)
    lo = (x - hi.astype(F32)).astype(F8)
    return hi, lo


def _rms(x, gain):
    ms = jnp.mean(x * x, axis=-1, keepdims=True)
    return x * lax.rsqrt(ms + NORM_EPS) * gain


def _qkv_a_kernel(x_ref, g_ref, wt_ref, qg_ref, kg_ref, cos_ref, sin_ref,
                  q_ref, k_ref, v_ref, s_ref, *, tm):
    h = _rms(x_ref[0], g_ref[...]).astype(BF16)
    qkv_t = lax.dot_general(wt_ref[...], h, (((1,), (1,)), ((), ())),
                            preferred_element_type=F32)
    n_slots = N_HEADS + 2 * A_KV_HEADS
    s_ref[...] = qkv_t.reshape(n_slots, HEAD_DIM, tm)

    def norm_rope(lo, hi, gain_ref, store):
        t = s_ref[lo:hi]
        r = lax.rsqrt(jnp.sum(t * t, axis=1, keepdims=True) * (1.0 / HEAD_DIM) + NORM_EPS)
        for a in range(2):
            d1 = a * AXIS_DIM
            d2 = d1 + ROPE_HALF
            x1 = s_ref[lo:hi, d1:d1 + ROPE_HALF, :] * r * gain_ref[d1:d1 + ROPE_HALF, :]
            x2 = s_ref[lo:hi, d2:d2 + ROPE_HALF, :] * r * gain_ref[d2:d2 + ROPE_HALF, :]
            c = cos_ref[a * ROPE_HALF:(a + 1) * ROPE_HALF, :]
            s = sin_ref[a * ROPE_HALF:(a + 1) * ROPE_HALF, :]
            store(d1, x1 * c - x2 * s)
            store(d2, x2 * c + x1 * s)

    def store_q(d, val):
        s_ref[0:N_HEADS, d:d + ROPE_HALF, :] = val

    def store_k(d, val):
        s_ref[N_HEADS:N_HEADS + A_KV_HEADS, d:d + ROPE_HALF, :] = val

    norm_rope(0, N_HEADS, qg_ref, store_q)
    norm_rope(N_HEADS, N_HEADS + A_KV_HEADS, kg_ref, store_k)
    q_hi, q_lo = _split_f8(s_ref[0:N_HEADS])
    for part, val in enumerate((q_hi, q_hi, q_lo, q_lo)):
        q_ref[0, :, part * HEAD_DIM:(part + 1) * HEAD_DIM, :] = val
    for g in range(A_KV_HEADS):
        k_hi, k_lo = _split_f8(s_ref[N_HEADS + g])
        k_hi, k_lo = k_hi.astype(F32), k_lo.astype(F32)
        k_ext_t = jnp.concatenate([k_hi, k_lo, k_hi, k_lo], axis=0)
        k_ref[0, g] = k_ext_t.T.astype(F8)
        for c in range(tm // FLASH_TK):
            v_ref[0, g, c] = s_ref[N_HEADS + A_KV_HEADS + g, :,
                                   c * FLASH_TK:(c + 1) * FLASH_TK].astype(BF16)


def _qkv_a(x, gain, w_t, q_gain, k_gain, cos_t, sin_t):
    b, s, _ = x.shape
    tm = ROW_TILE
    n_slots = N_HEADS + 2 * A_KV_HEADS
    return pl.pallas_call(
        functools.partial(_qkv_a_kernel, tm=tm),
        grid=(b, s // tm),
        in_specs=[
            pl.BlockSpec((1, tm, D_MODEL), lambda bi, i: (bi, i, 0)),
            _resident((1, D_MODEL)),
            _resident((A_QKV, D_MODEL)),
            _resident((HEAD_DIM, tm)),
            _resident((HEAD_DIM, tm)),
            pl.BlockSpec((AXIS_DIM, tm), lambda bi, i: (0, i)),
            pl.BlockSpec((AXIS_DIM, tm), lambda bi, i: (0, i)),
        ],
        out_specs=[
            pl.BlockSpec((1, N_HEADS, QK_DEPTH, tm), lambda bi, i: (bi, 0, 0, i)),
            pl.BlockSpec((1, A_KV_HEADS, tm, QK_DEPTH), lambda bi, i: (bi, 0, i, 0)),
            pl.BlockSpec((1, A_KV_HEADS, tm // FLASH_TK, HEAD_DIM, FLASH_TK),
                         lambda bi, i: (bi, 0, i, 0, 0)),
        ],
        out_shape=[
            jax.ShapeDtypeStruct((b, N_HEADS, QK_DEPTH, s), F8),
            jax.ShapeDtypeStruct((b, A_KV_HEADS, s, QK_DEPTH), F8),
            jax.ShapeDtypeStruct((b, A_KV_HEADS, s // FLASH_TK, HEAD_DIM, FLASH_TK), BF16),
        ],
        scratch_shapes=[pltpu.VMEM((n_slots, HEAD_DIM, tm), F32)],
        compiler_params=_params(2),
        name="qkv_a",
    )(x, gain, w_t, q_gain, k_gain, cos_t, sin_t)


def _flash_kernel(q_ref, k_ref, v_ref, qn_ref, kn_ref, o_ref, s_ref, cm_ref, *, n_chunks, tq):
    first = (pl.program_id(0) == 0) & (pl.program_id(1) == 0) & (pl.program_id(2) == 0)

    def scores(c, slot, e, j, q_src=q_ref, k_src=k_ref):
        k_c = k_src[0, 0, c * FLASH_TK:(c + 1) * FLASH_TK, :]
        q_e = q_src[0, e, :, j * tq:(j + 1) * tq]
        s_t = jnp.dot(k_c, q_e, preferred_element_type=F32)
        s_ref[slot, e] = s_t
        cm_ref[slot, e] = jnp.max(s_t, axis=0, keepdims=True)

    ones = jnp.ones((FLASH_L_ROWS, FLASH_TK), BF16)

    def update(c, slot, e, stat):
        m, acc = stat
        s_t = s_ref[slot, e]
        m_new = jnp.maximum(m, cm_ref[slot, e])
        alpha = jnp.exp2(m - m_new)
        p = jnp.exp2(s_t - m_new).astype(BF16)
        v_ext = jnp.concatenate([v_ref[0, 0, c], ones], axis=0)
        acc = alpha * acc + jnp.dot(v_ext, p, preferred_element_type=F32)
        return m_new, acc

    @pl.when(first)
    def _():
        for e in range(FLASH_HEADS):
            scores(0, 0, e, 0)

    for j in range(FLASH_Q_TILES):
        stats = tuple((jnp.full((1, tq), -jnp.inf, F32),
                       jnp.zeros((HEAD_DIM + FLASH_L_ROWS, tq), F32)) for _ in range(FLASH_HEADS))
        for c in range(n_chunks):
            new = []
            for e in range(FLASH_HEADS):
                if c + 1 < n_chunks:
                    scores(c + 1, (c + 1) % 2, e, j)
                elif j + 1 < FLASH_Q_TILES:
                    scores(0, 0, e, j + 1)
                else:
                    scores(0, 0, e, 0, qn_ref, kn_ref)
                new.append(update(c, c % 2, e, stats[e]))
            stats = tuple(new)
        o_t = jnp.concatenate([acc[:HEAD_DIM] / acc[HEAD_DIM:HEAD_DIM + 1] for (_, acc) in stats],
                              axis=0)
        o_ref[0, 0, j * tq:(j + 1) * tq, :] = o_t.T.astype(BF16)


def _flash(q_t, k, v_t):
    b, _, _, s = q_t.shape
    tq = FLASH_TQ
    n_chunks = s // FLASH_TK
    pairs = N_HEADS // FLASH_HEADS
    per_kv = A_GROUP // FLASH_HEADS
    tq_step = tq * FLASH_Q_TILES
    n_q = s // tq_step
    assert n_chunks % 2 == 0

    def following(bi, hp, i):
        flat = jnp.minimum((bi * pairs + hp) * n_q + i + 1, b * pairs * n_q - 1)
        return flat // (pairs * n_q), (flat // n_q) % pairs, flat % n_q

    def q_next(bi, hp, i):
        nb, nh, ni = following(bi, hp, i)
        return nb, nh, 0, ni

    def k_next(bi, hp, i):
        nb, nh, _ = following(bi, hp, i)
        return nb, nh // per_kv, 0, 0

    return pl.pallas_call(
        functools.partial(_flash_kernel, n_chunks=n_chunks, tq=tq),
        grid=(b, pairs, n_q),
        in_specs=[
            pl.BlockSpec((1, FLASH_HEADS, QK_DEPTH, tq_step), lambda bi, hp, i: (bi, hp, 0, i)),
            pl.BlockSpec((1, 1, s, QK_DEPTH), lambda bi, hp, i: (bi, hp // per_kv, 0, 0)),
            pl.BlockSpec((1, 1, n_chunks, HEAD_DIM, FLASH_TK),
                         lambda bi, hp, i: (bi, hp // per_kv, 0, 0, 0)),
            pl.BlockSpec((1, FLASH_HEADS, QK_DEPTH, tq_step), q_next),
            pl.BlockSpec((1, 1, s, QK_DEPTH), k_next),
        ],
        out_specs=pl.BlockSpec((1, 1, tq_step, O_GROUP_WIDTH), lambda bi, hp, i: (bi, hp, i, 0)),
        out_shape=jax.ShapeDtypeStruct((b, O_GROUPS, s, O_GROUP_WIDTH), BF16),
        scratch_shapes=[pltpu.VMEM((2, FLASH_HEADS, FLASH_TK, tq), F32),
                        pltpu.VMEM((2, FLASH_HEADS, 1, tq), F32)],
        compiler_params=_params(3),
        name="flash_a",
    )(q_t, k, v_t, q_t, k)


def _mlp_kernel(x_ref, a_ref, wo_ref, g_ref, win_ref, wout_ref, gf_ref, o_ref, *, final):
    a = jnp.concatenate([a_ref[0, g] for g in range(O_GROUPS)], axis=1)
    x = x_ref[0] + jnp.dot(a, wo_ref[...], preferred_element_type=F32)
    h = _rms(x, g_ref[...]).astype(BF16)
    acc = x
    for f in range(D_FF // FF_CHUNK):
        u = jnp.dot(h, win_ref[:, f * FF_CHUNK:(f + 1) * FF_CHUNK], preferred_element_type=F32)
        u = jnp.maximum(u, 0.0)
        acc = acc + jnp.dot((u * u).astype(BF16), wout_ref[f * FF_CHUNK:(f + 1) * FF_CHUNK, :],
                            preferred_element_type=F32)
    if final:
        acc = _rms(acc, gf_ref[...])
    o_ref[0] = acc


def _mlp(x, attn, w_o, gain, w_in, w_out, gain_final, final):
    b, s, _ = x.shape
    tm = MLP_ROW_TILE
    return pl.pallas_call(
        functools.partial(_mlp_kernel, final=final),
        grid=(b, s // tm),
        in_specs=[
            pl.BlockSpec((1, tm, D_MODEL), lambda bi, i: (bi, i, 0)),
            pl.BlockSpec((1, O_GROUPS, tm, O_GROUP_WIDTH), lambda bi, i: (bi, 0, i, 0)),
            _resident((HD, D_MODEL)),
            _resident((1, D_MODEL)),
            _resident((D_MODEL, D_FF)),
            _resident((D_FF, D_MODEL)),
            _resident((1, D_MODEL)),
        ],
        out_specs=pl.BlockSpec((1, tm, D_MODEL), lambda bi, i: (bi, i, 0)),
        out_shape=jax.ShapeDtypeStruct((b, s, D_MODEL), F32),
        compiler_params=_params(2),
        name="mlp_final" if final else "mlp",
    )(x, attn, w_o, gain, w_in, w_out, gain_final)


def _qkv_b_kernel(x_ref, g_ref, wqv_t_ref, wk_ref, q_ref, k_ref, v_ref, *, tm):
    h = _rms(x_ref[0], g_ref[...]).astype(BF16)
    k = jnp.dot(h, wk_ref[...], preferred_element_type=F32).astype(BF16)
    width = NA_HEADS * HEAD_DIM
    for g in range(N_HEADS // NA_HEADS):
        k_ref[0, g] = k[:, g * width:(g + 1) * width]
    nt = (((1,), (1,)), ((), ()))
    q_t = lax.dot_general(wqv_t_ref[0:HD, :], h, nt, preferred_element_type=F32)
    for t in range(tm // NA_Q_TILE):
        q_ref[0, t] = (q_t[:, t * NA_Q_TILE:(t + 1) * NA_Q_TILE] * NA_SCORE_SCALE).astype(BF16)
    v_t = lax.dot_general(wqv_t_ref[HD:2 * HD, :], h, nt, preferred_element_type=F32)
    for hd in range(N_HEADS):
        for c in range(tm // NA_NQ):
            v_ref[0, hd, c] = v_t[hd * HEAD_DIM:(hd + 1) * HEAD_DIM,
                                  c * NA_NQ:(c + 1) * NA_NQ].astype(BF16)


def _qkv_b(x, gain, wqv_t, wk):
    b, s, _ = x.shape
    tm = QKV_B_ROW_TILE
    groups = N_HEADS // NA_HEADS
    width = NA_HEADS * HEAD_DIM
    return pl.pallas_call(
        functools.partial(_qkv_b_kernel, tm=tm),
        grid=(b, s // tm),
        in_specs=[
            pl.BlockSpec((1, tm, D_MODEL), lambda bi, i: (bi, i, 0)),
            _resident((1, D_MODEL)),
            _resident((2 * HD, D_MODEL)),
            _resident((D_MODEL, HD)),
        ],
        out_specs=[
            pl.BlockSpec((1, tm // NA_Q_TILE, HD, NA_Q_TILE), lambda bi, i: (bi, i, 0, 0)),
            pl.BlockSpec((1, groups, tm, width), lambda bi, i: (bi, 0, i, 0)),
            pl.BlockSpec((1, N_HEADS, tm // NA_NQ, HEAD_DIM, NA_NQ), lambda bi, i: (bi, 0, i, 0, 0)),
        ],
        out_shape=[
            jax.ShapeDtypeStruct((b, s // NA_Q_TILE, HD, NA_Q_TILE), BF16),
            jax.ShapeDtypeStruct((b, groups, s, width), BF16),
            jax.ShapeDtypeStruct((b, N_HEADS, s // NA_NQ, HEAD_DIM, NA_NQ), BF16),
        ],
        compiler_params=_params(2),
        name="qkv_b",
    )(x, gain, wqv_t, wk)


def _natten_kernel(q_ref, qn_ref, k_ref, v_ref, b_ref, o_ref, s_ref, cm_ref, *, rows):
    u = pl.program_id(2)
    n_blocks = rows // NA_Q_ROWS
    width = NA_HEADS * HEAD_DIM
    row_id = lax.broadcasted_iota(jnp.int32, (width, NA_NQ), 0)
    ones = jnp.ones((FLASH_L_ROWS, NA_NK), BF16)

    def key_base(blk):
        return jnp.clip(NA_Q_ROWS * blk - NA_WIN_H // 2, 0, rows - NA_K_ROWS)

    def scores(blk, q_blk, slot, e):
        variant = jnp.where(blk == 0, 0, jnp.where(blk == n_blocks - 1, 2, 1))
        start = pl.multiple_of(key_base(blk) * GRID_W, NA_NQ)
        k_w = k_ref[0, 0, pl.ds(start, NA_NK), :]
        mine = (row_id >= e * HEAD_DIM) & (row_id < (e + 1) * HEAD_DIM)
        q_e = jnp.where(mine, q_blk, jnp.zeros_like(q_blk))
        s_t = jnp.dot(k_w, q_e, preferred_element_type=F32) + b_ref[variant, e]
        s_ref[slot, e] = s_t
        cm_ref[slot, e] = jnp.max(s_t, axis=0, keepdims=True)

    def attend(blk, slot, e):
        chunk0 = key_base(blk) // NA_Q_ROWS
        s_t = s_ref[slot, e]
        p = jnp.exp2(s_t - cm_ref[slot, e]).astype(BF16)
        v_t = [v_ref[0, e, chunk0 + j] for j in range(NA_K_ROWS // NA_Q_ROWS)]
        v_ext = jnp.concatenate([jnp.concatenate(v_t, axis=1), ones], axis=0)
        acc = jnp.dot(v_ext, p, preferred_element_type=F32)
        return acc[:HEAD_DIM] / acc[HEAD_DIM:HEAD_DIM + 1]

    @pl.when(u == 0)
    def _():
        for e in range(NA_HEADS):
            scores(0, q_ref[0, 0, :, 0:NA_NQ], 0, e)

    def half(blk, slot, blk_next, q_next, row0):
        outs = []
        for e in range(NA_HEADS):
            scores(blk_next, q_next, 1 - slot, e)
            outs.append(attend(blk, slot, e))
        o_ref[0, 0, row0:row0 + NA_NQ, :] = jnp.concatenate(outs, axis=0).T.astype(BF16)

    def q_block(n):
        return q_ref[0, n // 2, :, (n % 2) * NA_NQ:(n % 2 + 1) * NA_NQ]

    for n in range(NA_BLOCKS - 1):
        half(NA_BLOCKS * u + n, n % 2, NA_BLOCKS * u + n + 1, q_block(n + 1), n * NA_NQ)
    u_next = jnp.minimum(u + 1, n_blocks // NA_BLOCKS - 1)
    half(NA_BLOCKS * u + NA_BLOCKS - 1, (NA_BLOCKS - 1) % 2, NA_BLOCKS * u_next,
         qn_ref[0, 0, :, 0:NA_NQ], (NA_BLOCKS - 1) * NA_NQ)


def _natten(q_t, k, v_t, bias):
    b, _, s, _ = k.shape
    rows = s // GRID_W
    n_blocks = rows // NA_Q_ROWS
    groups = N_HEADS // NA_HEADS
    width = NA_HEADS * HEAD_DIM
    steps = n_blocks // NA_BLOCKS
    tiles = NA_BLOCKS // 2
    assert rows >= NA_K_ROWS + NA_Q_ROWS
    assert n_blocks % NA_BLOCKS == 0
    return pl.pallas_call(
        functools.partial(_natten_kernel, rows=rows),
        grid=(groups, b, steps),
        in_specs=[
            pl.BlockSpec((1, tiles, width, 2 * NA_NQ), lambda hg, bi, u: (bi, u, hg, 0)),
            pl.BlockSpec((1, 1, width, 2 * NA_NQ),
                         lambda hg, bi, u: (bi, tiles * jnp.minimum(u + 1, steps - 1), hg, 0)),
            pl.BlockSpec((1, 1, s, width), lambda hg, bi, u: (bi, hg, 0, 0)),
            pl.BlockSpec((1, NA_HEADS, s // NA_NQ, HEAD_DIM, NA_NQ),
                         lambda hg, bi, u: (bi, hg, 0, 0, 0)),
            pl.BlockSpec((3, NA_HEADS, NA_NK, NA_NQ), lambda hg, bi, u: (0, hg, 0, 0),
                         pipeline_mode=pl.Buffered(1)),
        ],
        out_specs=pl.BlockSpec((1, 1, NA_BLOCKS * NA_NQ, width), lambda hg, bi, u: (bi, hg, u, 0)),
        out_shape=jax.ShapeDtypeStruct((b, groups, s, width), BF16),
        scratch_shapes=[pltpu.VMEM((2, NA_HEADS, NA_NK, NA_NQ), F32),
                        pltpu.VMEM((2, NA_HEADS, 1, NA_NQ), F32)],
        compiler_params=_params(3),
        name="natten_b",
    )(q_t, q_t, k, v_t, bias)


def _rope_tables_t(seq):
    t = jnp.arange(seq)
    row = (t // GRID_W).astype(F32)
    col = (t % GRID_W).astype(F32)
    inv = ROPE_THETA ** (-jnp.arange(0, AXIS_DIM, 2, dtype=F32) / AXIS_DIM)
    ang = jnp.concatenate([inv[:, None] * row[None, :], inv[:, None] * col[None, :]], axis=0)
    return jnp.cos(ang), jnp.sin(ang)


NA_ROW_OFFSET = (NA_WIN_H - 1, NA_WIN_H - 1 - NA_WIN_H // 2,
                 NA_WIN_H - 1 - (NA_K_ROWS - NA_Q_ROWS))


def _na_window_start(variant, a):
    return (0, a, NA_K_ROWS - NA_WIN_H)[variant]


def _natten_bias_kernel(even_ref, odd_ref, mask_ref, o_ref):
    lanes = 2 * GRID_W
    lane = lax.broadcasted_iota(jnp.int32, (GRID_W, lanes), 1)
    masked = jnp.full((GRID_W, lanes), -jnp.inf, F32)
    for variant in range(3):
        for i in range(NA_K_ROWS):
            for pair in range(NA_Q_ROWS // 2):
                rows_a = (2 * pair, 2 * pair + 1)
                inside = [0 <= i - _na_window_start(variant, a) < NA_WIN_H for a in rows_a]
                blk = masked
                if any(inside):
                    base = jnp.zeros((1, lanes), F32)
                    for a, ok, src in zip(rows_a, inside, (even_ref, odd_ref)):
                        if ok:
                            d = i - a + NA_ROW_OFFSET[variant]
                            base = base + src[0, d:d + 1, :]
                    blk = pltpu.roll(jnp.broadcast_to(base, (GRID_W, lanes)), 0, 1,
                                     stride=1, stride_axis=0) + mask_ref[...]
                    if not inside[0]:
                        blk = jnp.where(lane < GRID_W, masked, blk)
                    if not inside[1]:
                        blk = jnp.where(lane >= GRID_W, masked, blk)
                o_ref[variant, 0, i * GRID_W:(i + 1) * GRID_W, pair * lanes:(pair + 1) * lanes] = blk


def _natten_bias(rel_bias):
    table = rel_bias.astype(F32) * LOG2_E
    lanes = 2 * GRID_W
    m = jnp.arange(lanes)
    reach = NA_WIN_W - 1
    col_even = jnp.where(m < GRID_W, reach - m, reach - (m - lanes))
    col_odd = reach - (m - GRID_W)

    def place(col, keep):
        vals = jnp.take(table, jnp.clip(col, 0, NA_BIAS_W - 1), axis=2)
        vals = jnp.where((keep & (col >= 0) & (col < NA_BIAS_W))[None, None, :], vals, 0.0)
        return jnp.pad(vals, ((0, 0), (0, 1), (0, 0)))

    even = place(col_even, (m <= reach) | (m >= lanes - reach))
    odd = place(col_odd, (m >= GRID_W - reach) & (m <= GRID_W + reach))
    c = jnp.arange(GRID_W)[None, :]
    kc = jnp.arange(GRID_W)[:, None]
    c0 = jnp.clip(c - NA_WIN_W // 2, 0, GRID_W - NA_WIN_W)
    valid_c = (kc >= c0) & (kc < c0 + NA_WIN_W)
    mask = jnp.where(jnp.concatenate([valid_c, valid_c], axis=1), 0.0, -jnp.inf).astype(F32)
    return pl.pallas_call(
        _natten_bias_kernel,
        grid=(N_HEADS,),
        in_specs=[
            pl.BlockSpec((1, NA_BIAS_H + 1, lanes), lambda h: (h, 0, 0)),
            pl.BlockSpec((1, NA_BIAS_H + 1, lanes), lambda h: (h, 0, 0)),
            pl.BlockSpec((GRID_W, lanes), lambda h: (0, 0)),
        ],
        out_specs=pl.BlockSpec((3, 1, NA_NK, NA_NQ), lambda h: (0, h, 0, 0)),
        out_shape=jax.ShapeDtypeStruct((3, N_HEADS, NA_NK, NA_NQ), F32),
        compiler_params=_params(1),
        name="natten_bias",
    )(even, odd, mask)


def _trunk(x, p):
    _, s, _ = x.shape
    depth = p["norm_mix"].shape[0]
    for i in range(depth):
        j = i // 2
        g_mix = p["norm_mix"][i][None, :]
        if i % 2 == 0:
            q_t, k, v_t = _qkv_a(x, g_mix, p["a_w_qkv_t"][j], p["a_q_gain"][j], p["a_k_gain"][j],
                                 p["cos_t"], p["sin_t"])
            o = _flash(q_t, k, v_t)
            w_o = p["a_w_o"][j]
        else:
            q_t, k, v_t = _qkv_b(x, g_mix, p["b_w_qv_t"][j], p["b_w_k"][j])
            o = _natten(q_t, k, v_t, p["b_bias"][j])
            w_o = p["b_w_o"][j]
        x = _mlp(x, o, w_o, p["norm_mlp"][i][None, :], p["mlp_w_in"][i], p["mlp_w_out"][i],
                 p["norm_final"][None, :], final=(i == depth - 1))
    return x


def kernel(x_prompt, x_sample, norm_mix, norm_mlp, norm_final, a_w_qkv, a_q_norm, a_k_norm, a_w_o,
           b_w_qkv, b_rel_bias, b_w_o, mlp_w_in, mlp_w_out):
    scale = HEAD_DIM ** -0.5 * LOG2_E
    q_bound = jnp.max(jnp.abs(a_q_norm), axis=1) * scale
    k_bound = jnp.max(jnp.abs(a_k_norm), axis=1)
    ratio = k_bound / q_bound
    ratio = jnp.where(jnp.isfinite(ratio) & (ratio > 0), ratio, 1.0)
    balance = jnp.exp2(jnp.round(0.5 * jnp.log2(ratio)))[:, None]
    shared = {
        "norm_mix": norm_mix, "norm_mlp": norm_mlp, "norm_final": norm_final,
        "a_w_qkv_t": jnp.swapaxes(a_w_qkv, 1, 2).astype(BF16),
        "a_q_gain": jnp.broadcast_to((a_q_norm * scale * balance)[:, :, None],
                                     a_q_norm.shape + (ROW_TILE,)),
        "a_k_gain": jnp.broadcast_to((a_k_norm / balance)[:, :, None],
                                     a_k_norm.shape + (ROW_TILE,)),
        "a_w_o": a_w_o.astype(BF16),
        "b_w_qv_t": jnp.swapaxes(jnp.concatenate([b_w_qkv[:, :, :HD], b_w_qkv[:, :, 2 * HD:]],
                                                 axis=2), 1, 2).astype(BF16),
        "b_w_k": b_w_qkv[:, :, HD:2 * HD].astype(BF16),
        "b_w_o": b_w_o.astype(BF16),
        "mlp_w_in": mlp_w_in.astype(BF16),
        "mlp_w_out": mlp_w_out.astype(BF16),
        "b_bias": [_natten_bias(rb) for rb in b_rel_bias],
    }
    outs = []
    for x in (x_prompt, x_sample):
        cos_t, sin_t = _rope_tables_t(x.shape[1])
        outs.append(_trunk(x, dict(shared, cos_t=cos_t, sin_t=sin_t)))
    return tuple(outs)
```

```python
import functools

import jax
import jax.numpy as jnp
from jax import lax
from jax.experimental import pallas as pl
from jax.experimental.pallas import tpu as pltpu

D_MODEL = 1024
GRID_W = 64
N_HEADS = 16
HEAD_DIM = 64
A_KV_HEADS = 4
A_GROUP = N_HEADS // A_KV_HEADS
ROPE_THETA = 10000.0
AXIS_DIM = HEAD_DIM // 2
ROPE_HALF = AXIS_DIM // 2
NA_WIN_H = 8
NA_WIN_W = 16
NA_BIAS_H = 2 * NA_WIN_H - 1
NA_BIAS_W = 2 * NA_WIN_W - 1
D_FF = 4 * D_MODEL
NORM_EPS = 1e-6
LOG2_E = 1.4426950408889634
NA_SCORE_SCALE = HEAD_DIM ** -0.5 * LOG2_E
HD = N_HEADS * HEAD_DIM
A_QKV = (N_HEADS + 2 * A_KV_HEADS) * HEAD_DIM

ROW_TILE = 1024
QKV_B_ROW_TILE = 1024
MLP_ROW_TILE = 1024
FLASH_TQ = 256
FLASH_Q_TILES = 2
FLASH_TK = 512
FLASH_HEADS = 4
FLASH_L_ROWS = 16
NA_Q_ROWS = 4
NA_K_ROWS = 12
NA_HEADS = 4
NA_BLOCKS = 8
O_GROUPS = 4
O_GROUP_WIDTH = 256
NA_NQ = NA_Q_ROWS * GRID_W
NA_Q_TILE = 2 * NA_NQ
NA_NK = NA_K_ROWS * GRID_W
FF_CHUNK = 1024
VMEM_LIMIT_BYTES = 56 * 1024 * 1024

BF16 = jnp.bfloat16
F32 = jnp.float32
F8 = jnp.float8_e4m3fn
F8_MAX = 448.0
QK_DEPTH = 4 * HEAD_DIM


def _params(n_axes, fusible_inputs=None):
    return pltpu.CompilerParams(
        dimension_semantics=("arbitrary",) * n_axes, vmem_limit_bytes=VMEM_LIMIT_BYTES,
        allow_input_fusion=fusible_inputs)


def _resident(shape):
    zeros = (0,) * len(shape)
    return pl.BlockSpec(shape, lambda *_: zeros, pipeline_mode=pl.Buffered(1))


def _split_f8(x):
    x = jnp.clip(x, -F8_MAX, F8_MAX)
    hi = x.astype(F8)
    lo = (x - hi.astype(F32)).astype(F8)
    return hi, lo


def _rms(x, gain):
    ms = jnp.mean(x * x, axis=-1, keepdims=True)
    return x * lax.rsqrt(ms + NORM_EPS) * gain


def _qkv_a_kernel(x_ref, g_ref, wt_ref, qg_ref, kg_ref, cos_ref, sin_ref,
                  q_ref, k_ref, v_ref, s_ref, *, tm):
    h = _rms(x_ref[0], g_ref[...]).astype(BF16)
    qkv_t = lax.dot_general(wt_ref[...], h, (((1,), (1,)), ((), ())),
                            preferred_element_type=F32)
    n_slots = N_HEADS + 2 * A_KV_HEADS
    s_ref[...] = qkv_t.reshape(n_slots, HEAD_DIM, tm)

    def norm_rope(lo, hi, gain_ref, store):
        t = s_ref[lo:hi]
        r = lax.rsqrt(jnp.sum(t * t, axis=1, keepdims=True) * (1.0 / HEAD_DIM) + NORM_EPS)
        for a in range(2):
            d1 = a * AXIS_DIM
            d2 = d1 + ROPE_HALF
            x1 = s_ref[lo:hi, d1:d1 + ROPE_HALF, :] * r * gain_ref[d1:d1 + ROPE_HALF, :]
            x2 = s_ref[lo:hi, d2:d2 + ROPE_HALF, :] * r * gain_ref[d2:d2 + ROPE_HALF, :]
            c = cos_ref[a * ROPE_HALF:(a + 1) * ROPE_HALF, :]
            s = sin_ref[a * ROPE_HALF:(a + 1) * ROPE_HALF, :]
            store(d1, x1 * c - x2 * s)
            store(d2, x2 * c + x1 * s)

    def store_q(d, val):
        s_ref[0:N_HEADS, d:d + ROPE_HALF, :] = val

    def store_k(d, val):
        s_ref[N_HEADS:N_HEADS + A_KV_HEADS, d:d + ROPE_HALF, :] = val

    norm_rope(0, N_HEADS, qg_ref, store_q)
    norm_rope(N_HEADS, N_HEADS + A_KV_HEADS, kg_ref, store_k)
    q_hi, q_lo = _split_f8(s_ref[0:N_HEADS])
    for part, val in enumerate((q_hi, q_hi, q_lo, q_lo)):
        q_ref[0, :, part * HEAD_DIM:(part + 1) * HEAD_DIM, :] = val
    for g in range(A_KV_HEADS):
        k_hi, k_lo = _split_f8(s_ref[N_HEADS + g])
        k_hi, k_lo = k_hi.astype(F32), k_lo.astype(F32)
        k_ext_t = jnp.concatenate([k_hi, k_lo, k_hi, k_lo], axis=0)
        k_ref[0, g] = k_ext_t.T.astype(F8)
        for c in range(tm // FLASH_TK):
            v_ref[0, g, c] = s_ref[N_HEADS + A_KV_HEADS + g, :,
                                   c * FLASH_TK:(c + 1) * FLASH_TK].astype(BF16)


def _qkv_a(x, gain, w_t, q_gain, k_gain, cos_t, sin_t):
    b, s, _ = x.shape
    tm = ROW_TILE
    n_slots = N_HEADS + 2 * A_KV_HEADS
    return pl.pallas_call(
        functools.partial(_qkv_a_kernel, tm=tm),
        grid=(b, s // tm),
        in_specs=[
            pl.BlockSpec((1, tm, D_MODEL), lambda bi, i: (bi, i, 0)),
            _resident((1, D_MODEL)),
            _resident((A_QKV, D_MODEL)),
            _resident((HEAD_DIM, tm)),
            _resident((HEAD_DIM, tm)),
            pl.BlockSpec((AXIS_DIM, tm), lambda bi, i: (0, i)),
            pl.BlockSpec((AXIS_DIM, tm), lambda bi, i: (0, i)),
        ],
        out_specs=[
            pl.BlockSpec((1, N_HEADS, QK_DEPTH, tm), lambda bi, i: (bi, 0, 0, i)),
            pl.BlockSpec((1, A_KV_HEADS, tm, QK_DEPTH), lambda bi, i: (bi, 0, i, 0)),
            pl.BlockSpec((1, A_KV_HEADS, tm // FLASH_TK, HEAD_DIM, FLASH_TK),
                         lambda bi, i: (bi, 0, i, 0, 0)),
        ],
        out_shape=[
            jax.ShapeDtypeStruct((b, N_HEADS, QK_DEPTH, s), F8),
            jax.ShapeDtypeStruct((b, A_KV_HEADS, s, QK_DEPTH), F8),
            jax.ShapeDtypeStruct((b, A_KV_HEADS, s // FLASH_TK, HEAD_DIM, FLASH_TK), BF16),
        ],
        scratch_shapes=[pltpu.VMEM((n_slots, HEAD_DIM, tm), F32)],
        compiler_params=_params(2),
        name="qkv_a",
    )(x, gain, w_t, q_gain, k_gain, cos_t, sin_t)


def _flash_kernel(q_ref, k_ref, v_ref, qn_ref, kn_ref, o_ref, s_ref, cm_ref, *, n_chunks, tq):
    first = (pl.program_id(0) == 0) & (pl.program_id(1) == 0) & (pl.program_id(2) == 0)

    def scores(c, slot, e, j, q_src=q_ref, k_src=k_ref):
        k_c = k_src[0, 0, c * FLASH_TK:(c + 1) * FLASH_TK, :]
        q_e = q_src[0, e, :, j * tq:(j + 1) * tq]
        s_t = jnp.dot(k_c, q_e, preferred_element_type=F32)
        s_ref[slot, e] = s_t
        cm_ref[slot, e] = jnp.max(s_t, axis=0, keepdims=True)

    ones = jnp.ones((FLASH_L_ROWS, FLASH_TK), BF16)

    def update(c, slot, e, stat):
        m, acc = stat
        s_t = s_ref[slot, e]
        m_new = jnp.maximum(m, cm_ref[slot, e])
        alpha = jnp.exp2(m - m_new)
        p = jnp.exp2(s_t - m_new).astype(BF16)
        v_ext = jnp.concatenate([v_ref[0, 0, c], ones], axis=0)
        acc = alpha * acc + jnp.dot(v_ext, p, preferred_element_type=F32)
        return m_new, acc

    @pl.when(first)
    def _():
        for e in range(FLASH_HEADS):
            scores(0, 0, e, 0)

    for j in range(FLASH_Q_TILES):
        stats = tuple((jnp.full((1, tq), -jnp.inf, F32),
                       jnp.zeros((HEAD_DIM + FLASH_L_ROWS, tq), F32)) for _ in range(FLASH_HEADS))
        for c in range(n_chunks):
            new = []
            for e in range(FLASH_HEADS):
                if c + 1 < n_chunks:
                    scores(c + 1, (c + 1) % 2, e, j)
                elif j + 1 < FLASH_Q_TILES:
                    scores(0, 0, e, j + 1)
                else:
                    scores(0, 0, e, 0, qn_ref, kn_ref)
                new.append(update(c, c % 2, e, stats[e]))
            stats = tuple(new)
        o_t = jnp.concatenate([acc[:HEAD_DIM] / acc[HEAD_DIM:HEAD_DIM + 1] for (_, acc) in stats],
                              axis=0)
        o_ref[0, 0, j * tq:(j + 1) * tq, :] = o_t.T.astype(BF16)


def _flash(q_t, k, v_t):
    b, _, _, s = q_t.shape
    tq = FLASH_TQ
    n_chunks = s // FLASH_TK
    pairs = N_HEADS // FLASH_HEADS
    per_kv = A_GROUP // FLASH_HEADS
    tq_step = tq * FLASH_Q_TILES
    n_q = s // tq_step
    assert n_chunks % 2 == 0

    def following(bi, hp, i):
        flat = jnp.minimum((bi * pairs + hp) * n_q + i + 1, b * pairs * n_q - 1)
        return flat // (pairs * n_q), (flat // n_q) % pairs, flat % n_q

    def q_next(bi, hp, i):
        nb, nh, ni = following(bi, hp, i)
        return nb, nh, 0, ni

    def k_next(bi, hp, i):
        nb, nh, _ = following(bi, hp, i)
        return nb, nh // per_kv, 0, 0

    return pl.pallas_call(
        functools.partial(_flash_kernel, n_chunks=n_chunks, tq=tq),
        grid=(b, pairs, n_q),
        in_specs=[
            pl.BlockSpec((1, FLASH_HEADS, QK_DEPTH, tq_step), lambda bi, hp, i: (bi, hp, 0, i)),
            pl.BlockSpec((1, 1, s, QK_DEPTH), lambda bi, hp, i: (bi, hp // per_kv, 0, 0)),
            pl.BlockSpec((1, 1, n_chunks, HEAD_DIM, FLASH_TK),
                         lambda bi, hp, i: (bi, hp // per_kv, 0, 0, 0)),
            pl.BlockSpec((1, FLASH_HEADS, QK_DEPTH, tq_step), q_next),
            pl.BlockSpec((1, 1, s, QK_DEPTH), k_next),
        ],
        out_specs=pl.BlockSpec((1, 1, tq_step, O_GROUP_WIDTH), lambda bi, hp, i: (bi, hp, i, 0)),
        out_shape=jax.ShapeDtypeStruct((b, O_GROUPS, s, O_GROUP_WIDTH), BF16),
        scratch_shapes=[pltpu.VMEM((2, FLASH_HEADS, FLASH_TK, tq), F32),
                        pltpu.VMEM((2, FLASH_HEADS, 1, tq), F32)],
        compiler_params=_params(3),
        name="flash_a",
    )(q_t, k, v_t, q_t, k)


def _mlp_kernel(x_ref, a_ref, wo_ref, g_ref, win_ref, wout_ref, gf_ref, o_ref, *, final):
    a = jnp.concatenate([a_ref[0, g] for g in range(O_GROUPS)], axis=1)
    x = x_ref[0] + jnp.dot(a, wo_ref[...], preferred_element_type=F32)
    h = _rms(x, g_ref[...]).astype(BF16)
    acc = x
    for f in range(D_FF // FF_CHUNK):
        u = jnp.dot(h, win_ref[:, f * FF_CHUNK:(f + 1) * FF_CHUNK], preferred_element_type=F32)
        u = jnp.maximum(u, 0.0)
        acc = acc + jnp.dot((u * u).astype(BF16), wout_ref[f * FF_CHUNK:(f + 1) * FF_CHUNK, :],
                            preferred_element_type=F32)
    if final:
        acc = _rms(acc, gf_ref[...])
    o_ref[0] = acc


def _mlp(x, attn, w_o, gain, w_in, w_out, gain_final, final):
    b, s, _ = x.shape
    tm = MLP_ROW_TILE
    return pl.pallas_call(
        functools.partial(_mlp_kernel, final=final),
        grid=(b, s // tm),
        in_specs=[
            pl.BlockSpec((1, tm, D_MODEL), lambda bi, i: (bi, i, 0)),
            pl.BlockSpec((1, O_GROUPS, tm, O_GROUP_WIDTH), lambda bi, i: (bi, 0, i, 0)),
            _resident((HD, D_MODEL)),
            _resident((1, D_MODEL)),
            _resident((D_MODEL, D_FF)),
            _resident((D_FF, D_MODEL)),
            _resident((1, D_MODEL)),
        ],
        out_specs=pl.BlockSpec((1, tm, D_MODEL), lambda bi, i: (bi, i, 0)),
        out_shape=jax.ShapeDtypeStruct((b, s, D_MODEL), F32),
        compiler_params=_params(2, [False, False, True, False, True, True, False]),
        name="mlp_final" if final else "mlp",
    )(x, attn, w_o, gain, w_in, w_out, gain_final)


def _qkv_b_kernel(x_ref, g_ref, wqv_t_ref, wk_ref, q_ref, k_ref, v_ref, *, tm):
    h = _rms(x_ref[0], g_ref[...]).astype(BF16)
    k = jnp.dot(h, wk_ref[...], preferred_element_type=F32).astype(BF16)
    width = NA_HEADS * HEAD_DIM
    for g in range(N_HEADS // NA_HEADS):
        k_ref[0, g] = k[:, g * width:(g + 1) * width]
    nt = (((1,), (1,)), ((), ()))
    q_t = lax.dot_general(wqv_t_ref[0:HD, :], h, nt, preferred_element_type=F32)
    for t in range(tm // NA_Q_TILE):
        q_ref[0, t] = (q_t[:, t * NA_Q_TILE:(t + 1) * NA_Q_TILE] * NA_SCORE_SCALE).astype(BF16)
    v_t = lax.dot_general(wqv_t_ref[HD:2 * HD, :], h, nt, preferred_element_type=F32)
    for hd in range(N_HEADS):
        for c in range(tm // NA_NQ):
            v_ref[0, hd, c] = v_t[hd * HEAD_DIM:(hd + 1) * HEAD_DIM,
                                  c * NA_NQ:(c + 1) * NA_NQ].astype(BF16)


def _qkv_b(x, gain, wqv_t, wk):
    b, s, _ = x.shape
    tm = QKV_B_ROW_TILE
    groups = N_HEADS // NA_HEADS
    width = NA_HEADS * HEAD_DIM
    return pl.pallas_call(
        functools.partial(_qkv_b_kernel, tm=tm),
        grid=(b, s // tm),
        in_specs=[
            pl.BlockSpec((1, tm, D_MODEL), lambda bi, i: (bi, i, 0)),
            _resident((1, D_MODEL)),
            _resident((2 * HD, D_MODEL)),
            _resident((D_MODEL, HD)),
        ],
        out_specs=[
            pl.BlockSpec((1, tm // NA_Q_TILE, HD, NA_Q_TILE), lambda bi, i: (bi, i, 0, 0)),
            pl.BlockSpec((1, groups, tm, width), lambda bi, i: (bi, 0, i, 0)),
            pl.BlockSpec((1, N_HEADS, tm // NA_NQ, HEAD_DIM, NA_NQ), lambda bi, i: (bi, 0, i, 0, 0)),
        ],
        out_shape=[
            jax.ShapeDtypeStruct((b, s // NA_Q_TILE, HD, NA_Q_TILE), BF16),
            jax.ShapeDtypeStruct((b, groups, s, width), BF16),
            jax.ShapeDtypeStruct((b, N_HEADS, s // NA_NQ, HEAD_DIM, NA_NQ), BF16),
        ],
        compiler_params=_params(2),
        name="qkv_b",
    )(x, gain, wqv_t, wk)


def _natten_kernel(q_ref, qn_ref, k_ref, v_ref, b_ref, o_ref, s_ref, cm_ref, *, rows):
    u = pl.program_id(2)
    n_blocks = rows // NA_Q_ROWS
    width = NA_HEADS * HEAD_DIM
    row_id = lax.broadcasted_iota(jnp.int32, (width, NA_NQ), 0)
    ones = jnp.ones((FLASH_L_ROWS, NA_NK), BF16)

    def key_base(blk):
        return jnp.clip(NA_Q_ROWS * blk - NA_WIN_H // 2, 0, rows - NA_K_ROWS)

    def scores(blk, q_blk, slot, e):
        variant = jnp.where(blk == 0, 0, jnp.where(blk == n_blocks - 1, 2, 1))
        start = pl.multiple_of(key_base(blk) * GRID_W, NA_NQ)
        k_w = k_ref[0, 0, pl.ds(start, NA_NK), :]
        mine = (row_id >= e * HEAD_DIM) & (row_id < (e + 1) * HEAD_DIM)
        q_e = jnp.where(mine, q_blk, jnp.zeros_like(q_blk))
        s_t = jnp.dot(k_w, q_e, preferred_element_type=F32) + b_ref[variant, e]
        s_ref[slot, e] = s_t
        cm_ref[slot, e] = jnp.max(s_t, axis=0, keepdims=True)

    def attend(blk, slot, e):
        chunk0 = key_base(blk) // NA_Q_ROWS
        s_t = s_ref[slot, e]
        p = jnp.exp2(s_t - cm_ref[slot, e]).astype(BF16)
        v_t = [v_ref[0, e, chunk0 + j] for j in range(NA_K_ROWS // NA_Q_ROWS)]
        v_ext = jnp.concatenate([jnp.concatenate(v_t, axis=1), ones], axis=0)
        acc = jnp.dot(v_ext, p, preferred_element_type=F32)
        return acc[:HEAD_DIM] / acc[HEAD_DIM:HEAD_DIM + 1]

    @pl.when(u == 0)
    def _():
        for e in range(NA_HEADS):
            scores(0, q_ref[0, 0, :, 0:NA_NQ], 0, e)

    def half(blk, slot, blk_next, q_next, row0):
        outs = []
        for e in range(NA_HEADS):
            scores(blk_next, q_next, 1 - slot, e)
            outs.append(attend(blk, slot, e))
        o_ref[0, 0, row0:row0 + NA_NQ, :] = jnp.concatenate(outs, axis=0).T.astype(BF16)

    def q_block(n):
        return q_ref[0, n // 2, :, (n % 2) * NA_NQ:(n % 2 + 1) * NA_NQ]

    for n in range(NA_BLOCKS - 1):
        half(NA_BLOCKS * u + n, n % 2, NA_BLOCKS * u + n + 1, q_block(n + 1), n * NA_NQ)
    u_next = jnp.minimum(u + 1, n_blocks // NA_BLOCKS - 1)
    half(NA_BLOCKS * u + NA_BLOCKS - 1, (NA_BLOCKS - 1) % 2, NA_BLOCKS * u_next,
         qn_ref[0, 0, :, 0:NA_NQ], (NA_BLOCKS - 1) * NA_NQ)


def _natten(q_t, k, v_t, bias):
    b, _, s, _ = k.shape
    rows = s // GRID_W
    n_blocks = rows // NA_Q_ROWS
    groups = N_HEADS // NA_HEADS
    width = NA_HEADS * HEAD_DIM
    steps = n_blocks // NA_BLOCKS
    tiles = NA_BLOCKS // 2
    assert rows >= NA_K_ROWS + NA_Q_ROWS
    return pl.pallas_call(
        functools.partial(_natten_kernel, rows=rows),
        grid=(groups, b, steps),
        in_specs=[
            pl.BlockSpec((1, tiles, width, 2 * NA_NQ), lambda hg, bi, u: (bi, u, hg, 0)),
            pl.BlockSpec((1, 1, width, 2 * NA_NQ),
                         lambda hg, bi, u: (bi, tiles * jnp.minimum(u + 1, steps - 1), hg, 0)),
            pl.BlockSpec((1, 1, s, width), lambda hg, bi, u: (bi, hg, 0, 0)),
            pl.BlockSpec((1, NA_HEADS, s // NA_NQ, HEAD_DIM, NA_NQ),
                         lambda hg, bi, u: (bi, hg, 0, 0, 0)),
            pl.BlockSpec((3, NA_HEADS, NA_NK, NA_NQ), lambda hg, bi, u: (0, hg, 0, 0),
                         pipeline_mode=pl.Buffered(1)),
        ],
        out_specs=pl.BlockSpec((1, 1, NA_BLOCKS * NA_NQ, width), lambda hg, bi, u: (bi, hg, u, 0)),
        out_shape=jax.ShapeDtypeStruct((b, groups, s, width), BF16),
        scratch_shapes=[pltpu.VMEM((2, NA_HEADS, NA_NK, NA_NQ), F32),
                        pltpu.VMEM((2, NA_HEADS, 1, NA_NQ), F32)],
        compiler_params=_params(3),
        name="natten_b",
    )(q_t, q_t, k, v_t, bias)


def _rope_tables_t(seq):
    t = jnp.arange(seq)
    row = (t // GRID_W).astype(F32)
    col = (t % GRID_W).astype(F32)
    inv = ROPE_THETA ** (-jnp.arange(0, AXIS_DIM, 2, dtype=F32) / AXIS_DIM)
    ang = jnp.concatenate([inv[:, None] * row[None, :], inv[:, None] * col[None, :]], axis=0)
    return jnp.cos(ang), jnp.sin(ang)


NA_ROW_OFFSET = (NA_WIN_H - 1, NA_WIN_H - 1 - NA_WIN_H // 2,
                 NA_WIN_H - 1 - (NA_K_ROWS - NA_Q_ROWS))


def _na_window_start(variant, a):
    return (0, a, NA_K_ROWS - NA_WIN_H)[variant]


def _natten_bias_kernel(even_ref, odd_ref, mask_ref, o_ref):
    lanes = 2 * GRID_W
    lane = lax.broadcasted_iota(jnp.int32, (GRID_W, lanes), 1)
    masked = jnp.full((GRID_W, lanes), -jnp.inf, F32)
    for variant in range(3):
        for i in range(NA_K_ROWS):
            for pair in range(NA_Q_ROWS // 2):
                rows_a = (2 * pair, 2 * pair + 1)
                inside = [0 <= i - _na_window_start(variant, a) < NA_WIN_H for a in rows_a]
                blk = masked
                if any(inside):
                    base = jnp.zeros((1, lanes), F32)
                    for a, ok, src in zip(rows_a, inside, (even_ref, odd_ref)):
                        if ok:
                            d = i - a + NA_ROW_OFFSET[variant]
                            base = base + src[0, d:d + 1, :]
                    blk = pltpu.roll(jnp.broadcast_to(base, (GRID_W, lanes)), 0, 1,
                                     stride=1, stride_axis=0) + mask_ref[...]
                    if not inside[0]:
                        blk = jnp.where(lane < GRID_W, masked, blk)
                    if not inside[1]:
                        blk = jnp.where(lane >= GRID_W, masked, blk)
                o_ref[variant, 0, i * GRID_W:(i + 1) * GRID_W, pair * lanes:(pair + 1) * lanes] = blk


def _natten_bias(rel_bias):
    table = rel_bias.astype(F32) * LOG2_E
    lanes = 2 * GRID_W
    m = jnp.arange(lanes)
    reach = NA_WIN_W - 1
    col_even = jnp.where(m < GRID_W, reach - m, reach - (m - lanes))
    col_odd = reach - (m - GRID_W)

    def place(col, keep):
        vals = jnp.take(table, jnp.clip(col, 0, NA_BIAS_W - 1), axis=2)
        vals = jnp.where((keep & (col >= 0) & (col < NA_BIAS_W))[None, None, :], vals, 0.0)
        return jnp.pad(vals, ((0, 0), (0, 1), (0, 0)))

    even = place(col_even, (m <= reach) | (m >= lanes - reach))
    odd = place(col_odd, (m >= GRID_W - reach) & (m <= GRID_W + reach))
    c = jnp.arange(GRID_W)[None, :]
    kc = jnp.arange(GRID_W)[:, None]
    c0 = jnp.clip(c - NA_WIN_W // 2, 0, GRID_W - NA_WIN_W)
    valid_c = (kc >= c0) & (kc < c0 + NA_WIN_W)
    mask = jnp.where(jnp.concatenate([valid_c, valid_c], axis=1), 0.0, -jnp.inf).astype(F32)
    return pl.pallas_call(
        _natten_bias_kernel,
        grid=(N_HEADS,),
        in_specs=[
            pl.BlockSpec((1, NA_BIAS_H + 1, lanes), lambda h: (h, 0, 0)),
            pl.BlockSpec((1, NA_BIAS_H + 1, lanes), lambda h: (h, 0, 0)),
            pl.BlockSpec((GRID_W, lanes), lambda h: (0, 0)),
        ],
        out_specs=pl.BlockSpec((3, 1, NA_NK, NA_NQ), lambda h: (0, h, 0, 0)),
        out_shape=jax.ShapeDtypeStruct((3, N_HEADS, NA_NK, NA_NQ), F32),
        compiler_params=_params(1),
        name="natten_bias",
    )(even, odd, mask)


def _trunk(x, p):
    _, s, _ = x.shape
    depth = p["norm_mix"].shape[0]
    for i in range(depth):
        j = i // 2
        g_mix = p["norm_mix"][i][None, :]
        if i % 2 == 0:
            q_t, k, v_t = _qkv_a(x, g_mix, p["a_w_qkv_t"][j], p["a_q_gain"][j], p["a_k_gain"][j],
                                 p["cos_t"], p["sin_t"])
            o = _flash(q_t, k, v_t)
            w_o = p["a_w_o"][j]
        else:
            q_t, k, v_t = _qkv_b(x, g_mix, p["b_w_qv_t"][j], p["b_w_k"][j])
            o = _natten(q_t, k, v_t, p["b_bias"][j])
            w_o = p["b_w_o"][j]
        x = _mlp(x, o, w_o, p["norm_mlp"][i][None, :], p["mlp_w_in"][i], p["mlp_w_out"][i],
                 p["norm_final"][None, :], final=(i == depth - 1))
    return x


def kernel(x_prompt, x_sample, norm_mix, norm_mlp, norm_final, a_w_qkv, a_q_norm, a_k_norm, a_w_o,
           b_w_qkv, b_rel_bias, b_w_o, mlp_w_in, mlp_w_out):
    scale = HEAD_DIM ** -0.5 * LOG2_E
    q_bound = jnp.max(jnp.abs(a_q_norm), axis=1) * scale
    k_bound = jnp.max(jnp.abs(a_k_norm), axis=1)
    ratio = k_bound / q_bound
    ratio = jnp.where(jnp.isfinite(ratio) & (ratio > 0), ratio, 1.0)
    balance = jnp.exp2(jnp.round(0.5 * jnp.log2(ratio)))[:, None]
    shared = {
        "norm_mix": norm_mix, "norm_mlp": norm_mlp, "norm_final": norm_final,
        "a_w_qkv_t": jnp.swapaxes(a_w_qkv, 1, 2).astype(BF16),
        "a_q_gain": jnp.broadcast_to((a_q_norm * scale * balance)[:, :, None],
                                     a_q_norm.shape + (ROW_TILE,)),
        "a_k_gain": jnp.broadcast_to((a_k_norm / balance)[:, :, None],
                                     a_k_norm.shape + (ROW_TILE,)),
        "a_w_o": a_w_o.astype(BF16),
        "b_w_qv_t": jnp.swapaxes(jnp.concatenate([b_w_qkv[:, :, :HD], b_w_qkv[:, :, 2 * HD:]],
                                                 axis=2), 1, 2).astype(BF16),
        "b_w_k": b_w_qkv[:, :, HD:2 * HD].astype(BF16),
        "b_w_o": b_w_o.astype(BF16),
        "mlp_w_in": mlp_w_in.astype(BF16),
        "mlp_w_out": mlp_w_out.astype(BF16),
        "b_bias": [_natten_bias(rb) for rb in b_rel_bias],
    }
    outs = []
    for x in (x_prompt, x_sample):
        cos_t, sin_t = _rope_tables_t(x.shape[1])
        outs.append(_trunk(x, dict(shared, cos_t=cos_t, sin_t=sin_t)))
    return tuple(outs)
```

```python
import functools

import jax
import jax.numpy as jnp
from jax import lax
from jax.experimental import pallas as pl
from jax.experimental.pallas import tpu as pltpu

D_MODEL = 1024
GRID_W = 64
N_HEADS = 16
HEAD_DIM = 64
A_KV_HEADS = 4
A_GROUP = N_HEADS // A_KV_HEADS
ROPE_THETA = 10000.0
AXIS_DIM = HEAD_DIM // 2
ROPE_HALF = AXIS_DIM // 2
NA_WIN_H = 8
NA_WIN_W = 16
NA_BIAS_H = 2 * NA_WIN_H - 1
NA_BIAS_W = 2 * NA_WIN_W - 1
D_FF = 4 * D_MODEL
NORM_EPS = 1e-6
LOG2_E = 1.4426950408889634
NA_SCORE_SCALE = HEAD_DIM ** -0.5 * LOG2_E
HD = N_HEADS * HEAD_DIM
A_QKV = (N_HEADS + 2 * A_KV_HEADS) * HEAD_DIM

ROW_TILE = 1024
QKV_B_ROW_TILE = 1024
MLP_ROW_TILE = 1024
FLASH_TQ = 256
FLASH_Q_TILES = 2
FLASH_TK = 512
FLASH_HEADS = 4
FLASH_L_ROWS = 16
NA_Q_ROWS = 4
NA_K_ROWS = 12
NA_HEADS = 4
NA_BLOCKS = 8
O_GROUPS = 4
O_GROUP_WIDTH = 256
NA_NQ = NA_Q_ROWS * GRID_W
NA_Q_TILE = 2 * NA_NQ
NA_NK = NA_K_ROWS * GRID_W
FF_CHUNK = 1024
VMEM_LIMIT_BYTES = 56 * 1024 * 1024

BF16 = jnp.bfloat16
F32 = jnp.float32
F8 = jnp.float8_e4m3fn
F8_MAX = 448.0
QK_DEPTH = 4 * HEAD_DIM


def _params(n_axes):
    return pltpu.CompilerParams(
        dimension_semantics=("arbitrary",) * n_axes, vmem_limit_bytes=VMEM_LIMIT_BYTES)


def _resident(shape):
    zeros = (0,) * len(shape)
    return pl.BlockSpec(shape, lambda *_: zeros, pipeline_mode=pl.Buffered(1))


def _split_f8(x):
    x = jnp.clip(x, -F8_MAX, F8_MAX)
    hi = x.astype(F8)
    lo = (x - hi.astype(F32)).astype(F8)
    return hi, lo


def _rms(x, gain):
    ms = jnp.mean(x * x, axis=-1, keepdims=True)
    return x * lax.rsqrt(ms + NORM_EPS) * gain


def _qkv_a_kernel(x_ref, g_ref, wt_ref, qg_ref, kg_ref, cos_ref, sin_ref,
                  q_ref, k_ref, v_ref, s_ref, *, tm):
    h = _rms(x_ref[0], g_ref[...]).astype(BF16)
    qkv_t = lax.dot_general(wt_ref[...], h, (((1,), (1,)), ((), ())),
                            preferred_element_type=F32)
    n_slots = N_HEADS + 2 * A_KV_HEADS
    s_ref[...] = qkv_t.reshape(n_slots, HEAD_DIM, tm)

    def norm_rope(lo, hi, gain_ref, store):
        t = s_ref[lo:hi]
        r = lax.rsqrt(jnp.sum(t * t, axis=1, keepdims=True) * (1.0 / HEAD_DIM) + NORM_EPS)
        for a in range(2):
            d1 = a * AXIS_DIM
            d2 = d1 + ROPE_HALF
            x1 = s_ref[lo:hi, d1:d1 + ROPE_HALF, :] * r * gain_ref[d1:d1 + ROPE_HALF, :]
            x2 = s_ref[lo:hi, d2:d2 + ROPE_HALF, :] * r * gain_ref[d2:d2 + ROPE_HALF, :]
            c = cos_ref[a * ROPE_HALF:(a + 1) * ROPE_HALF, :]
            s = sin_ref[a * ROPE_HALF:(a + 1) * ROPE_HALF, :]
            store(d1, x1 * c - x2 * s)
            store(d2, x2 * c + x1 * s)

    def store_q(d, val):
        s_ref[0:N_HEADS, d:d + ROPE_HALF, :] = val

    def store_k(d, val):
        s_ref[N_HEADS:N_HEADS + A_KV_HEADS, d:d + ROPE_HALF, :] = val

    norm_rope(0, N_HEADS, qg_ref, store_q)
    norm_rope(N_HEADS, N_HEADS + A_KV_HEADS, kg_ref, store_k)
    q_hi, q_lo = _split_f8(s_ref[0:N_HEADS])
    for part, val in enumerate((q_hi, q_hi, q_lo, q_lo)):
        q_ref[0, :, part * HEAD_DIM:(part + 1) * HEAD_DIM, :] = val
    for g in range(A_KV_HEADS):
        k_hi, k_lo = _split_f8(s_ref[N_HEADS + g])
        k_hi, k_lo = k_hi.astype(F32), k_lo.astype(F32)
        k_ext_t = jnp.concatenate([k_hi, k_lo, k_hi, k_lo], axis=0)
        k_ref[0, g] = k_ext_t.T.astype(F8)
        for c in range(tm // FLASH_TK):
            v_ref[0, g, c] = s_ref[N_HEADS + A_KV_HEADS + g, :,
                                   c * FLASH_TK:(c + 1) * FLASH_TK].astype(BF16)


def _qkv_a(x, gain, w_t, q_gain, k_gain, cos_t, sin_t):
    b, s, _ = x.shape
    tm = ROW_TILE
    n_slots = N_HEADS + 2 * A_KV_HEADS
    return pl.pallas_call(
        functools.partial(_qkv_a_kernel, tm=tm),
        grid=(b, s // tm),
        in_specs=[
            pl.BlockSpec((1, tm, D_MODEL), lambda bi, i: (bi, i, 0)),
            _resident((1, D_MODEL)),
            _resident((A_QKV, D_MODEL)),
            _resident((HEAD_DIM, tm)),
            _resident((HEAD_DIM, tm)),
            pl.BlockSpec((AXIS_DIM, tm), lambda bi, i: (0, i)),
            pl.BlockSpec((AXIS_DIM, tm), lambda bi, i: (0, i)),
        ],
        out_specs=[
            pl.BlockSpec((1, N_HEADS, QK_DEPTH, tm), lambda bi, i: (bi, 0, 0, i)),
            pl.BlockSpec((1, A_KV_HEADS, tm, QK_DEPTH), lambda bi, i: (bi, 0, i, 0)),
            pl.BlockSpec((1, A_KV_HEADS, tm // FLASH_TK, HEAD_DIM, FLASH_TK),
                         lambda bi, i: (bi, 0, i, 0, 0)),
        ],
        out_shape=[
            jax.ShapeDtypeStruct((b, N_HEADS, QK_DEPTH, s), F8),
            jax.ShapeDtypeStruct((b, A_KV_HEADS, s, QK_DEPTH), F8),
            jax.ShapeDtypeStruct((b, A_KV_HEADS, s // FLASH_TK, HEAD_DIM, FLASH_TK), BF16),
        ],
        scratch_shapes=[pltpu.VMEM((n_slots, HEAD_DIM, tm), F32)],
        compiler_params=_params(2),
        name="qkv_a",
    )(x, gain, w_t, q_gain, k_gain, cos_t, sin_t)


def _flash_kernel(q_ref, k_ref, v_ref, qn_ref, kn_ref, o_ref, s_ref, cm_ref, *, n_chunks, tq):
    first = (pl.program_id(0) == 0) & (pl.program_id(1) == 0) & (pl.program_id(2) == 0)

    def scores(c, slot, e, j, q_src=q_ref, k_src=k_ref):
        k_c = k_src[0, 0, c * FLASH_TK:(c + 1) * FLASH_TK, :]
        q_e = q_src[0, e, :, j * tq:(j + 1) * tq]
        s_t = jnp.dot(k_c, q_e, preferred_element_type=F32)
        s_ref[slot, e] = s_t
        cm_ref[slot, e] = jnp.max(s_t, axis=0, keepdims=True)

    ones = jnp.ones((FLASH_L_ROWS, FLASH_TK), BF16)

    def update(c, slot, e, stat):
        m, acc = stat
        s_t = s_ref[slot, e]
        m_new = jnp.maximum(m, cm_ref[slot, e])
        alpha = jnp.exp2(m - m_new)
        p = jnp.exp2(s_t - m_new).astype(BF16)
        v_ext = jnp.concatenate([v_ref[0, 0, c], ones], axis=0)
        acc = alpha * acc + jnp.dot(v_ext, p, preferred_element_type=F32)
        return m_new, acc

    @pl.when(first)
    def _():
        for e in range(FLASH_HEADS):
            scores(0, 0, e, 0)

    for j in range(FLASH_Q_TILES):
        stats = tuple((jnp.full((1, tq), -jnp.inf, F32),
                       jnp.zeros((HEAD_DIM + FLASH_L_ROWS, tq), F32)) for _ in range(FLASH_HEADS))
        for c in range(n_chunks):
            new = []
            for e in range(FLASH_HEADS):
                new.append(update(c, c % 2, e, stats[e]))
                if c + 1 < n_chunks:
                    scores(c + 1, (c + 1) % 2, e, j)
                elif j + 1 < FLASH_Q_TILES:
                    scores(0, 0, e, j + 1)
                else:
                    scores(0, 0, e, 0, qn_ref, kn_ref)
            stats = tuple(new)
        o_t = jnp.concatenate([acc[:HEAD_DIM] / acc[HEAD_DIM:HEAD_DIM + 1] for (_, acc) in stats],
                              axis=0)
        o_ref[0, 0, j * tq:(j + 1) * tq, :] = o_t.T.astype(BF16)


def _flash(q_t, k, v_t):
    b, _, _, s = q_t.shape
    tq = FLASH_TQ
    n_chunks = s // FLASH_TK
    pairs = N_HEADS // FLASH_HEADS
    per_kv = A_GROUP // FLASH_HEADS
    tq_step = tq * FLASH_Q_TILES
    n_q = s // tq_step
    assert n_chunks % 2 == 0

    def following(bi, hp, i):
        flat = jnp.minimum((bi * pairs + hp) * n_q + i + 1, b * pairs * n_q - 1)
        return flat // (pairs * n_q), (flat // n_q) % pairs, flat % n_q

    def q_next(bi, hp, i):
        nb, nh, ni = following(bi, hp, i)
        return nb, nh, 0, ni

    def k_next(bi, hp, i):
        nb, nh, _ = following(bi, hp, i)
        return nb, nh // per_kv, 0, 0

    return pl.pallas_call(
        functools.partial(_flash_kernel, n_chunks=n_chunks, tq=tq),
        grid=(b, pairs, n_q),
        in_specs=[
            pl.BlockSpec((1, FLASH_HEADS, QK_DEPTH, tq_step), lambda bi, hp, i: (bi, hp, 0, i)),
            pl.BlockSpec((1, 1, s, QK_DEPTH), lambda bi, hp, i: (bi, hp // per_kv, 0, 0)),
            pl.BlockSpec((1, 1, n_chunks, HEAD_DIM, FLASH_TK),
                         lambda bi, hp, i: (bi, hp // per_kv, 0, 0, 0)),
            pl.BlockSpec((1, FLASH_HEADS, QK_DEPTH, tq_step), q_next),
            pl.BlockSpec((1, 1, s, QK_DEPTH), k_next),
        ],
        out_specs=pl.BlockSpec((1, 1, tq_step, O_GROUP_WIDTH), lambda bi, hp, i: (bi, hp, i, 0)),
        out_shape=jax.ShapeDtypeStruct((b, O_GROUPS, s, O_GROUP_WIDTH), BF16),
        scratch_shapes=[pltpu.VMEM((2, FLASH_HEADS, FLASH_TK, tq), F32),
                        pltpu.VMEM((2, FLASH_HEADS, 1, tq), F32)],
        compiler_params=_params(3),
        name="flash_a",
    )(q_t, k, v_t, q_t, k)


def _mlp_kernel(x_ref, a_ref, wo_ref, g_ref, win_ref, wout_ref, gf_ref, o_ref, *, final):
    a = jnp.concatenate([a_ref[0, g] for g in range(O_GROUPS)], axis=1)
    x = x_ref[0] + jnp.dot(a, wo_ref[...], preferred_element_type=F32)
    h = _rms(x, g_ref[...]).astype(BF16)
    acc = x
    for f in range(D_FF // FF_CHUNK):
        u = jnp.dot(h, win_ref[:, f * FF_CHUNK:(f + 1) * FF_CHUNK], preferred_element_type=F32)
        u = jnp.maximum(u, 0.0)
        acc = acc + jnp.dot((u * u).astype(BF16), wout_ref[f * FF_CHUNK:(f + 1) * FF_CHUNK, :],
                            preferred_element_type=F32)
    if final:
        acc = _rms(acc, gf_ref[...])
    o_ref[0] = acc


def _mlp(x, attn, w_o, gain, w_in, w_out, gain_final, final):
    b, s, _ = x.shape
    tm = MLP_ROW_TILE
    return pl.pallas_call(
        functools.partial(_mlp_kernel, final=final),
        grid=(b, s // tm),
        in_specs=[
            pl.BlockSpec((1, tm, D_MODEL), lambda bi, i: (bi, i, 0)),
            pl.BlockSpec((1, O_GROUPS, tm, O_GROUP_WIDTH), lambda bi, i: (bi, 0, i, 0)),
            _resident((HD, D_MODEL)),
            _resident((1, D_MODEL)),
            _resident((D_MODEL, D_FF)),
            _resident((D_FF, D_MODEL)),
            _resident((1, D_MODEL)),
        ],
        out_specs=pl.BlockSpec((1, tm, D_MODEL), lambda bi, i: (bi, i, 0)),
        out_shape=jax.ShapeDtypeStruct((b, s, D_MODEL), F32),
        compiler_params=_params(2),
        name="mlp_final" if final else "mlp",
    )(x, attn, w_o, gain, w_in, w_out, gain_final)


def _qkv_b_kernel(x_ref, g_ref, wqv_t_ref, wk_ref, q_ref, k_ref, v_ref, *, tm):
    h = _rms(x_ref[0], g_ref[...]).astype(BF16)
    k = jnp.dot(h, wk_ref[...], preferred_element_type=F32).astype(BF16)
    width = NA_HEADS * HEAD_DIM
    for g in range(N_HEADS // NA_HEADS):
        k_ref[0, g] = k[:, g * width:(g + 1) * width]
    nt = (((1,), (1,)), ((), ()))
    q_t = lax.dot_general(wqv_t_ref[0:HD, :], h, nt, preferred_element_type=F32)
    for t in range(tm // NA_Q_TILE):
        q_ref[0, t] = (q_t[:, t * NA_Q_TILE:(t + 1) * NA_Q_TILE] * NA_SCORE_SCALE).astype(BF16)
    v_t = lax.dot_general(wqv_t_ref[HD:2 * HD, :], h, nt, preferred_element_type=F32)
    for hd in range(N_HEADS):
        for c in range(tm // NA_NQ):
            v_ref[0, hd, c] = v_t[hd * HEAD_DIM:(hd + 1) * HEAD_DIM,
                                  c * NA_NQ:(c + 1) * NA_NQ].astype(BF16)


def _qkv_b(x, gain, wqv_t, wk):
    b, s, _ = x.shape
    tm = QKV_B_ROW_TILE
    groups = N_HEADS // NA_HEADS
    width = NA_HEADS * HEAD_DIM
    return pl.pallas_call(
        functools.partial(_qkv_b_kernel, tm=tm),
        grid=(b, s // tm),
        in_specs=[
            pl.BlockSpec((1, tm, D_MODEL), lambda bi, i: (bi, i, 0)),
            _resident((1, D_MODEL)),
            _resident((2 * HD, D_MODEL)),
            _resident((D_MODEL, HD)),
        ],
        out_specs=[
            pl.BlockSpec((1, tm // NA_Q_TILE, HD, NA_Q_TILE), lambda bi, i: (bi, i, 0, 0)),
            pl.BlockSpec((1, groups, tm, width), lambda bi, i: (bi, 0, i, 0)),
            pl.BlockSpec((1, N_HEADS, tm // NA_NQ, HEAD_DIM, NA_NQ), lambda bi, i: (bi, 0, i, 0, 0)),
        ],
        out_shape=[
            jax.ShapeDtypeStruct((b, s // NA_Q_TILE, HD, NA_Q_TILE), BF16),
            jax.ShapeDtypeStruct((b, groups, s, width), BF16),
            jax.ShapeDtypeStruct((b, N_HEADS, s // NA_NQ, HEAD_DIM, NA_NQ), BF16),
        ],
        compiler_params=_params(2),
        name="qkv_b",
    )(x, gain, wqv_t, wk)


def _natten_kernel(q_ref, qn_ref, k_ref, v_ref, b_ref, o_ref, s_ref, cm_ref, *, rows):
    u = pl.program_id(2)
    n_blocks = rows // NA_Q_ROWS
    width = NA_HEADS * HEAD_DIM
    row_id = lax.broadcasted_iota(jnp.int32, (width, NA_NQ), 0)
    ones = jnp.ones((FLASH_L_ROWS, NA_NK), BF16)

    def key_base(blk):
        return jnp.clip(NA_Q_ROWS * blk - NA_WIN_H // 2, 0, rows - NA_K_ROWS)

    def scores(blk, q_blk, slot, e):
        variant = jnp.where(blk == 0, 0, jnp.where(blk == n_blocks - 1, 2, 1))
        start = pl.multiple_of(key_base(blk) * GRID_W, NA_NQ)
        k_w = k_ref[0, 0, pl.ds(start, NA_NK), :]
        mine = (row_id >= e * HEAD_DIM) & (row_id < (e + 1) * HEAD_DIM)
        q_e = jnp.where(mine, q_blk, jnp.zeros_like(q_blk))
        s_t = jnp.dot(k_w, q_e, preferred_element_type=F32) + b_ref[variant, e]
        s_ref[slot, e] = s_t
        cm_ref[slot, e] = jnp.max(s_t, axis=0, keepdims=True)

    def attend(blk, slot, e):
        chunk0 = key_base(blk) // NA_Q_ROWS
        s_t = s_ref[slot, e]
        p = jnp.exp2(s_t - cm_ref[slot, e]).astype(BF16)
        v_t = [v_ref[0, e, chunk0 + j] for j in range(NA_K_ROWS // NA_Q_ROWS)]
        v_ext = jnp.concatenate([jnp.concatenate(v_t, axis=1), ones], axis=0)
        acc = jnp.dot(v_ext, p, preferred_element_type=F32)
        return acc[:HEAD_DIM] / acc[HEAD_DIM:HEAD_DIM + 1]

    @pl.when(u == 0)
    def _():
        for e in range(NA_HEADS):
            scores(0, q_ref[0, 0, :, 0:NA_NQ], 0, e)

    def half(blk, slot, blk_next, q_next, row0):
        outs = []
        for e in range(NA_HEADS):
            scores(blk_next, q_next, 1 - slot, e)
            outs.append(attend(blk, slot, e))
        o_ref[0, 0, row0:row0 + NA_NQ, :] = jnp.concatenate(outs, axis=0).T.astype(BF16)

    def q_block(n):
        return q_ref[0, n // 2, :, (n % 2) * NA_NQ:(n % 2 + 1) * NA_NQ]

    for n in range(NA_BLOCKS - 1):
        half(NA_BLOCKS * u + n, n % 2, NA_BLOCKS * u + n + 1, q_block(n + 1), n * NA_NQ)
    u_next = jnp.minimum(u + 1, n_blocks // NA_BLOCKS - 1)
    half(NA_BLOCKS * u + NA_BLOCKS - 1, (NA_BLOCKS - 1) % 2, NA_BLOCKS * u_next,
         qn_ref[0, 0, :, 0:NA_NQ], (NA_BLOCKS - 1) * NA_NQ)


def _natten(q_t, k, v_t, bias):
    b, _, s, _ = k.shape
    rows = s // GRID_W
    n_blocks = rows // NA_Q_ROWS
    groups = N_HEADS // NA_HEADS
    width = NA_HEADS * HEAD_DIM
    steps = n_blocks // NA_BLOCKS
    tiles = NA_BLOCKS // 2
    assert rows >= NA_K_ROWS + NA_Q_ROWS
    return pl.pallas_call(
        functools.partial(_natten_kernel, rows=rows),
        grid=(groups, b, steps),
        in_specs=[
            pl.BlockSpec((1, tiles, width, 2 * NA_NQ), lambda hg, bi, u: (bi, u, hg, 0)),
            pl.BlockSpec((1, 1, width, 2 * NA_NQ),
                         lambda hg, bi, u: (bi, tiles * jnp.minimum(u + 1, steps - 1), hg, 0)),
            pl.BlockSpec((1, 1, s, width), lambda hg, bi, u: (bi, hg, 0, 0)),
            pl.BlockSpec((1, NA_HEADS, s // NA_NQ, HEAD_DIM, NA_NQ),
                         lambda hg, bi, u: (bi, hg, 0, 0, 0)),
            pl.BlockSpec((3, NA_HEADS, NA_NK, NA_NQ), lambda hg, bi, u: (0, hg, 0, 0),
                         pipeline_mode=pl.Buffered(1)),
        ],
        out_specs=pl.BlockSpec((1, 1, NA_BLOCKS * NA_NQ, width), lambda hg, bi, u: (bi, hg, u, 0)),
        out_shape=jax.ShapeDtypeStruct((b, groups, s, width), BF16),
        scratch_shapes=[pltpu.VMEM((2, NA_HEADS, NA_NK, NA_NQ), F32),
                        pltpu.VMEM((2, NA_HEADS, 1, NA_NQ), F32)],
        compiler_params=_params(3),
        name="natten_b",
    )(q_t, q_t, k, v_t, bias)


def _rope_tables_t(seq):
    t = jnp.arange(seq)
    row = (t // GRID_W).astype(F32)
    col = (t % GRID_W).astype(F32)
    inv = ROPE_THETA ** (-jnp.arange(0, AXIS_DIM, 2, dtype=F32) / AXIS_DIM)
    ang = jnp.concatenate([inv[:, None] * row[None, :], inv[:, None] * col[None, :]], axis=0)
    return jnp.cos(ang), jnp.sin(ang)


NA_ROW_OFFSET = (NA_WIN_H - 1, NA_WIN_H - 1 - NA_WIN_H // 2,
                 NA_WIN_H - 1 - (NA_K_ROWS - NA_Q_ROWS))


def _na_window_start(variant, a):
    return (0, a, NA_K_ROWS - NA_WIN_H)[variant]


def _natten_bias_kernel(even_ref, odd_ref, mask_ref, o_ref):
    lanes = 2 * GRID_W
    lane = lax.broadcasted_iota(jnp.int32, (GRID_W, lanes), 1)
    masked = jnp.full((GRID_W, lanes), -jnp.inf, F32)
    for variant in range(3):
        for i in range(NA_K_ROWS):
            for pair in range(NA_Q_ROWS // 2):
                rows_a = (2 * pair, 2 * pair + 1)
                inside = [0 <= i - _na_window_start(variant, a) < NA_WIN_H for a in rows_a]
                blk = masked
                if any(inside):
                    base = jnp.zeros((1, lanes), F32)
                    for a, ok, src in zip(rows_a, inside, (even_ref, odd_ref)):
                        if ok:
                            d = i - a + NA_ROW_OFFSET[variant]
                            base = base + src[0, d:d + 1, :]
                    blk = pltpu.roll(jnp.broadcast_to(base, (GRID_W, lanes)), 0, 1,
                                     stride=1, stride_axis=0) + mask_ref[...]
                    if not inside[0]:
                        blk = jnp.where(lane < GRID_W, masked, blk)
                    if not inside[1]:
                        blk = jnp.where(lane >= GRID_W, masked, blk)
                o_ref[variant, 0, i * GRID_W:(i + 1) * GRID_W, pair * lanes:(pair + 1) * lanes] = blk


def _natten_bias(rel_bias):
    table = rel_bias.astype(F32) * LOG2_E
    lanes = 2 * GRID_W
    m = jnp.arange(lanes)
    reach = NA_WIN_W - 1
    col_even = jnp.where(m < GRID_W, reach - m, reach - (m - lanes))
    col_odd = reach - (m - GRID_W)

    def place(col, keep):
        vals = jnp.take(table, jnp.clip(col, 0, NA_BIAS_W - 1), axis=2)
        vals = jnp.where((keep & (col >= 0) & (col < NA_BIAS_W))[None, None, :], vals, 0.0)
        return jnp.pad(vals, ((0, 0), (0, 1), (0, 0)))

    even = place(col_even, (m <= reach) | (m >= lanes - reach))
    odd = place(col_odd, (m >= GRID_W - reach) & (m <= GRID_W + reach))
    c = jnp.arange(GRID_W)[None, :]
    kc = jnp.arange(GRID_W)[:, None]
    c0 = jnp.clip(c - NA_WIN_W // 2, 0, GRID_W - NA_WIN_W)
    valid_c = (kc >= c0) & (kc < c0 + NA_WIN_W)
    mask = jnp.where(jnp.concatenate([valid_c, valid_c], axis=1), 0.0, -jnp.inf).astype(F32)
    return pl.pallas_call(
        _natten_bias_kernel,
        grid=(N_HEADS,),
        in_specs=[
            pl.BlockSpec((1, NA_BIAS_H + 1, lanes), lambda h: (h, 0, 0)),
            pl.BlockSpec((1, NA_BIAS_H + 1, lanes), lambda h: (h, 0, 0)),
            pl.BlockSpec((GRID_W, lanes), lambda h: (0, 0)),
        ],
        out_specs=pl.BlockSpec((3, 1, NA_NK, NA_NQ), lambda h: (0, h, 0, 0)),
        out_shape=jax.ShapeDtypeStruct((3, N_HEADS, NA_NK, NA_NQ), F32),
        compiler_params=_params(1),
        name="natten_bias",
    )(even, odd, mask)


def _trunk(x, p):
    _, s, _ = x.shape
    depth = p["norm_mix"].shape[0]
    for i in range(depth):
        j = i // 2
        g_mix = p["norm_mix"][i][None, :]
        if i % 2 == 0:
            q_t, k, v_t = _qkv_a(x, g_mix, p["a_w_qkv_t"][j], p["a_q_gain"][j], p["a_k_gain"][j],
                                 p["cos_t"], p["sin_t"])
            o = _flash(q_t, k, v_t)
            w_o = p["a_w_o"][j]
        else:
            q_t, k, v_t = _qkv_b(x, g_mix, p["b_w_qv_t"][j], p["b_w_k"][j])
            o = _natten(q_t, k, v_t, p["b_bias"][j])
            w_o = p["b_w_o"][j]
        x = _mlp(x, o, w_o, p["norm_mlp"][i][None, :], p["mlp_w_in"][i], p["mlp_w_out"][i],
                 p["norm_final"][None, :], final=(i == depth - 1))
    return x


def kernel(x_prompt, x_sample, norm_mix, norm_mlp, norm_final, a_w_qkv, a_q_norm, a_k_norm, a_w_o,
           b_w_qkv, b_rel_bias, b_w_o, mlp_w_in, mlp_w_out):
    scale = HEAD_DIM ** -0.5 * LOG2_E
    q_bound = jnp.max(jnp.abs(a_q_norm), axis=1) * scale
    k_bound = jnp.max(jnp.abs(a_k_norm), axis=1)
    ratio = k_bound / q_bound
    ratio = jnp.where(jnp.isfinite(ratio) & (ratio > 0), ratio, 1.0)
    balance = jnp.exp2(jnp.round(0.5 * jnp.log2(ratio)))[:, None]
    shared = {
        "norm_mix": norm_mix, "norm_mlp": norm_mlp, "norm_final": norm_final,
        "a_w_qkv_t": jnp.swapaxes(a_w_qkv, 1, 2).astype(BF16),
        "a_q_gain": jnp.broadcast_to((a_q_norm * scale * balance)[:, :, None],
                                     a_q_norm.shape + (ROW_TILE,)),
        "a_k_gain": jnp.broadcast_to((a_k_norm / balance)[:, :, None],
                                     a_k_norm.shape + (ROW_TILE,)),
        "a_w_o": a_w_o.astype(BF16),
        "b_w_qv_t": jnp.swapaxes(jnp.concatenate([b_w_qkv[:, :, :HD], b_w_qkv[:, :, 2 * HD:]],
                                                 axis=2), 1, 2).astype(BF16),
        "b_w_k": b_w_qkv[:, :, HD:2 * HD].astype(BF16),
        "b_w_o": b_w_o.astype(BF16),
        "mlp_w_in": mlp_w_in.astype(BF16),
        "mlp_w_out": mlp_w_out.astype(BF16),
        "b_bias": [_natten_bias(rb) for rb in b_rel_bias],
    }
    outs = []
    for x in (x_prompt, x_sample):
        cos_t, sin_t = _rope_tables_t(x.shape[1])
        outs.append(_trunk(x, dict(shared, cos_t=cos_t, sin_t=sin_t)))
    return tuple(outs)
```

```python
import functools

import jax
import jax.numpy as jnp
from jax import lax
from jax.experimental import pallas as pl
from jax.experimental.pallas import tpu as pltpu

D_MODEL = 1024
GRID_W = 64
N_HEADS = 16
HEAD_DIM = 64
A_KV_HEADS = 4
A_GROUP = N_HEADS // A_KV_HEADS
ROPE_THETA = 10000.0
AXIS_DIM = HEAD_DIM // 2
ROPE_HALF = AXIS_DIM // 2
NA_WIN_H = 8
NA_WIN_W = 16
NA_BIAS_H = 2 * NA_WIN_H - 1
NA_BIAS_W = 2 * NA_WIN_W - 1
D_FF = 4 * D_MODEL
NORM_EPS = 1e-6
LOG2_E = 1.4426950408889634
NA_SCORE_SCALE = HEAD_DIM ** -0.5 * LOG2_E
HD = N_HEADS * HEAD_DIM
A_QKV = (N_HEADS + 2 * A_KV_HEADS) * HEAD_DIM

ROW_TILE = 1024
QKV_B_ROW_TILE = 1024
MLP_ROW_TILE = 1024
FLASH_TQ = 256
FLASH_Q_TILES = 2
FLASH_TK = 512
FLASH_HEADS = 4
FLASH_L_ROWS = 16
NA_Q_ROWS = 4
NA_K_ROWS = 12
NA_HEADS = 4
NA_BLOCKS = 8
O_GROUPS = 4
O_GROUP_WIDTH = 256
NA_NQ = NA_Q_ROWS * GRID_W
NA_Q_TILE = 2 * NA_NQ
NA_NK = NA_K_ROWS * GRID_W
FF_CHUNK = 1024
VMEM_LIMIT_BYTES = 56 * 1024 * 1024

BF16 = jnp.bfloat16
F32 = jnp.float32
F8 = jnp.float8_e4m3fn
F8_MAX = 448.0
QK_DEPTH = 4 * HEAD_DIM


def _params(n_axes):
    return pltpu.CompilerParams(
        dimension_semantics=("arbitrary",) * n_axes, vmem_limit_bytes=VMEM_LIMIT_BYTES)


def _resident(shape):
    zeros = (0,) * len(shape)
    return pl.BlockSpec(shape, lambda *_: zeros, pipeline_mode=pl.Buffered(1))


def _split_f8(x):
    x = jnp.clip(x, -F8_MAX, F8_MAX)
    hi = x.astype(F8)
    lo = (x - hi.astype(F32)).astype(F8)
    return hi, lo


def _rms(x, gain):
    ms = jnp.mean(x * x, axis=-1, keepdims=True)
    return x * lax.rsqrt(ms + NORM_EPS) * gain


def _qkv_a_kernel(x_ref, g_ref, wt_ref, qg_ref, kg_ref, cos_ref, sin_ref,
                  q_ref, k_ref, v_ref, s_ref, *, tm):
    h = _rms(x_ref[0], g_ref[...]).astype(BF16)
    qkv_t = lax.dot_general(wt_ref[...], h, (((1,), (1,)), ((), ())),
                            preferred_element_type=F32)
    n_slots = N_HEADS + 2 * A_KV_HEADS
    s_ref[...] = qkv_t.reshape(n_slots, HEAD_DIM, tm)

    def norm_rope(lo, hi, gain_ref, store):
        t = s_ref[lo:hi]
        r = lax.rsqrt(jnp.sum(t * t, axis=1, keepdims=True) * (1.0 / HEAD_DIM) + NORM_EPS)
        for a in range(2):
            d1 = a * AXIS_DIM
            d2 = d1 + ROPE_HALF
            x1 = s_ref[lo:hi, d1:d1 + ROPE_HALF, :] * r * gain_ref[d1:d1 + ROPE_HALF, :]
            x2 = s_ref[lo:hi, d2:d2 + ROPE_HALF, :] * r * gain_ref[d2:d2 + ROPE_HALF, :]
            c = cos_ref[a * ROPE_HALF:(a + 1) * ROPE_HALF, :]
            s = sin_ref[a * ROPE_HALF:(a + 1) * ROPE_HALF, :]
            store(d1, x1 * c - x2 * s)
            store(d2, x2 * c + x1 * s)

    def store_q(d, val):
        s_ref[0:N_HEADS, d:d + ROPE_HALF, :] = val

    def store_k(d, val):
        s_ref[N_HEADS:N_HEADS + A_KV_HEADS, d:d + ROPE_HALF, :] = val

    norm_rope(0, N_HEADS, qg_ref, store_q)
    norm_rope(N_HEADS, N_HEADS + A_KV_HEADS, kg_ref, store_k)
    q_hi, q_lo = _split_f8(s_ref[0:N_HEADS])
    for part, val in enumerate((q_hi, q_hi, q_lo, q_lo)):
        q_ref[0, :, part * HEAD_DIM:(part + 1) * HEAD_DIM, :] = val
    for g in range(A_KV_HEADS):
        k_hi, k_lo = _split_f8(s_ref[N_HEADS + g])
        k_hi, k_lo = k_hi.astype(F32), k_lo.astype(F32)
        k_ext_t = jnp.concatenate([k_hi, k_lo, k_hi, k_lo], axis=0)
        k_ref[0, g] = k_ext_t.T.astype(F8)
        for c in range(tm // FLASH_TK):
            v_ref[0, g, c] = s_ref[N_HEADS + A_KV_HEADS + g, :,
                                   c * FLASH_TK:(c + 1) * FLASH_TK].astype(BF16)


def _qkv_a(x, gain, w_t, q_gain, k_gain, cos_t, sin_t):
    b, s, _ = x.shape
    tm = ROW_TILE
    n_slots = N_HEADS + 2 * A_KV_HEADS
    return pl.pallas_call(
        functools.partial(_qkv_a_kernel, tm=tm),
        grid=(b, s // tm),
        in_specs=[
            pl.BlockSpec((1, tm, D_MODEL), lambda bi, i: (bi, i, 0)),
            _resident((1, D_MODEL)),
            _resident((A_QKV, D_MODEL)),
            _resident((HEAD_DIM, tm)),
            _resident((HEAD_DIM, tm)),
            pl.BlockSpec((AXIS_DIM, tm), lambda bi, i: (0, i)),
            pl.BlockSpec((AXIS_DIM, tm), lambda bi, i: (0, i)),
        ],
        out_specs=[
            pl.BlockSpec((1, N_HEADS, QK_DEPTH, tm), lambda bi, i: (bi, 0, 0, i)),
            pl.BlockSpec((1, A_KV_HEADS, tm, QK_DEPTH), lambda bi, i: (bi, 0, i, 0)),
            pl.BlockSpec((1, A_KV_HEADS, tm // FLASH_TK, HEAD_DIM, FLASH_TK),
                         lambda bi, i: (bi, 0, i, 0, 0)),
        ],
        out_shape=[
            jax.ShapeDtypeStruct((b, N_HEADS, QK_DEPTH, s), F8),
            jax.ShapeDtypeStruct((b, A_KV_HEADS, s, QK_DEPTH), F8),
            jax.ShapeDtypeStruct((b, A_KV_HEADS, s // FLASH_TK, HEAD_DIM, FLASH_TK), BF16),
        ],
        scratch_shapes=[pltpu.VMEM((n_slots, HEAD_DIM, tm), F32)],
        compiler_params=_params(2),
        name="qkv_a",
    )(x, gain, w_t, q_gain, k_gain, cos_t, sin_t)


def _flash_kernel(q_ref, k_ref, v_ref, qn_ref, kn_ref, o_ref, s_ref, cm_ref, *, n_chunks, tq):
    first = (pl.program_id(0) == 0) & (pl.program_id(1) == 0) & (pl.program_id(2) == 0)

    def scores(c, slot, e, j, q_src=q_ref, k_src=k_ref):
        k_c = k_src[0, 0, c * FLASH_TK:(c + 1) * FLASH_TK, :]
        q_e = q_src[0, e, :, j * tq:(j + 1) * tq]
        s_t = jnp.dot(k_c, q_e, preferred_element_type=F32)
        s_ref[slot, e] = s_t
        cm_ref[slot, e] = jnp.max(s_t, axis=0, keepdims=True)

    ones = jnp.ones((FLASH_L_ROWS, FLASH_TK), BF16)

    def update(c, slot, e, stat):
        m, acc = stat
        s_t = s_ref[slot, e]
        m_new = jnp.maximum(m, cm_ref[slot, e])
        alpha = jnp.exp2(m - m_new)
        p = jnp.exp2(s_t - m_new).astype(BF16)
        v_ext = jnp.concatenate([v_ref[0, 0, c], ones], axis=0)
        acc = alpha * acc + jnp.dot(v_ext, p, preferred_element_type=F32)
        return m_new, acc

    @pl.when(first)
    def _():
        for e in range(FLASH_HEADS):
            scores(0, 0, e, 0)

    for j in range(FLASH_Q_TILES):
        stats = tuple((jnp.full((1, tq), -jnp.inf, F32),
                       jnp.zeros((HEAD_DIM + FLASH_L_ROWS, tq), F32)) for _ in range(FLASH_HEADS))
        for c in range(n_chunks):
            new = []
            for e in range(FLASH_HEADS):
                new.append(update(c, c % 2, e, stats[e]))
                if c + 1 < n_chunks:
                    scores(c + 1, (c + 1) % 2, e, j)
                elif j + 1 < FLASH_Q_TILES:
                    scores(0, 0, e, j + 1)
                else:
                    scores(0, 0, e, 0, qn_ref, kn_ref)
            stats = tuple(new)
        o_t = jnp.concatenate([acc[:HEAD_DIM] / acc[HEAD_DIM:HEAD_DIM + 1] for (_, acc) in stats],
                              axis=0)
        o_ref[0, 0, j * tq:(j + 1) * tq, :] = o_t.T.astype(BF16)


def _flash(q_t, k, v_t):
    b, _, _, s = q_t.shape
    tq = FLASH_TQ
    n_chunks = s // FLASH_TK
    pairs = N_HEADS // FLASH_HEADS
    per_kv = A_GROUP // FLASH_HEADS
    tq_step = tq * FLASH_Q_TILES
    n_q = s // tq_step
    assert n_chunks % 2 == 0

    def following(bi, hp, i):
        flat = jnp.minimum((bi * pairs + hp) * n_q + i + 1, b * pairs * n_q - 1)
        return flat // (pairs * n_q), (flat // n_q) % pairs, flat % n_q

    def q_next(bi, hp, i):
        nb, nh, ni = following(bi, hp, i)
        return nb, nh, 0, ni

    def k_next(bi, hp, i):
        nb, nh, _ = following(bi, hp, i)
        return nb, nh // per_kv, 0, 0

    return pl.pallas_call(
        functools.partial(_flash_kernel, n_chunks=n_chunks, tq=tq),
        grid=(b, pairs, n_q),
        in_specs=[
            pl.BlockSpec((1, FLASH_HEADS, QK_DEPTH, tq_step), lambda bi, hp, i: (bi, hp, 0, i)),
            pl.BlockSpec((1, 1, s, QK_DEPTH), lambda bi, hp, i: (bi, hp // per_kv, 0, 0)),
            pl.BlockSpec((1, 1, n_chunks, HEAD_DIM, FLASH_TK),
                         lambda bi, hp, i: (bi, hp // per_kv, 0, 0, 0)),
            pl.BlockSpec((1, FLASH_HEADS, QK_DEPTH, tq_step), q_next),
            pl.BlockSpec((1, 1, s, QK_DEPTH), k_next),
        ],
        out_specs=pl.BlockSpec((1, 1, tq_step, O_GROUP_WIDTH), lambda bi, hp, i: (bi, hp, i, 0)),
        out_shape=jax.ShapeDtypeStruct((b, O_GROUPS, s, O_GROUP_WIDTH), BF16),
        scratch_shapes=[pltpu.VMEM((2, FLASH_HEADS, FLASH_TK, tq), F32),
                        pltpu.VMEM((2, FLASH_HEADS, 1, tq), F32)],
        compiler_params=_params(3),
        name="flash_a",
    )(q_t, k, v_t, q_t, k)


def _mlp_kernel(x_ref, a_ref, wo_ref, g_ref, win_ref, wout_ref, gf_ref, o_ref, *, final):
    a = jnp.concatenate([a_ref[0, g] for g in range(O_GROUPS)], axis=1)
    x = x_ref[0] + jnp.dot(a, wo_ref[...], preferred_element_type=F32)
    h = _rms(x, g_ref[...]).astype(BF16)
    acc = x
    for f in range(D_FF // FF_CHUNK):
        u = jnp.dot(h, win_ref[:, f * FF_CHUNK:(f + 1) * FF_CHUNK], preferred_element_type=F32)
        u = jnp.maximum(u, 0.0)
        acc = acc + jnp.dot((u * u).astype(BF16), wout_ref[f * FF_CHUNK:(f + 1) * FF_CHUNK, :],
                            preferred_element_type=F32)
    if final:
        acc = _rms(acc, gf_ref[...])
    o_ref[0] = acc


def _mlp(x, attn, w_o, gain, w_in, w_out, gain_final, final):
    b, s, _ = x.shape
    tm = MLP_ROW_TILE
    return pl.pallas_call(
        functools.partial(_mlp_kernel, final=final),
        grid=(b, s // tm),
        in_specs=[
            pl.BlockSpec((1, tm, D_MODEL), lambda bi, i: (bi, i, 0)),
            pl.BlockSpec((1, O_GROUPS, tm, O_GROUP_WIDTH), lambda bi, i: (bi, 0, i, 0)),
            _resident((HD, D_MODEL)),
            _resident((1, D_MODEL)),
            _resident((D_MODEL, D_FF)),
            _resident((D_FF, D_MODEL)),
            _resident((1, D_MODEL)),
        ],
        out_specs=pl.BlockSpec((1, tm, D_MODEL), lambda bi, i: (bi, i, 0)),
        out_shape=jax.ShapeDtypeStruct((b, s, D_MODEL), F32),
        compiler_params=_params(2),
        name="mlp_final" if final else "mlp",
    )(x, attn, w_o, gain, w_in, w_out, gain_final)


def _qkv_b_kernel(x_ref, g_ref, wqv_t_ref, wk_ref, q_ref, k_ref, v_ref, *, tm):
    h = _rms(x_ref[0], g_ref[...]).astype(BF16)
    k = jnp.dot(h, wk_ref[...], preferred_element_type=F32).astype(BF16)
    width = NA_HEADS * HEAD_DIM
    for g in range(N_HEADS // NA_HEADS):
        k_ref[0, g] = k[:, g * width:(g + 1) * width]
    nt = (((1,), (1,)), ((), ()))
    q_t = lax.dot_general(wqv_t_ref[0:HD, :], h, nt, preferred_element_type=F32)
    for t in range(tm // NA_Q_TILE):
        q_ref[0, t] = (q_t[:, t * NA_Q_TILE:(t + 1) * NA_Q_TILE] * NA_SCORE_SCALE).astype(BF16)
    v_t = lax.dot_general(wqv_t_ref[HD:2 * HD, :], h, nt, preferred_element_type=F32)
    for hd in range(N_HEADS):
        for c in range(tm // NA_NQ):
            v_ref[0, hd, c] = v_t[hd * HEAD_DIM:(hd + 1) * HEAD_DIM,
                                  c * NA_NQ:(c + 1) * NA_NQ].astype(BF16)


def _qkv_b(x, gain, wqv_t, wk):
    b, s, _ = x.shape
    tm = QKV_B_ROW_TILE
    groups = N_HEADS // NA_HEADS
    width = NA_HEADS * HEAD_DIM
    return pl.pallas_call(
        functools.partial(_qkv_b_kernel, tm=tm),
        grid=(b, s // tm),
        in_specs=[
            pl.BlockSpec((1, tm, D_MODEL), lambda bi, i: (bi, i, 0)),
            _resident((1, D_MODEL)),
            _resident((2 * HD, D_MODEL)),
            _resident((D_MODEL, HD)),
        ],
        out_specs=[
            pl.BlockSpec((1, tm // NA_Q_TILE, HD, NA_Q_TILE), lambda bi, i: (bi, i, 0, 0)),
            pl.BlockSpec((1, groups, tm, width), lambda bi, i: (bi, 0, i, 0)),
            pl.BlockSpec((1, N_HEADS, tm // NA_NQ, HEAD_DIM, NA_NQ), lambda bi, i: (bi, 0, i, 0, 0)),
        ],
        out_shape=[
            jax.ShapeDtypeStruct((b, s // NA_Q_TILE, HD, NA_Q_TILE), BF16),
            jax.ShapeDtypeStruct((b, groups, s, width), BF16),
            jax.ShapeDtypeStruct((b, N_HEADS, s // NA_NQ, HEAD_DIM, NA_NQ), BF16),
        ],
        compiler_params=_params(2),
        name="qkv_b",
    )(x, gain, wqv_t, wk)


def _natten_kernel(q_ref, qn_ref, k_ref, v_ref, b_ref, o_ref, s_ref, cm_ref, *, rows):
    u = pl.program_id(2)
    n_blocks = rows // NA_Q_ROWS
    width = NA_HEADS * HEAD_DIM
    row_id = lax.broadcasted_iota(jnp.int32, (width, NA_NQ), 0)
    ones = jnp.ones((FLASH_L_ROWS, NA_NK), BF16)

    def key_base(blk):
        return jnp.clip(NA_Q_ROWS * blk - NA_WIN_H // 2, 0, rows - NA_K_ROWS)

    def scores(blk, q_blk, slot, e):
        variant = jnp.where(blk == 0, 0, jnp.where(blk == n_blocks - 1, 2, 1))
        start = pl.multiple_of(key_base(blk) * GRID_W, NA_NQ)
        k_w = k_ref[0, 0, pl.ds(start, NA_NK), :]
        mine = (row_id >= e * HEAD_DIM) & (row_id < (e + 1) * HEAD_DIM)
        q_e = jnp.where(mine, q_blk, jnp.zeros_like(q_blk))
        s_t = jnp.dot(k_w, q_e, preferred_element_type=F32) + b_ref[variant, e]
        s_ref[slot, e] = s_t
        cm_ref[slot, e] = jnp.max(s_t, axis=0, keepdims=True)

    def attend(blk, slot, e):
        chunk0 = key_base(blk) // NA_Q_ROWS
        s_t = s_ref[slot, e]
        p = jnp.exp2(s_t - cm_ref[slot, e]).astype(BF16)
        v_t = [v_ref[0, e, chunk0 + j] for j in range(NA_K_ROWS // NA_Q_ROWS)]
        v_ext = jnp.concatenate([jnp.concatenate(v_t, axis=1), ones], axis=0)
        acc = jnp.dot(v_ext, p, preferred_element_type=F32)
        return acc[:HEAD_DIM] / acc[HEAD_DIM:HEAD_DIM + 1]

    @pl.when(u == 0)
    def _():
        for e in range(NA_HEADS):
            scores(0, q_ref[0, 0, :, 0:NA_NQ], 0, e)

    def half(blk, slot, blk_next, q_next, row0):
        outs = []
        for e in range(NA_HEADS):
            outs.append(attend(blk, slot, e))
            scores(blk_next, q_next, 1 - slot, e)
        o_ref[0, 0, row0:row0 + NA_NQ, :] = jnp.concatenate(outs, axis=0).T.astype(BF16)

    def q_block(n):
        return q_ref[0, n // 2, :, (n % 2) * NA_NQ:(n % 2 + 1) * NA_NQ]

    for n in range(NA_BLOCKS - 1):
        half(NA_BLOCKS * u + n, n % 2, NA_BLOCKS * u + n + 1, q_block(n + 1), n * NA_NQ)
    u_next = jnp.minimum(u + 1, n_blocks // NA_BLOCKS - 1)
    half(NA_BLOCKS * u + NA_BLOCKS - 1, (NA_BLOCKS - 1) % 2, NA_BLOCKS * u_next,
         qn_ref[0, 0, :, 0:NA_NQ], (NA_BLOCKS - 1) * NA_NQ)


def _natten(q_t, k, v_t, bias):
    b, _, s, _ = k.shape
    rows = s // GRID_W
    n_blocks = rows // NA_Q_ROWS
    groups = N_HEADS // NA_HEADS
    width = NA_HEADS * HEAD_DIM
    steps = n_blocks // NA_BLOCKS
    tiles = NA_BLOCKS // 2
    assert rows >= NA_K_ROWS + NA_Q_ROWS
    return pl.pallas_call(
        functools.partial(_natten_kernel, rows=rows),
        grid=(groups, b, steps),
        in_specs=[
            pl.BlockSpec((1, tiles, width, 2 * NA_NQ), lambda hg, bi, u: (bi, u, hg, 0)),
            pl.BlockSpec((1, 1, width, 2 * NA_NQ),
                         lambda hg, bi, u: (bi, tiles * jnp.minimum(u + 1, steps - 1), hg, 0)),
            pl.BlockSpec((1, 1, s, width), lambda hg, bi, u: (bi, hg, 0, 0)),
            pl.BlockSpec((1, NA_HEADS, s // NA_NQ, HEAD_DIM, NA_NQ),
                         lambda hg, bi, u: (bi, hg, 0, 0, 0)),
            pl.BlockSpec((3, NA_HEADS, NA_NK, NA_NQ), lambda hg, bi, u: (0, hg, 0, 0),
                         pipeline_mode=pl.Buffered(1)),
        ],
        out_specs=pl.BlockSpec((1, 1, NA_BLOCKS * NA_NQ, width), lambda hg, bi, u: (bi, hg, u, 0)),
        out_shape=jax.ShapeDtypeStruct((b, groups, s, width), BF16),
        scratch_shapes=[pltpu.VMEM((2, NA_HEADS, NA_NK, NA_NQ), F32),
                        pltpu.VMEM((2, NA_HEADS, 1, NA_NQ), F32)],
        compiler_params=_params(3),
        name="natten_b",
    )(q_t, q_t, k, v_t, bias)


def _rope_tables_t(seq):
    t = jnp.arange(seq)
    row = (t // GRID_W).astype(F32)
    col = (t % GRID_W).astype(F32)
    inv = ROPE_THETA ** (-jnp.arange(0, AXIS_DIM, 2, dtype=F32) / AXIS_DIM)
    ang = jnp.concatenate([inv[:, None] * row[None, :], inv[:, None] * col[None, :]], axis=0)
    return jnp.cos(ang), jnp.sin(ang)


NA_ROW_OFFSET = (NA_WIN_H - 1, NA_WIN_H - 1 - NA_WIN_H // 2,
                 NA_WIN_H - 1 - (NA_K_ROWS - NA_Q_ROWS))


def _na_window_start(variant, a):
    return (0, a, NA_K_ROWS - NA_WIN_H)[variant]


def _natten_bias_kernel(even_ref, odd_ref, mask_ref, o_ref):
    lanes = 2 * GRID_W
    lane = lax.broadcasted_iota(jnp.int32, (GRID_W, lanes), 1)
    masked = jnp.full((GRID_W, lanes), -jnp.inf, F32)
    for variant in range(3):
        for i in range(NA_K_ROWS):
            for pair in range(NA_Q_ROWS // 2):
                rows_a = (2 * pair, 2 * pair + 1)
                inside = [0 <= i - _na_window_start(variant, a) < NA_WIN_H for a in rows_a]
                blk = masked
                if any(inside):
                    base = jnp.zeros((1, lanes), F32)
                    for a, ok, src in zip(rows_a, inside, (even_ref, odd_ref)):
                        if ok:
                            d = i - a + NA_ROW_OFFSET[variant]
                            base = base + src[0, d:d + 1, :]
                    blk = pltpu.roll(jnp.broadcast_to(base, (GRID_W, lanes)), 0, 1,
                                     stride=1, stride_axis=0) + mask_ref[...]
                    if not inside[0]:
                        blk = jnp.where(lane < GRID_W, masked, blk)
                    if not inside[1]:
                        blk = jnp.where(lane >= GRID_W, masked, blk)
                o_ref[variant, 0, i * GRID_W:(i + 1) * GRID_W, pair * lanes:(pair + 1) * lanes] = blk


def _natten_bias(rel_bias):
    table = rel_bias.astype(F32) * LOG2_E
    lanes = 2 * GRID_W
    m = jnp.arange(lanes)
    reach = NA_WIN_W - 1
    col_even = jnp.where(m < GRID_W, reach - m, reach - (m - lanes))
    col_odd = reach - (m - GRID_W)

    def place(col, keep):
        vals = jnp.take(table, jnp.clip(col, 0, NA_BIAS_W - 1), axis=2)
        vals = jnp.where((keep & (col >= 0) & (col < NA_BIAS_W))[None, None, :], vals, 0.0)
        return jnp.pad(vals, ((0, 0), (0, 1), (0, 0)))

    even = place(col_even, (m <= reach) | (m >= lanes - reach))
    odd = place(col_odd, (m >= GRID_W - reach) & (m <= GRID_W + reach))
    c = jnp.arange(GRID_W)[None, :]
    kc = jnp.arange(GRID_W)[:, None]
    c0 = jnp.clip(c - NA_WIN_W // 2, 0, GRID_W - NA_WIN_W)
    valid_c = (kc >= c0) & (kc < c0 + NA_WIN_W)
    mask = jnp.where(jnp.concatenate([valid_c, valid_c], axis=1), 0.0, -jnp.inf).astype(F32)
    return pl.pallas_call(
        _natten_bias_kernel,
        grid=(N_HEADS,),
        in_specs=[
            pl.BlockSpec((1, NA_BIAS_H + 1, lanes), lambda h: (h, 0, 0)),
            pl.BlockSpec((1, NA_BIAS_H + 1, lanes), lambda h: (h, 0, 0)),
            pl.BlockSpec((GRID_W, lanes), lambda h: (0, 0)),
        ],
        out_specs=pl.BlockSpec((3, 1, NA_NK, NA_NQ), lambda h: (0, h, 0, 0)),
        out_shape=jax.ShapeDtypeStruct((3, N_HEADS, NA_NK, NA_NQ), F32),
        compiler_params=_params(1),
        name="natten_bias",
    )(even, odd, mask)


def _trunk(x, p):
    _, s, _ = x.shape
    depth = p["norm_mix"].shape[0]
    for i in range(depth):
        j = i // 2
        g_mix = p["norm_mix"][i][None, :]
        if i % 2 == 0:
            q_t, k, v_t = _qkv_a(x, g_mix, p["a_w_qkv_t"][j], p["a_q_gain"][j], p["a_k_gain"][j],
                                 p["cos_t"], p["sin_t"])
            o = _flash(q_t, k, v_t)
            w_o = p["a_w_o"][j]
        else:
            q_t, k, v_t = _qkv_b(x, g_mix, p["b_w_qv_t"][j], p["b_w_k"][j])
            o = _natten(q_t, k, v_t, p["b_bias"][j])
            w_o = p["b_w_o"][j]
        x = _mlp(x, o, w_o, p["norm_mlp"][i][None, :], p["mlp_w_in"][i], p["mlp_w_out"][i],
                 p["norm_final"][None, :], final=(i == depth - 1))
    return x


def kernel(x_prompt, x_sample, norm_mix, norm_mlp, norm_final, a_w_qkv, a_q_norm, a_k_norm, a_w_o,
           b_w_qkv, b_rel_bias, b_w_o, mlp_w_in, mlp_w_out):
    scale = HEAD_DIM ** -0.5 * LOG2_E
    q_bound = jnp.max(jnp.abs(a_q_norm), axis=1) * scale
    k_bound = jnp.max(jnp.abs(a_k_norm), axis=1)
    ratio = k_bound / q_bound
    ratio = jnp.where(jnp.isfinite(ratio) & (ratio > 0), ratio, 1.0)
    balance = jnp.exp2(jnp.round(0.5 * jnp.log2(ratio)))[:, None]
    shared = {
        "norm_mix": norm_mix, "norm_mlp": norm_mlp, "norm_final": norm_final,
        "a_w_qkv_t": jnp.swapaxes(a_w_qkv, 1, 2).astype(BF16),
        "a_q_gain": jnp.broadcast_to((a_q_norm * scale * balance)[:, :, None],
                                     a_q_norm.shape + (ROW_TILE,)),
        "a_k_gain": jnp.broadcast_to((a_k_norm / balance)[:, :, None],
                                     a_k_norm.shape + (ROW_TILE,)),
        "a_w_o": a_w_o.astype(BF16),
        "b_w_qv_t": jnp.swapaxes(jnp.concatenate([b_w_qkv[:, :, :HD], b_w_qkv[:, :, 2 * HD:]],
                                                 axis=2), 1, 2).astype(BF16),
        "b_w_k": b_w_qkv[:, :, HD:2 * HD].astype(BF16),
        "b_w_o": b_w_o.astype(BF16),
        "mlp_w_in": mlp_w_in.astype(BF16),
        "mlp_w_out": mlp_w_out.astype(BF16),
        "b_bias": [_natten_bias(rb) for rb in b_rel_bias],
    }
    outs = []
    for x in (x_prompt, x_sample):
        cos_t, sin_t = _rope_tables_t(x.shape[1])
        outs.append(_trunk(x, dict(shared, cos_t=cos_t, sin_t=sin_t)))
    return tuple(outs)
```
